```python
import math
import jax, jax.numpy as jnp
from jax import lax
import numpy as np

D_MODEL = 1024
BATCH = 32
SEQ = 256
DEPTH = 4
DEC_BATCH = 2
DEC_SEQ = 2048
PAST_LEN = 512

GRID_W = 64
N_EV = (DEPTH + 1) // 2
N_OD = DEPTH // 2
D_FF = 4 * D_MODEL
ALPHA = (2.0 * DEPTH) ** 0.25
BETA = (8.0 * DEPTH) ** -0.25
LN_EPS = 1e-5
RMS_EPS = 1e-6
Q_BLOCK = 128
ROPE_THETA = 10000.0

HY_DIM = D_MODEL // 2
HY_ORDER = 2
HY_SHORT = 3
HY_BANDS = 16
HY_EMB = 2 * HY_BANDS + 1
HY_FILT_HID = 64
HY_DECAY_MIN = 3.0
HY_DECAY_MAX = 15.0

MLA_HEADS = 8
MLA_NOPE = 64
MLA_ROPE = 32
MLA_V = 64
MLA_Q_RANK = 256
MLA_KV_RANK = 128

CV_DIM = D_MODEL // 2
CV_WIDTH = 31

GQA_HEADS = 8
GQA_KV_HEADS = 2
GQA_HD = 64

EV_SPLITS = (3 * HY_DIM, 3 * HY_DIM + MLA_Q_RANK, 3 * HY_DIM + MLA_Q_RANK + MLA_KV_RANK)
EV_IN = EV_SPLITS[2] + MLA_ROPE
EV_OUT = HY_DIM + MLA_HEADS * MLA_V
OD_SPLITS = (2 * CV_DIM, 2 * CV_DIM + GQA_HEADS * GQA_HD, 2 * CV_DIM + (GQA_HEADS + GQA_KV_HEADS) * GQA_HD)
OD_IN = OD_SPLITS[2] + GQA_KV_HEADS * GQA_HD
OD_OUT = CV_DIM + GQA_HEADS * GQA_HD

kernel_name = 'hybrid_hyena_mla_conformer_gqa_dit_step'


def layer_norm(x, g, b):
    xf = x.astype(jnp.float32)
    mu = jnp.mean(xf, axis=-1, keepdims=True)
    var = jnp.mean(jnp.square(xf - mu), axis=-1, keepdims=True)
    return ((xf - mu) * lax.rsqrt(var + LN_EPS) * g + b).astype(x.dtype)


def rms_norm(x, g):
    xf = x.astype(jnp.float32)
    return (xf * lax.rsqrt(jnp.mean(xf * xf, axis=-1, keepdims=True) + RMS_EPS) * g).astype(x.dtype)


def depthwise_conv(x, w, b):
    k = w.shape[0]
    y = lax.conv_general_dilated(x, w[:, None, :].astype(x.dtype), window_strides=(1,),
                                 padding=[(k // 2, k // 2)],
                                 dimension_numbers=('NWC', 'WIO', 'NWC'),
                                 feature_group_count=x.shape[-1])
    return y + b


def grid_positions(n_tok):
    rows = n_tok // GRID_W
    row = jnp.repeat(jnp.arange(rows, dtype=jnp.float32), GRID_W)
    col = jnp.tile(jnp.arange(GRID_W, dtype=jnp.float32), rows)
    return row, col


def rope_axis(x, pos):
    r = x.shape[-1]
    inv = ROPE_THETA ** (-jnp.arange(0, r, 2, dtype=jnp.float32) / r)
    ang = pos[:, None] * inv[None, :]
    ang = jnp.concatenate([ang, ang], axis=-1)[None, :, None, :]
    xf = x.astype(jnp.float32)
    x1, x2 = jnp.split(xf, 2, axis=-1)
    rot = jnp.concatenate([-x2, x1], axis=-1)
    return (xf * jnp.cos(ang) + rot * jnp.sin(ang)).astype(x.dtype)


def rope_2d(x):
    row, col = grid_positions(x.shape[1])
    half = x.shape[-1] // 2
    return jnp.concatenate([rope_axis(x[..., :half], row), rope_axis(x[..., half:], col)], axis=-1)


def block_attention(q, k, v):
    b, lq, h, dk = q.shape
    kvh, dv = k.shape[2], v.shape[-1]
    g = h // kvh
    nblk = lq // Q_BLOCK
    scale = dk ** -0.5
    qb = q.reshape(b, nblk, Q_BLOCK, kvh, g, dk).transpose(1, 0, 2, 3, 4, 5)

    def one_block(qblk):
        s = jnp.einsum('bqhgd,bkhd->bhgqk', qblk, k, preferred_element_type=jnp.float32) * scale
        p = jax.nn.softmax(s, axis=-1)
        return jnp.einsum('bhgqk,bkhd->bqhgd', p.astype(v.dtype), v)

    o = lax.map(one_block, qb)
    return o.transpose(1, 0, 2, 3, 4, 5).reshape(b, lq, h * dv)


def hyena_filter_spectra(n_tok, w1, b1, w2, b2, w3, freq, decay):
    t = jnp.arange(n_tok, dtype=jnp.float32) / n_tok
    bands = jnp.arange(1, HY_BANDS + 1, dtype=jnp.float32)
    ang = 2.0 * math.pi * t[:, None] * bands[None, :]
    z = jnp.concatenate([t[:, None], jnp.cos(ang), jnp.sin(ang)], axis=-1)
    hid = jnp.sin(freq * (z @ w1 + b1))
    hid = jnp.sin(freq * (hid @ w2 + b2))
    h = ((hid @ w3) * jnp.exp(-t[:, None] * jnp.abs(decay))).astype(jnp.float32)
    h = h.reshape(n_tok, 2, HY_ORDER, HY_DIM)
    fwd, bwd = h[:, 0], h[:, 1]
    two_sided = jnp.concatenate([fwd, jnp.zeros_like(fwd[:1]), bwd[1:][::-1]], axis=0)
    return jnp.fft.rfft(two_sided, axis=0)


def fft_long_conv(z, spec):
    n = z.shape[1]
    zf = jnp.fft.rfft(z.astype(jnp.float32), n=2 * n, axis=1)
    return jnp.fft.irfft(zf * spec[None], n=2 * n, axis=1)[:, :n].astype(z.dtype)


def hyena(u, ev, i):
    n_tok = u.shape[1]
    u = depthwise_conv(u, ev['conv_w'][i], ev['conv_b'][i])
    x1, x2, v = jnp.split(u, 3, axis=-1)
    spec = hyena_filter_spectra(n_tok, ev['f_w1'][i], ev['f_b1'][i], ev['f_w2'][i], ev['f_b2'][i],
                                ev['f_w3'][i], ev['freq'][i], ev['decay'][i])
    z = v
    for n, gate in enumerate((x1, x2)):
        z = gate * (fft_long_conv(z, spec[:, n]) + ev['skip'][i, n] * z)
    return z


def mla_expand(ckv, k_rope, w_ukv):
    b, l, _ = ckv.shape
    kv = (ckv @ w_ukv).reshape(b, l, MLA_HEADS, MLA_NOPE + MLA_V)
    k_nope, v = kv[..., :MLA_NOPE], kv[..., MLA_NOPE:]
    k_r = jnp.broadcast_to(k_rope[:, :, None, :], (b, l, MLA_HEADS, MLA_ROPE))
    return jnp.concatenate([k_nope, k_r], axis=-1), v


def even_mixer(h, ev, i, ctx):
    u = h @ ev['w_in'][i]
    u_hy, cq, ckv, kr = jnp.split(u, EV_SPLITS, axis=-1)
    y_hy = hyena(u_hy, ev, i)
    b, l, _ = h.shape
    q = (rms_norm(cq, ev['q_norm_g'][i]) @ ev['w_uq'][i]).reshape(b, l, MLA_HEADS, MLA_NOPE + MLA_ROPE)
    ckv = rms_norm(ckv, ev['kv_norm_g'][i])
    if ctx is None:
        k, v = mla_expand(ckv, kr, ev['w_ukv'][i])
        state = (ckv, kr)
    else:
        q = jnp.concatenate([q[..., :MLA_NOPE], rope_2d(q[..., MLA_NOPE:])], axis=-1)
        kr_rot = rope_2d(kr[:, :, None, :])[:, :, 0, :]
        k_lat, v_lat = mla_expand(ckv, kr_rot, ev['w_ukv'][i])
        k_ctx, v_ctx = mla_expand(ctx[0], ctx[1], ev['w_ukv'][i])
        k = jnp.concatenate([k_ctx, k_lat], axis=1)
        v = jnp.concatenate([v_ctx, v_lat], axis=1)
        state = None
    y_mla = block_attention(q, k, v)
    y = jnp.concatenate([y_hy, y_mla], axis=-1) @ ev['w_out'][i]
    return y, state


def odd_mixer(h, od, i, ctx):
    u = h @ od['w_in'][i]
    u_cv, q, k, v = jnp.split(u, OD_SPLITS, axis=-1)
    a, gt = jnp.split(u_cv, 2, axis=-1)
    y_cv = depthwise_conv(a * jax.nn.sigmoid(gt), od['dw_w'][i], od['dw_b'][i])
    y_cv = jax.nn.silu(layer_norm(y_cv, od['cv_ln_g'][i], od['cv_ln_b'][i]))
    b, l, _ = h.shape
    q = rms_norm(q.reshape(b, l, GQA_HEADS, GQA_HD), od['q_norm_g'][i])
    k = rms_norm(k.reshape(b, l, GQA_KV_HEADS, GQA_HD), od['k_norm_g'][i])
    v = v.reshape(b, l, GQA_KV_HEADS, GQA_HD)
    if ctx is None:
        state = (k, v)
    else:
        q = rope_2d(q)
        k = jnp.concatenate([ctx[0], rope_2d(k)], axis=1)
        v = jnp.concatenate([ctx[1], v], axis=1)
        state = None
    y_at = block_attention(q, k, v)
    y = jnp.concatenate([y_cv, y_at], axis=-1) @ od['w_out'][i]
    return y, state


def trunk_layer(x, l, cond, ev, od, sh, ctx):
    mod = (jax.nn.silu(cond) @ sh['ada_w'][l] + sh['ada_b'][l])[:, None, :]
    sh1, sc1, g1, sh2, sc2, g2 = jnp.split(mod, 6, axis=-1)
    h = x * (1.0 + sc1) + sh1
    if l % 2 == 0:
        y, state = even_mixer(h, ev, l // 2, ctx)
    else:
        y, state = odd_mixer(h, od, l // 2, ctx)
    x = layer_norm(ALPHA * x + g1 * y, sh['ln_g'][l, 0], sh['ln_b'][l, 0])
    h = x * (1.0 + sc2) + sh2
    f = jnp.square(jax.nn.relu(h @ sh['w1'][l] + sh['b1'][l])) @ sh['w2'][l] + sh['b2'][l]
    x = layer_norm(ALPHA * x + g2 * f, sh['ln_g'][l, 1], sh['ln_b'][l, 1])
    return x, state


def setup_inputs(seed: int = 0) -> dict:
    key = jax.random.key(seed)
    keys = iter(jax.random.split(key, 64))

    def nrm(shape, scale):
        return jax.random.normal(next(keys), shape, jnp.float32) * scale

    def gain(shape):
        return 1.0 + nrm(shape, 0.1)

    d = D_MODEL
    return {
        'x_prompt': nrm((BATCH, SEQ, d), 1.0),
        'x_sample': nrm((DEC_BATCH, DEC_SEQ, d), 1.0),
        'cache_mla_ckv': nrm((DEC_BATCH, N_EV, PAST_LEN, MLA_KV_RANK), 1.0),
        'cache_mla_krope': nrm((DEC_BATCH, N_EV, PAST_LEN, MLA_ROPE), 1.0),
        'cache_gqa_k': nrm((DEC_BATCH, N_OD, PAST_LEN, GQA_KV_HEADS, GQA_HD), 1.0),
        'cache_gqa_v': nrm((DEC_BATCH, N_OD, PAST_LEN, GQA_KV_HEADS, GQA_HD), 1.0),
        'c': nrm((DEC_BATCH, d), 1.0),
        'c_ctx': nrm((d,), 1.0),
        'ev_w_in': nrm((N_EV, d, EV_IN), d ** -0.5),
        'hy_conv_w': nrm((N_EV, HY_SHORT, 3 * HY_DIM), HY_SHORT ** -0.5),
        'hy_conv_b': nrm((N_EV, 3 * HY_DIM), 0.02),
        'hy_filt_w1': nrm((N_EV, HY_EMB, HY_FILT_HID), HY_EMB ** -0.5),
        'hy_filt_b1': nrm((N_EV, HY_FILT_HID), 0.02),
        'hy_filt_w2': nrm((N_EV, HY_FILT_HID, HY_FILT_HID), HY_FILT_HID ** -0.5),
        'hy_filt_b2': nrm((N_EV, HY_FILT_HID), 0.02),
        'hy_filt_w3': nrm((N_EV, HY_FILT_HID, 2 * HY_ORDER * HY_DIM), 0.1 * HY_FILT_HID ** -0.5),
        'hy_sin_freq': gain((N_EV, HY_FILT_HID)),
        'hy_decay': jax.random.uniform(next(keys), (N_EV, 2 * HY_ORDER * HY_DIM), jnp.float32,
                                       HY_DECAY_MIN, HY_DECAY_MAX),
        'hy_skip': nrm((N_EV, HY_ORDER, HY_DIM), 1.0),
        'mla_q_norm_g': gain((N_EV, MLA_Q_RANK)),
        'mla_w_uq': nrm((N_EV, MLA_Q_RANK, MLA_HEADS * (MLA_NOPE + MLA_ROPE)), MLA_Q_RANK ** -0.5),
        'mla_kv_norm_g': gain((N_EV, MLA_KV_RANK)),
        'mla_w_ukv': nrm((N_EV, MLA_KV_RANK, MLA_HEADS * (MLA_NOPE + MLA_V)), MLA_KV_RANK ** -0.5),
        'ev_w_out': nrm((N_EV, EV_OUT, d), BETA * EV_OUT ** -0.5),
        'od_w_in': nrm((N_OD, d, OD_IN), d ** -0.5),
        'cv_dw_w': nrm((N_OD, CV_WIDTH, CV_DIM), CV_WIDTH ** -0.5),
        'cv_dw_b': nrm((N_OD, CV_DIM), 0.02),
        'cv_ln_g': gain((N_OD, CV_DIM)),
        'cv_ln_b': nrm((N_OD, CV_DIM), 0.02),
        'gqa_q_norm_g': gain((N_OD, GQA_HD)),
        'gqa_k_norm_g': gain((N_OD, GQA_HD)),
        'od_w_out': nrm((N_OD, OD_OUT, d), BETA * OD_OUT ** -0.5),
        'ada_w': nrm((DEPTH, d, 6 * d), 0.5 * d ** -0.5),
        'ada_b': nrm((DEPTH, 6 * d), 0.02),
        'ln_g': gain((DEPTH, 2, d)),
        'ln_b': nrm((DEPTH, 2, d), 0.02),
        'mlp_w1': nrm((DEPTH, d, D_FF), d ** -0.5),
        'mlp_b1': nrm((DEPTH, D_FF), 0.02),
        'mlp_w2': nrm((DEPTH, D_FF, d), BETA * D_FF ** -0.5),
        'mlp_b2': nrm((DEPTH, d), 0.02),
    }


def reference(x_prompt, x_sample, cache_mla_ckv, cache_mla_krope, cache_gqa_k, cache_gqa_v, c, c_ctx,
              ev_w_in, hy_conv_w, hy_conv_b, hy_filt_w1, hy_filt_b1, hy_filt_w2, hy_filt_b2, hy_filt_w3,
              hy_sin_freq, hy_decay, hy_skip, mla_q_norm_g, mla_w_uq, mla_kv_norm_g, mla_w_ukv, ev_w_out,
              od_w_in, cv_dw_w, cv_dw_b, cv_ln_g, cv_ln_b, gqa_q_norm_g, gqa_k_norm_g, od_w_out,
              ada_w, ada_b, ln_g, ln_b, mlp_w1, mlp_b1, mlp_w2, mlp_b2):
    ev = {'w_in': ev_w_in, 'conv_w': hy_conv_w, 'conv_b': hy_conv_b, 'f_w1': hy_filt_w1, 'f_b1': hy_filt_b1,
          'f_w2': hy_filt_w2, 'f_b2': hy_filt_b2, 'f_w3': hy_filt_w3, 'freq': hy_sin_freq, 'decay': hy_decay,
          'skip': hy_skip, 'q_norm_g': mla_q_norm_g, 'w_uq': mla_w_uq, 'kv_norm_g': mla_kv_norm_g,
          'w_ukv': mla_w_ukv, 'w_out': ev_w_out}
    od = {'w_in': od_w_in, 'dw_w': cv_dw_w, 'dw_b': cv_dw_b, 'cv_ln_g': cv_ln_g, 'cv_ln_b': cv_ln_b,
          'q_norm_g': gqa_q_norm_g, 'k_norm_g': gqa_k_norm_g, 'w_out': od_w_out}
    sh = {'ada_w': ada_w, 'ada_b': ada_b, 'ln_g': ln_g, 'ln_b': ln_b,
          'w1': mlp_w1, 'b1': mlp_b1, 'w2': mlp_w2, 'b2': mlp_b2}

    xp = x_prompt
    ckv_list, krope_list, k_list, v_list = [], [], [], []
    for l in range(DEPTH):
        xp, st = trunk_layer(xp, l, c_ctx[None, :], ev, od, sh, None)
        if l % 2 == 0:
            ckv_list.append(st[0])
            krope_list.append(st[1])
        else:
            k_list.append(st[0])
            v_list.append(st[1])
    y_prompt = xp
    new_mla_ckv = jnp.stack(ckv_list, axis=1)
    new_mla_krope = jnp.stack(krope_list, axis=1)
    new_gqa_k = jnp.stack(k_list, axis=1)
    new_gqa_v = jnp.stack(v_list, axis=1)

    xs = x_sample
    for l in range(DEPTH):
        i = l // 2
        if l % 2 == 0:
            ctx = (cache_mla_ckv[:, i], cache_mla_krope[:, i])
        else:
            ctx = (cache_gqa_k[:, i], cache_gqa_v[:, i])
        xs, _ = trunk_layer(xs, l, c, ev, od, sh, ctx)
    y_sample = xs

    return (y_prompt, y_sample, new_mla_ckv, new_mla_krope, new_gqa_k, new_gqa_v)
```

```python
import functools
import math

import jax
import jax.numpy as jnp
from jax import lax
from jax.experimental import pallas as pl
from jax.experimental.pallas import tpu as pltpu

F32 = jnp.float32
BF16 = jnp.bfloat16

DEPTH = 4
GRID_W = 64
ALPHA = (2.0 * DEPTH) ** 0.25
LN_EPS = 1e-5
RMS_EPS = 1e-6
ROPE_THETA = 10000.0

HY_DIM = 512
HY_ORDER = 2
HY_BANDS = 16
HY_EMB = 2 * HY_BANDS + 1
HY_FILT_HID = 64

MLA_HEADS = 8
MLA_NOPE = 64
MLA_ROPE = 32
MLA_V = 64
MLA_Q_RANK = 256
MLA_KV_RANK = 128

CV_DIM = 512

GQA_HEADS = 8
GQA_KV_HEADS = 2
GQA_HD = 64

LANES = 128
DFT_CHUNK = 512
HALF_CHUNK = DFT_CHUNK // 2
MIB = 2 ** 20


def _cparams(sem, vmem_mib):
    return pltpu.CompilerParams(dimension_semantics=sem, vmem_limit_bytes=vmem_mib * MIB)


def _layer_norm(z, g, b):
    mu = jnp.mean(z, axis=-1, keepdims=True)
    zc = z - mu
    var = jnp.mean(zc * zc, axis=-1, keepdims=True)
    return zc * lax.rsqrt(var + LN_EPS) * g + b


def _rms(x, g):
    return x * lax.rsqrt(jnp.mean(x * x, axis=-1, keepdims=True) + RMS_EPS) * g


def _seg_mean(sq, s_mat):
    hi = sq.astype(BF16)
    lo = (sq - hi.astype(F32)).astype(BF16)
    return (jnp.dot(hi, s_mat, preferred_element_type=F32)
            + jnp.dot(lo, s_mat, preferred_element_type=F32))


def _rope(x, cos, sin_signed, half):
    w = x.shape[1]
    lane = lax.broadcasted_iota(jnp.int32, x.shape, 1)
    first = (lane % (2 * half)) < half
    rot = jnp.where(first, pltpu.roll(x, w - half, 1), pltpu.roll(x, half, 1))
    return x * cos + rot * sin_signed


def _ada_kernel(c_ref, w_ref, b_ref, o_ref):
    c = c_ref[...]
    s = (c * jax.nn.sigmoid(c)).astype(BF16)
    o_ref[0] = jnp.dot(s, w_ref[0].astype(BF16), preferred_element_type=F32) + b_ref[0]


def _ada(cond8, ada_w, ada_b):
    depth, d, n = ada_w.shape
    tn = 1536
    return pl.pallas_call(
        _ada_kernel,
        grid=(depth, n // tn),
        in_specs=[pl.BlockSpec((8, d), lambda l, j: (0, 0)),
                  pl.BlockSpec((1, d, tn), lambda l, j: (l, 0, j)),
                  pl.BlockSpec((1, 1, tn), lambda l, j: (l, 0, j))],
        out_specs=pl.BlockSpec((1, 8, tn), lambda l, j: (l, 0, j)),
        out_shape=jax.ShapeDtypeStruct((depth, 8, n), F32),
        compiler_params=_cparams(("arbitrary", "arbitrary"), 32),
        name="ada_mod",
    )(cond8, ada_w, ada_b.reshape(depth, 1, n))


class _Tok:
    def __init__(self, bc, lc, bl, ll):
        self.bc, self.lc, self.bl, self.ll = bc, lc, bl, ll
        self.t_ctx = bc * lc
        self.t_lat = bl * ll
        self.t = self.t_ctx + self.t_lat

    def cond_map(self, tm):
        n_ctx, per_b = self.t_ctx // tm, self.ll // tm
        return lambda m: jnp.where(m < n_ctx, 0, 1 + (m - n_ctx) // per_b)

    def rope_map(self, tm):
        n_ctx, per_b = self.t_ctx // tm, self.ll // tm
        return lambda m: jnp.where(m < n_ctx, 0, 1 + (m - n_ctx) % per_b)


def _rope_tables(ll, tm, n_heads, head_w, rope_off, rope_dim, scale):
    half = rope_dim // 2
    quarter = half // 2
    t = jnp.arange(ll, dtype=F32)
    row = jnp.floor(t / GRID_W)
    col = t - row * GRID_W
    inv = ROPE_THETA ** (-jnp.arange(0, half, 2, dtype=F32) / half)
    ang_r = row[:, None] * inv[None, :]
    ang_c = col[:, None] * inv[None, :]
    ang = jnp.concatenate([ang_r, ang_r, ang_c, ang_c], axis=-1)
    sign = jnp.concatenate([-jnp.ones((quarter,), F32), jnp.ones((quarter,), F32)] * 2)
    cos_h = jnp.ones((ll, head_w), F32).at[:, rope_off:rope_off + rope_dim].set(jnp.cos(ang))
    sin_h = jnp.zeros((ll, head_w), F32).at[:, rope_off:rope_off + rope_dim].set(jnp.sin(ang) * sign)
    cos_l = jnp.tile(cos_h, (1, n_heads))
    sin_l = jnp.tile(sin_h, (1, n_heads))
    w = n_heads * head_w
    cos = jnp.concatenate([jnp.ones((tm, w), F32), cos_l], axis=0) * scale
    sin = jnp.concatenate([jnp.zeros((tm, w), F32), sin_l], axis=0) * scale
    return cos, sin


EV_W = 2048


def _even_in_kernel(x_ref, mod_ref, w_ref, gq_ref, wuq_ref, gkv_ref, cq_ref, sq_ref, ck_ref, sk_ref,
                    uhy_ref, q_ref, ckr_ref):
    h = (x_ref[...] * (1.0 + mod_ref[0, 1:2, :]) + mod_ref[0, 0:1, :]).astype(BF16)
    u = jnp.dot(h, w_ref[...], preferred_element_type=F32)
    n_hy = 3 * HY_DIM
    uhy_ref[...] = u[:, :n_hy]
    cqn = _rms(u[:, n_hy:n_hy + MLA_Q_RANK], gq_ref[...])
    q = jnp.dot(cqn.astype(BF16), wuq_ref[...], preferred_element_type=F32)
    q_ref[...] = _rope(q, cq_ref[...], sq_ref[...], MLA_ROPE // 4).astype(BF16)
    o = n_hy + MLA_Q_RANK
    ckr_ref[:, :MLA_KV_RANK] = _rms(u[:, o:o + MLA_KV_RANK], gkv_ref[...])
    ckr_ref[:, MLA_KV_RANK:] = _rope(u[:, o + MLA_KV_RANK:], ck_ref[...], sk_ref[...], MLA_ROPE // 4)


def _even_in(tok, x, mod, w_in_p, gq, wuq_p, gkv, tabs, tm):
    cq, sq, ck, sk = tabs
    d = x.shape[1]
    cm, rm = tok.cond_map(tm), tok.rope_map(tm)
    qw = MLA_HEADS * LANES
    row = lambda m: (m, 0)
    full = lambda m: (0, 0)
    return pl.pallas_call(
        _even_in_kernel,
        grid=(tok.t // tm,),
        in_specs=[pl.BlockSpec((tm, d), row),
                  pl.BlockSpec((1, 6, d), lambda m: (cm(m), 0, 0)),
                  pl.BlockSpec((d, EV_W), full),
                  pl.BlockSpec((1, MLA_Q_RANK), full),
                  pl.BlockSpec((MLA_Q_RANK, qw), full),
                  pl.BlockSpec((1, MLA_KV_RANK), full),
                  pl.BlockSpec((tm, qw), lambda m: (rm(m), 0)),
                  pl.BlockSpec((tm, qw), lambda m: (rm(m), 0)),
                  pl.BlockSpec((tm, LANES), lambda m: (rm(m), 0)),
                  pl.BlockSpec((tm, LANES), lambda m: (rm(m), 0))],
        out_specs=[pl.BlockSpec((tm, 3 * HY_DIM), row),
                   pl.BlockSpec((tm, qw), row),
                   pl.BlockSpec((tm, 2 * LANES), row)],
        out_shape=[jax.ShapeDtypeStruct((tok.t, 3 * HY_DIM), F32),
                   jax.ShapeDtypeStruct((tok.t, qw), BF16),
                   jax.ShapeDtypeStruct((tok.t, 2 * LANES), F32)],
        compiler_params=_cparams(("arbitrary",), 48),
        name="even_in",
    )(x, mod, w_in_p, gq, wuq_p, gkv, cq, sq, ck, sk)


def _odd_in_kernel(x_ref, mod_ref, w_ref, gq_ref, gk_ref, sq_mat_ref, sk_mat_ref, cq_ref, sq_ref, ck_ref, sk_ref,
                   ucv_ref, q_ref, kv_ref):
    h = (x_ref[...] * (1.0 + mod_ref[0, 1:2, :]) + mod_ref[0, 0:1, :]).astype(BF16)
    u = jnp.dot(h, w_ref[...], preferred_element_type=F32)
    n_cv = 2 * CV_DIM
    n_q = GQA_HEADS * GQA_HD
    n_k = GQA_KV_HEADS * GQA_HD
    ucv_ref[...] = u[:, :n_cv]
    q = u[:, n_cv:n_cv + n_q]
    qn = q * lax.rsqrt(_seg_mean(q * q, sq_mat_ref[...]) + RMS_EPS) * gq_ref[...]
    q_ref[...] = _rope(qn, cq_ref[...], sq_ref[...], GQA_HD // 4).astype(BF16)
    k = u[:, n_cv + n_q:n_cv + n_q + n_k]
    kn = k * lax.rsqrt(_seg_mean(k * k, sk_mat_ref[...]) + RMS_EPS) * gk_ref[...]
    kv_ref[:, :n_k] = _rope(kn, ck_ref[...], sk_ref[...], GQA_HD // 4)
    kv_ref[:, n_k:] = u[:, n_cv + n_q + n_k:]


def _odd_in(tok, x, mod, w_in, gq, gk, tabs, tm):
    cq, sq, ck, sk = tabs
    d, n = w_in.shape
    n_q = GQA_HEADS * GQA_HD
    n_k = GQA_KV_HEADS * GQA_HD
    cm, rm = tok.cond_map(tm), tok.rope_map(tm)

    def seg_mat(w):
        i = jnp.arange(w)
        return jnp.where((i[:, None] // GQA_HD) == (i[None, :] // GQA_HD), 1.0 / GQA_HD, 0.0).astype(BF16)

    row = lambda m: (m, 0)
    full = lambda m: (0, 0)
    return pl.pallas_call(
        _odd_in_kernel,
        grid=(tok.t // tm,),
        in_specs=[pl.BlockSpec((tm, d), row),
                  pl.BlockSpec((1, 6, d), lambda m: (cm(m), 0, 0)),
                  pl.BlockSpec((d, n), full),
                  pl.BlockSpec((1, n_q), full),
                  pl.BlockSpec((1, n_k), full),
                  pl.BlockSpec((n_q, n_q), full),
                  pl.BlockSpec((n_k, n_k), full),
                  pl.BlockSpec((tm, n_q), lambda m: (rm(m), 0)),
                  pl.BlockSpec((tm, n_q), lambda m: (rm(m), 0)),
                  pl.BlockSpec((tm, n_k), lambda m: (rm(m), 0)),
                  pl.BlockSpec((tm, n_k), lambda m: (rm(m), 0))],
        out_specs=[pl.BlockSpec((tm, 2 * CV_DIM), row),
                   pl.BlockSpec((tm, n_q), row),
                   pl.BlockSpec((tm, 2 * n_k), row)],
        out_shape=[jax.ShapeDtypeStruct((tok.t, 2 * CV_DIM), F32),
                   jax.ShapeDtypeStruct((tok.t, n_q), BF16),
                   jax.ShapeDtypeStruct((tok.t, 2 * n_k), F32)],
        compiler_params=_cparams(("arbitrary",), 48),
        name="odd_in",
    )(x, mod, w_in, gq, gk, seg_mat(n_q), seg_mat(n_k), cq, sq, ck, sk)


def _softmax_pv(q, k, v):
    s = lax.dot_general(q, k, (((1,), (1,)), ((), ())), preferred_element_type=F32)
    p = jnp.exp(s - jnp.max(s, axis=-1, keepdims=True))
    l = jnp.sum(p, axis=-1, keepdims=True)
    return jnp.dot(p.astype(BF16), v, preferred_element_type=F32) * (1.0 / l)


def _mla_attn_kernel(*refs, seg_lens, has_cache):
    if has_cache:
        q_ref, cckv_ref, ckr_ref, own_ref, wk_ref, wv_ref, o_ref, ck_scr, k_scr, v_scr = refs
    else:
        q_ref, own_ref, wk_ref, wv_ref, o_ref, ck_scr, k_scr, v_scr = refs
    p = pl.program_id(2)

    @pl.when((pl.program_id(1) == 0) & (p == 0))
    def _():
        off = 0
        if has_cache:
            n = seg_lens[0]
            ck_scr[0:n, :MLA_KV_RANK] = cckv_ref[0, 0].astype(BF16)
            ck_scr[0:n, MLA_KV_RANK:] = jnp.zeros((n, LANES), BF16)
            ck_scr[0:n, MLA_KV_RANK:MLA_KV_RANK + MLA_ROPE] = ckr_ref[0, 0].astype(BF16)
            off = n
        ck_scr[off:off + seg_lens[-1], :] = own_ref[...].astype(BF16)
        ck = ck_scr[...]
        for hp in range(MLA_HEADS // 2):
            k_scr[hp] = jnp.dot(ck, wk_ref[hp], preferred_element_type=F32).astype(BF16)
        for h in range(MLA_HEADS):
            v_scr[h] = jnp.dot(ck[:, :MLA_KV_RANK], wv_ref[h], preferred_element_type=F32).astype(BF16)

    kp = k_scr[p]
    o_ref[...] = (_softmax_pv(q_ref[:, :LANES], kp[:, :LANES], v_scr[2 * p])
                  + _softmax_pv(q_ref[:, LANES:], kp[:, LANES:], v_scr[2 * p + 1])).astype(o_ref.dtype)


def _mla_attn(q, ckr, wk, wv, tok_off, nb, lq, tq, cache=None):
    nq = lq // tq
    qb0 = tok_off // tq
    ob0 = tok_off // lq
    npair = MLA_HEADS // 2
    seg_lens = (lq,) if cache is None else (cache[0].shape[2], lq)
    lk = sum(seg_lens)
    in_specs = [pl.BlockSpec((tq, 2 * LANES), lambda b, i, p: (qb0 + b * nq + i, p))]
    args = [q]
    if cache is not None:
        cckv, ckr_c, layer = cache
        past = cckv.shape[2]
        in_specs += [pl.BlockSpec((1, 1, past, MLA_KV_RANK), lambda b, i, p: (b, layer, 0, 0)),
                     pl.BlockSpec((1, 1, past, MLA_ROPE), lambda b, i, p: (b, layer, 0, 0))]
        args += [cckv, ckr_c]
    in_specs += [pl.BlockSpec((lq, 2 * LANES), lambda b, i, p: (ob0 + b, 0)),
                 pl.BlockSpec((npair, 2 * LANES, 2 * LANES), lambda b, i, p: (0, 0, 0)),
                 pl.BlockSpec((MLA_HEADS, MLA_KV_RANK, LANES), lambda b, i, p: (0, 0, 0))]
    args += [ckr, wk, wv]
    return pl.pallas_call(
        functools.partial(_mla_attn_kernel, seg_lens=seg_lens, has_cache=cache is not None),
        grid=(nb, nq, npair),
        in_specs=in_specs,
        out_specs=pl.BlockSpec((tq, LANES), lambda b, i, p: (b * nq + i, p)),
        out_shape=jax.ShapeDtypeStruct((nb * lq, MLA_HEADS * MLA_V), BF16),
        scratch_shapes=[pltpu.VMEM((lk, 2 * LANES), BF16),
                        pltpu.VMEM((npair, lk, 2 * LANES), BF16),
                        pltpu.VMEM((MLA_HEADS, lk, LANES), BF16)],
        compiler_params=_cparams(("arbitrary", "arbitrary", "arbitrary"), 48),
        name="mla_attn_lat" if cache is not None else "mla_attn_ctx",
    )(*args)


def _gqa_attn_kernel(*refs, seg_lens, has_cache):
    if has_cache:
        q_ref, ck_ref, cv_ref, own_ref, sel_ref, o_ref, kv_scr, k_scr, v_scr = refs
    else:
        q_ref, own_ref, sel_ref, o_ref, kv_scr, k_scr, v_scr = refs
    p = pl.program_id(2)
    n_k = GQA_KV_HEADS * GQA_HD

    @pl.when((pl.program_id(1) == 0) & (p == 0))
    def _():
        off = 0
        if has_cache:
            n = seg_lens[0]
            kv_scr[0:n, :n_k] = ck_ref[0, 0].astype(BF16)
            kv_scr[0:n, n_k:] = cv_ref[0, 0].astype(BF16)
            off = n
        kv_scr[off:off + seg_lens[-1], :] = own_ref[...].astype(BF16)
        kk = kv_scr[:, :n_k]
        vv = kv_scr[:, n_k:]
        for s in range(2 * GQA_KV_HEADS):
            k_scr[s] = jnp.dot(kk, sel_ref[s], preferred_element_type=F32).astype(BF16)
            v_scr[s] = jnp.dot(vv, sel_ref[s], preferred_element_type=F32).astype(BF16)

    group_pairs = GQA_HEADS // GQA_KV_HEADS // 2
    s0 = 2 * (p // group_pairs)
    q = q_ref[...]
    o_ref[...] = (_softmax_pv(q, k_scr[s0], v_scr[s0])
                  + _softmax_pv(q, k_scr[s0 + 1], v_scr[s0 + 1])).astype(o_ref.dtype)


def _gqa_attn(q, kv, tok_off, nb, lq, tq, cache=None):
    nq = lq // tq
    qb0 = tok_off // tq
    ob0 = tok_off // lq
    npair = GQA_HEADS // 2
    n_k = GQA_KV_HEADS * GQA_HD
    seg_lens = (lq,) if cache is None else (cache[0].shape[2], lq)
    lk = sum(seg_lens)
    src = jnp.arange(n_k)[:, None]
    dst = jnp.arange(n_k)[None, :]
    sel = jnp.stack([jnp.where((src // GQA_HD == kvh) & (dst // GQA_HD == i) & (src % GQA_HD == dst % GQA_HD), 1.0, 0.0)
                     for kvh in range(GQA_KV_HEADS) for i in range(2)]).astype(BF16)
    in_specs = [pl.BlockSpec((tq, LANES), lambda b, i, p: (qb0 + b * nq + i, p))]
    args = [q]
    if cache is not None:
        ck, cv, layer = cache
        past = ck.shape[2]
        in_specs += [pl.BlockSpec((1, 1, past, n_k), lambda b, i, p: (b, layer, 0, 0)),
                     pl.BlockSpec((1, 1, past, n_k), lambda b, i, p: (b, layer, 0, 0))]
        args += [ck, cv]
    in_specs += [pl.BlockSpec((lq, 2 * n_k), lambda b, i, p: (ob0 + b, 0)),
                 pl.BlockSpec((2 * GQA_KV_HEADS, n_k, n_k), lambda b, i, p: (0, 0, 0))]
    args += [kv, sel]
    return pl.pallas_call(
        functools.partial(_gqa_attn_kernel, seg_lens=seg_lens, has_cache=cache is not None),
        grid=(nb, nq, npair),
        in_specs=in_specs,
        out_specs=pl.BlockSpec((tq, LANES), lambda b, i, p: (b * nq + i, p)),
        out_shape=jax.ShapeDtypeStruct((nb * lq, GQA_HEADS * GQA_HD), BF16),
        scratch_shapes=[pltpu.VMEM((lk, 2 * n_k), BF16),
                        pltpu.VMEM((2 * GQA_KV_HEADS, lk, n_k), BF16),
                        pltpu.VMEM((2 * GQA_KV_HEADS, lk, n_k), BF16)],
        compiler_params=_cparams(("arbitrary", "arbitrary", "arbitrary"), 48),
        name="gqa_attn_lat" if cache is not None else "gqa_attn_ctx",
    )(*args)


def _dft_mats(l):
    n = 2 * l
    r = jnp.arange(n, dtype=jnp.int32)
    within = r % DFT_CHUNK
    nyq = r == HALF_CHUNK
    f = jnp.where(nyq, l, (r // DFT_CHUNK) * HALF_CHUNK + within % HALF_CHUNK)
    is_im = (within >= HALF_CHUNK) & ~nyq
    s = jnp.arange(l, dtype=jnp.int32)
    ang = ((f[:, None] * s[None, :]) % n).astype(F32) * (2.0 * math.pi / n)
    fwd = jnp.where(is_im[:, None], -jnp.sin(ang), jnp.cos(ang))
    wgt = jnp.where((r == 0) | nyq, 1.0 / n, 2.0 / n)
    inv = (fwd * wgt[:, None]).T
    return fwd.astype(BF16), inv.astype(BF16)


def _spec_kernel(f_ref, hf_ref, hb_ref, p_ref, q_ref, p2_ref):
    tf = jnp.dot(f_ref[...], hf_ref[...], preferred_element_type=F32)
    tb = jnp.dot(f_ref[...], hb_ref[...], preferred_element_type=F32)
    re = tf[:HALF_CHUNK] + tb[:HALF_CHUNK]
    im = tf[HALF_CHUNK:] - tb[HALF_CHUNK:]
    nyq = tf[HALF_CHUNK:] + tb[HALF_CHUNK:]
    row0 = (lax.broadcasted_iota(jnp.int32, re.shape, 0) == 0) & (pl.program_id(1) == 0)
    p_ref[...] = re
    q_ref[...] = jnp.where(row0, 0.0, im)
    p2_ref[...] = jnp.where(row0, nyq, re)


def _filter_spectrum(fwd_mat, hfilt, cw):
    n, l = fwd_mat.shape
    tc = 512
    nct = cw // tc
    out = jax.ShapeDtypeStruct((n // 2, cw), F32)
    ospec = pl.BlockSpec((HALF_CHUNK, tc), lambda c, j: (j, c))
    return pl.pallas_call(
        _spec_kernel,
        grid=(nct, n // DFT_CHUNK),
        in_specs=[pl.BlockSpec((DFT_CHUNK, l), lambda c, j: (j, 0)),
                  pl.BlockSpec((l, tc), lambda c, j: (0, c)),
                  pl.BlockSpec((l, tc), lambda c, j: (0, nct + c))],
        out_specs=[ospec, ospec, ospec],
        out_shape=[out, out, out],
        compiler_params=_cparams(("arbitrary", "arbitrary"), 40),
        name="filter_spectrum",
    )(fwd_mat, hfilt, hfilt)


def _dft_conv_step(zbf_scr, acc_scr, f_ref, fi_ref, p_ref, q_ref, p2_ref, l, nseq):
    p, q, p2 = p_ref[...], q_ref[...], p2_ref[...]
    for b in range(nseq):
        rows = slice(b * l, (b + 1) * l)
        zc = jnp.dot(f_ref[...], zbf_scr[rows, :], preferred_element_type=F32)
        re, im = zc[:HALF_CHUNK], zc[HALF_CHUNK:]
        y = jnp.concatenate([re * p - im * q, re * q + im * p2], axis=0).astype(BF16)
        acc_scr[rows, :] += jnp.dot(fi_ref[...], y, preferred_element_type=F32)


def _short_conv(u, w_ref, b_ref, l):
    rows = u.shape[0]
    t = lax.broadcasted_iota(jnp.int32, u.shape, 0) % l
    prev = jnp.where(t == 0, 0.0, pltpu.roll(u, 1, 0))
    nxt = jnp.where(t == l - 1, 0.0, pltpu.roll(u, rows - 1, 0))
    return w_ref[0:1, :] * prev + w_ref[1:2, :] * u + w_ref[2:3, :] * nxt + b_ref[...]


def _hyena_kernel(*refs, l, nseq, nch, conv_z):
    if conv_z:
        (uz_ref, ug_ref, cwz_ref, cbz_ref, cwg_ref, cbg_ref, skip_ref,
         f_ref, fi_ref, p_ref, q_ref, p2_ref, o_ref, z_scr, g_scr, zbf_scr, acc_scr) = refs
    else:
        (uz_ref, ug_ref, cwg_ref, cbg_ref, skip_ref,
         f_ref, fi_ref, p_ref, q_ref, p2_ref, o_ref, z_scr, g_scr, zbf_scr, acc_scr) = refs
    j = pl.program_id(2)

    @pl.when(j == 0)
    def _():
        z = _short_conv(uz_ref[...], cwz_ref, cbz_ref, l) if conv_z else uz_ref[...]
        z_scr[...] = z
        zbf_scr[...] = z.astype(BF16)
        g_scr[...] = _short_conv(ug_ref[...], cwg_ref, cbg_ref, l)
        acc_scr[...] = jnp.zeros_like(acc_scr)

    _dft_conv_step(zbf_scr, acc_scr, f_ref, fi_ref, p_ref, q_ref, p2_ref, l, nseq)

    @pl.when(j == nch - 1)
    def _():
        o_ref[...] = (g_scr[...] * (acc_scr[...] + skip_ref[...] * z_scr[...])).astype(o_ref.dtype)


def _hyena_stage(z_src, z_col, u_hy, g_col, conv_w, conv_b, skip, mats, spec, spec_col,
                 tok_off, nb, l, nseq, tc, conv_z, out_dtype):
    fwd_mat, inv_mat = mats
    p_arr, q_arr, p2_arr = spec
    n = 2 * l
    nch = n // DFT_CHUNK
    nct = HY_DIM // tc
    rows = nseq * l
    rb0 = tok_off // rows
    zrb0 = rb0 if conv_z else 0
    cpb = HY_DIM // tc
    cvec = lambda g: pl.BlockSpec((1, tc), lambda b, c, j: (0, g * cpb + c))
    in_specs = [pl.BlockSpec((rows, tc), lambda b, c, j: (zrb0 + b, z_col * cpb + c)),
                pl.BlockSpec((rows, tc), lambda b, c, j: (rb0 + b, g_col * cpb + c))]
    args = [z_src, u_hy]
    if conv_z:
        in_specs += [pl.BlockSpec((3, tc), lambda b, c, j: (0, z_col * cpb + c)), cvec(z_col)]
        args += [conv_w, conv_b]
    in_specs += [pl.BlockSpec((3, tc), lambda b, c, j: (0, g_col * cpb + c)), cvec(g_col),
                 pl.BlockSpec((1, tc), lambda b, c, j: (0, c)),
                 pl.BlockSpec((DFT_CHUNK, l), lambda b, c, j: (j, 0)),
                 pl.BlockSpec((l, DFT_CHUNK), lambda b, c, j: (0, j))]
    args += [conv_w, conv_b, skip, fwd_mat, inv_mat]
    sspec = pl.BlockSpec((HALF_CHUNK, tc), lambda b, c, j: (j, spec_col * cpb + c))
    in_specs += [sspec, sspec, sspec]
    args += [p_arr, q_arr, p2_arr]
    return pl.pallas_call(
        functools.partial(_hyena_kernel, l=l, nseq=nseq, nch=nch, conv_z=conv_z),
        grid=(nb // nseq, nct, nch),
        in_specs=in_specs,
        out_specs=pl.BlockSpec((rows, tc), lambda b, c, j: (b, c)),
        out_shape=jax.ShapeDtypeStruct((nb * l, HY_DIM), out_dtype),
        scratch_shapes=[pltpu.VMEM((rows, tc), F32), pltpu.VMEM((rows, tc), F32),
                        pltpu.VMEM((rows, tc), BF16), pltpu.VMEM((rows, tc), F32)],
        compiler_params=_cparams(("arbitrary", "arbitrary", "arbitrary"), 48),
        name="hyena_stage",
    )(*args)


def _conformer_kernel(ua_ref, ug_ref, b_ref, lng_ref, lnb_ref, f_ref, fi_ref, p_ref, q_ref, p2_ref, o_ref,
                      zbf_scr, acc_scr, *, l, nseq, nch):
    j = pl.program_id(1)

    @pl.when(j == 0)
    def _():
        zbf_scr[...] = (ua_ref[...] * jax.nn.sigmoid(ug_ref[...])).astype(BF16)
        acc_scr[...] = jnp.zeros_like(acc_scr)

    _dft_conv_step(zbf_scr, acc_scr, f_ref, fi_ref, p_ref, q_ref, p2_ref, l, nseq)

    @pl.when(j == nch - 1)
    def _():
        y = _layer_norm(acc_scr[...] + b_ref[...], lng_ref[...], lnb_ref[...])
        o_ref[...] = (y * jax.nn.sigmoid(y)).astype(o_ref.dtype)


def _conformer(u_cv, bias, ln_g, ln_b, mats, spec, tok_off, nb, l, nseq):
    fwd_mat, inv_mat = mats
    n = 2 * l
    nch = n // DFT_CHUNK
    rows = nseq * l
    rb0 = tok_off // rows
    c = CV_DIM
    vec = pl.BlockSpec((1, c), lambda b, j: (0, 0))
    sspec = pl.BlockSpec((HALF_CHUNK, c), lambda b, j: (j, 0))
    return pl.pallas_call(
        functools.partial(_conformer_kernel, l=l, nseq=nseq, nch=nch),
        grid=(nb // nseq, nch),
        in_specs=[pl.BlockSpec((rows, c), lambda b, j: (rb0 + b, 0)),
                  pl.BlockSpec((rows, c), lambda b, j: (rb0 + b, 1)),
                  vec, vec, vec,
                  pl.BlockSpec((DFT_CHUNK, l), lambda b, j: (j, 0)),
                  pl.BlockSpec((l, DFT_CHUNK), lambda b, j: (0, j)),
                  sspec, sspec, sspec],
        out_specs=pl.BlockSpec((rows, c), lambda b, j: (b, 0)),
        out_shape=jax.ShapeDtypeStruct((nb * l, c), BF16),
        scratch_shapes=[pltpu.VMEM((rows, c), BF16), pltpu.VMEM((rows, c), F32)],
        compiler_params=_cparams(("arbitrary", "arbitrary"), 56),
        name="conformer",
    )(u_cv, u_cv, bias, ln_g, ln_b, fwd_mat, inv_mat, *spec)


def _filter_kernel(z_ref, w1_ref, b1_ref, w2_ref, b2_ref, w3_ref, freq_ref, decay_ref, o_ref):
    hp = lax.Precision.HIGHEST
    z = z_ref[...]
    freq = freq_ref[...]
    hid = jnp.sin(freq * (jnp.dot(z, w1_ref[...], precision=hp, preferred_element_type=F32) + b1_ref[...]))
    hid = jnp.sin(freq * (jnp.dot(hid, w2_ref[...], precision=hp, preferred_element_type=F32) + b2_ref[...]))
    h = jnp.dot(hid, w3_ref[...], precision=hp, preferred_element_type=F32)
    h = h * jnp.exp(-z[:, 0:1] * jnp.abs(decay_ref[...]))
    n_fwd = h.shape[1] // 2
    row = lax.broadcasted_iota(jnp.int32, h.shape, 0) + pl.program_id(0) * h.shape[0]
    col = lax.broadcasted_iota(jnp.int32, h.shape, 1)
    o_ref[...] = jnp.where((row == 0) & (col >= n_fwd), 0.0, h).astype(o_ref.dtype)


def _hyena_filter(l, w1, b1, w2, b2, w3, freq, decay):
    t = jnp.arange(l, dtype=F32) / l
    bands = jnp.arange(1, HY_BANDS + 1, dtype=F32)
    ang = 2.0 * math.pi * t[:, None] * bands[None, :]
    z = jnp.concatenate([t[:, None], jnp.cos(ang), jnp.sin(ang), jnp.zeros((l, LANES - HY_EMB), F32)], axis=-1)
    w1p = jnp.concatenate([w1, jnp.zeros((LANES - HY_EMB, HY_FILT_HID), F32)], axis=0)
    nout = w3.shape[1]
    tl = 256
    full = lambda i: (0, 0)
    return pl.pallas_call(
        _filter_kernel,
        grid=(l // tl,),
        in_specs=[pl.BlockSpec((tl, LANES), lambda i: (i, 0)),
                  pl.BlockSpec((LANES, HY_FILT_HID), full),
                  pl.BlockSpec((1, HY_FILT_HID), full),
                  pl.BlockSpec((HY_FILT_HID, HY_FILT_HID), full),
                  pl.BlockSpec((1, HY_FILT_HID), full),
                  pl.BlockSpec((HY_FILT_HID, nout), full),
                  pl.BlockSpec((1, HY_FILT_HID), full),
                  pl.BlockSpec((1, nout), full)],
        out_specs=pl.BlockSpec((tl, nout), lambda i: (i, 0)),
        out_shape=jax.ShapeDtypeStruct((l, nout), BF16),
        compiler_params=_cparams(("arbitrary",), 32),
        name="hyena_filter",
    )(z, w1p, b1[None], w2, b2[None], w3, freq[None], decay[None])


def _out_proj_kernel(ya_ref, yb_ref, x_ref, mod_ref, wa_ref, wb_ref, g_ref, b_ref, o_ref):
    y = (jnp.dot(ya_ref[...], wa_ref[...], preferred_element_type=F32)
         + jnp.dot(yb_ref[...], wb_ref[...], preferred_element_type=F32))
    z = ALPHA * x_ref[...] + mod_ref[0, 2:3, :] * y
    o_ref[...] = _layer_norm(z, g_ref[...], b_ref[...])


def _out_proj(tok, ya, yb, x, mod, w_out, ln_g, ln_b, tm):
    d = x.shape[1]
    ka, kb = ya.shape[1], yb.shape[1]
    cm = tok.cond_map(tm)
    row = lambda m: (m, 0)
    full = lambda m: (0, 0)
    return pl.pallas_call(
        _out_proj_kernel,
        grid=(tok.t // tm,),
        in_specs=[pl.BlockSpec((tm, ka), row),
                  pl.BlockSpec((tm, kb), row),
                  pl.BlockSpec((tm, d), row),
                  pl.BlockSpec((1, 6, d), lambda m: (cm(m), 0, 0)),
                  pl.BlockSpec((ka, d), lambda m: (0, 0)),
                  pl.BlockSpec((kb, d), lambda m: (ka // kb, 0)),
                  pl.BlockSpec((1, d), full),
                  pl.BlockSpec((1, d), full)],
        out_specs=pl.BlockSpec((tm, d), row),
        out_shape=jax.ShapeDtypeStruct((tok.t, d), F32),
        compiler_params=_cparams(("arbitrary",), 40),
        name="out_proj_ln",
    )(ya, yb, x, mod, w_out, w_out, ln_g, ln_b)


def _mlp_kernel(x_ref, mod_ref, w1_ref, b1_ref, w2_ref, b2_ref, g_ref, b_ref, o_ref, h_scr, acc_scr, *, nf):
    f = pl.program_id(1)

    @pl.when(f == 0)
    def _():
        h_scr[...] = (x_ref[...] * (1.0 + mod_ref[0, 4:5, :]) + mod_ref[0, 3:4, :]).astype(BF16)
        acc_scr[...] = jnp.zeros_like(acc_scr)

    a = jnp.maximum(jnp.dot(h_scr[...], w1_ref[...], preferred_element_type=F32) + b1_ref[...], 0.0)
    acc_scr[...] += jnp.dot((a * a).astype(BF16), w2_ref[...], preferred_element_type=F32)

    @pl.when(f == nf - 1)
    def _():
        z = ALPHA * x_ref[...] + mod_ref[0, 5:6, :] * (acc_scr[...] + b2_ref[...])
        o_ref[...] = _layer_norm(z, g_ref[...], b_ref[...])


def _mlp(tok, x, mod, w1, b1, w2, b2, ln_g, ln_b, tm, tf):
    d = x.shape[1]
    dff = w1.shape[1]
    nf = dff // tf
    cm = tok.cond_map(tm)
    return pl.pallas_call(
        functools.partial(_mlp_kernel, nf=nf),
        grid=(tok.t // tm, nf),
        in_specs=[pl.BlockSpec((tm, d), lambda m, f: (m, 0)),
                  pl.BlockSpec((1, 6, d), lambda m, f: (cm(m), 0, 0)),
                  pl.BlockSpec((d, tf), lambda m, f: (0, f)),
                  pl.BlockSpec((1, tf), lambda m, f: (0, f)),
                  pl.BlockSpec((tf, d), lambda m, f: (f, 0)),
                  pl.BlockSpec((1, d), lambda m, f: (0, 0)),
                  pl.BlockSpec((1, d), lambda m, f: (0, 0)),
                  pl.BlockSpec((1, d), lambda m, f: (0, 0))],
        out_specs=pl.BlockSpec((tm, d), lambda m, f: (m, 0)),
        out_shape=jax.ShapeDtypeStruct((tok.t, d), F32),
        scratch_shapes=[pltpu.VMEM((tm, d), BF16), pltpu.VMEM((tm, d), F32)],
        compiler_params=_cparams(("arbitrary", "arbitrary"), 56),
        name="mlp_ln",
    )(x, mod, w1, b1, w2, b2, ln_g, ln_b)


def _mla_weights(w_uq, w_ukv):
    hd = MLA_NOPE + MLA_ROPE
    wq = w_uq.reshape(MLA_Q_RANK, MLA_HEADS, hd)
    wq = jnp.pad(wq, ((0, 0), (0, 0), (0, LANES - hd))).reshape(MLA_Q_RANK, MLA_HEADS * LANES)
    wkv = w_ukv.reshape(MLA_KV_RANK, MLA_HEADS, MLA_NOPE + MLA_V)
    w_nope, w_v = wkv[..., :MLA_NOPE], wkv[..., MLA_NOPE:]
    wk = jnp.zeros((MLA_HEADS, 2 * LANES, LANES), F32)
    wk = wk.at[:, :MLA_KV_RANK, :MLA_NOPE].set(w_nope.transpose(1, 0, 2))
    eye = jnp.eye(MLA_ROPE, dtype=F32)
    wk = wk.at[:, MLA_KV_RANK:MLA_KV_RANK + MLA_ROPE, MLA_NOPE:MLA_NOPE + MLA_ROPE].set(eye)
    wk = wk.reshape(MLA_HEADS // 2, 2, 2 * LANES, LANES).transpose(0, 2, 1, 3).reshape(MLA_HEADS // 2, 2 * LANES, 2 * LANES)
    wv = jnp.zeros((MLA_HEADS, MLA_KV_RANK, LANES), F32)
    vt = w_v.transpose(1, 0, 2)
    wv = wv.at[0::2, :, :MLA_V].set(vt[0::2])
    wv = wv.at[1::2, :, MLA_V:].set(vt[1::2])
    return wq.astype(BF16), wk.astype(BF16), wv.astype(BF16)


def _conformer_taps(dw_w, l):
    k, c = dw_w.shape
    half = k // 2
    fwd = jnp.zeros((l, c), F32).at[:half + 1].set(dw_w[:half + 1][::-1])
    bwd = jnp.zeros((l, c), F32).at[1:half + 1].set(dw_w[half + 1:])
    return jnp.concatenate([fwd, bwd], axis=1).astype(BF16)


def kernel(x_prompt, x_sample, cache_mla_ckv, cache_mla_krope, cache_gqa_k, cache_gqa_v, c, c_ctx, ev_w_in, hy_conv_w, hy_conv_b, hy_filt_w1, hy_filt_b1, hy_filt_w2, hy_filt_b2, hy_filt_w3, hy_sin_freq, hy_decay, hy_skip, mla_q_norm_g, mla_w_uq, mla_kv_norm_g, mla_w_ukv, ev_w_out, od_w_in, cv_dw_w, cv_dw_b, cv_ln_g, cv_ln_b, gqa_q_norm_g, gqa_k_norm_g, od_w_out, ada_w, ada_b, ln_g, ln_b, mlp_w1, mlp_b1, mlp_w2, mlp_b2):
    bc, lc, d = x_prompt.shape
    bl, ll, _ = x_sample.shape
    tok = _Tok(bc, lc, bl, ll)
    n_ev = ev_w_in.shape[0]
    n_od = od_w_in.shape[0]
    tm_in = min(512, ll)
    tm_mlp = min(1024, ll)
    tq = 256
    seq_ctx = max(1, 1024 // lc)
    tc_lat = 256 if ll > 1024 else 512

    cond8 = jnp.zeros((8, d), F32).at[0].set(c_ctx).at[1:1 + bl].set(c)
    mods = _ada(cond8, ada_w, ada_b).reshape(DEPTH, 8, 6, d)

    mla_scale = (MLA_NOPE + MLA_ROPE) ** -0.5
    ev_tabs = (_rope_tables(ll, tm_in, MLA_HEADS, LANES, MLA_NOPE, MLA_ROPE, mla_scale)
               + _rope_tables(ll, tm_in, 1, LANES, 0, MLA_ROPE, 1.0))
    od_tabs = (_rope_tables(ll, tm_in, GQA_HEADS, GQA_HD, 0, GQA_HD, GQA_HD ** -0.5)
               + _rope_tables(ll, tm_in, GQA_KV_HEADS, GQA_HD, 0, GQA_HD, 1.0))
    mats_ctx = _dft_mats(lc)
    mats_lat = _dft_mats(ll)

    x = jnp.concatenate([x_prompt.reshape(tok.t_ctx, d), x_sample.reshape(tok.t_lat, d)], axis=0)
    ckv_list, krope_list, k_list, v_list = [], [], [], []

    for layer in range(DEPTH):
        i = layer // 2
        mod = mods[layer]
        if layer % 2 == 0:
            w_in_p = jnp.pad(ev_w_in[i], ((0, 0), (0, EV_W - ev_w_in.shape[2]))).astype(BF16)
            wq, wk, wv = _mla_weights(mla_w_uq[i], mla_w_ukv[i])
            u_hy, q, ckr = _even_in(tok, x, mod, w_in_p, mla_q_norm_g[i][None], wq, mla_kv_norm_g[i][None],
                                    ev_tabs, tm_in)
            ckv_list.append(ckr[:tok.t_ctx, :MLA_KV_RANK].reshape(bc, lc, MLA_KV_RANK))
            krope_list.append(ckr[:tok.t_ctx, MLA_KV_RANK:MLA_KV_RANK + MLA_ROPE].reshape(bc, lc, MLA_ROPE))
            at_ctx = _mla_attn(q, ckr, wk, wv, 0, bc, lc, min(tq, lc))
            at_lat = _mla_attn(q, ckr, wk, wv, tok.t_ctx, bl, ll, tq, cache=(cache_mla_ckv, cache_mla_krope, i))
            y_b = jnp.concatenate([at_ctx, at_lat], axis=0)

            skip = hy_skip[i]
            conv_b = hy_conv_b[i][None]
            ys = []
            for (off, nb, l, nseq, tc, mats) in ((0, bc, lc, seq_ctx, HY_DIM, mats_ctx),
                                                 (tok.t_ctx, bl, ll, 1, tc_lat, mats_lat)):
                hf = _hyena_filter(l, hy_filt_w1[i], hy_filt_b1[i], hy_filt_w2[i], hy_filt_b2[i],
                                   hy_filt_w3[i], hy_sin_freq[i], hy_decay[i])
                spec = _filter_spectrum(mats[0], hf, HY_ORDER * HY_DIM)
                z1 = _hyena_stage(u_hy, 2, u_hy, 0, hy_conv_w[i], conv_b, skip[0:1], mats, spec, 0,
                                  off, nb, l, nseq, tc, True, F32)
                ys.append(_hyena_stage(z1, 0, u_hy, 1, hy_conv_w[i], conv_b, skip[1:2], mats, spec, 1,
                                       off, nb, l, nseq, tc, False, BF16))
            y_a = jnp.concatenate(ys, axis=0)
            w_out = ev_w_out[i].astype(BF16)
        else:
            u_cv, q, kv = _odd_in(tok, x, mod, od_w_in[i].astype(BF16),
                                  jnp.tile(gqa_q_norm_g[i], GQA_HEADS)[None],
                                  jnp.tile(gqa_k_norm_g[i], GQA_KV_HEADS)[None], od_tabs, tm_in)
            n_k = GQA_KV_HEADS * GQA_HD
            k_list.append(kv[:tok.t_ctx, :n_k].reshape(bc, lc, GQA_KV_HEADS, GQA_HD))
            v_list.append(kv[:tok.t_ctx, n_k:].reshape(bc, lc, GQA_KV_HEADS, GQA_HD))
            past = cache_gqa_k.shape[2]
            cache = (cache_gqa_k.reshape(bl, n_od, past, n_k), cache_gqa_v.reshape(bl, n_od, past, n_k), i)
            at_ctx = _gqa_attn(q, kv, 0, bc, lc, min(tq, lc))
            at_lat = _gqa_attn(q, kv, tok.t_ctx, bl, ll, tq, cache=cache)
            y_b = jnp.concatenate([at_ctx, at_lat], axis=0)

            ys = []
            for (off, nb, l, nseq, mats) in ((0, bc, lc, seq_ctx, mats_ctx), (tok.t_ctx, bl, ll, 1, mats_lat)):
                spec = _filter_spectrum(mats[0], _conformer_taps(cv_dw_w[i], l), CV_DIM)
                ys.append(_conformer(u_cv, cv_dw_b[i][None], cv_ln_g[i][None], cv_ln_b[i][None], mats, spec,
                                     off, nb, l, nseq))
            y_a = jnp.concatenate(ys, axis=0)
            w_out = od_w_out[i].astype(BF16)

        x = _out_proj(tok, y_a, y_b, x, mod, w_out, ln_g[layer, 0][None], ln_b[layer, 0][None], tm_in)
        x = _mlp(tok, x, mod, mlp_w1[layer].astype(BF16), mlp_b1[layer][None], mlp_w2[layer].astype(BF16),
                 mlp_b2[layer][None], ln_g[layer, 1][None], ln_b[layer, 1][None], tm_mlp, 1024)

    y_prompt = x[:tok.t_ctx].reshape(bc, lc, d)
    y_sample = x[tok.t_ctx:].reshape(bl, ll, d)
    return (y_prompt, y_sample, jnp.stack(ckv_list, axis=1), jnp.stack(krope_list, axis=1),
            jnp.stack(k_list, axis=1), jnp.stack(v_list, axis=1))
```

```python
import functools
import math

import jax
import jax.numpy as jnp
from jax import lax
from jax.experimental import pallas as pl
from jax.experimental.pallas import tpu as pltpu

F32 = jnp.float32
BF16 = jnp.bfloat16

DEPTH = 4
GRID_W = 64
ALPHA = (2.0 * DEPTH) ** 0.25
LN_EPS = 1e-5
RMS_EPS = 1e-6
ROPE_THETA = 10000.0

HY_DIM = 512
HY_ORDER = 2
HY_BANDS = 16
HY_EMB = 2 * HY_BANDS + 1
HY_FILT_HID = 64

MLA_HEADS = 8
MLA_NOPE = 64
MLA_ROPE = 32
MLA_V = 64
MLA_Q_RANK = 256
MLA_KV_RANK = 128

CV_DIM = 512

GQA_HEADS = 8
GQA_KV_HEADS = 2
GQA_HD = 64

LANES = 128
CONV_BLOCK = 512
MIB = 2 ** 20


def _cparams(sem, vmem_mib):
    return pltpu.CompilerParams(dimension_semantics=sem, vmem_limit_bytes=vmem_mib * MIB)


def _layer_norm(z, g, b):
    mu = jnp.mean(z, axis=-1, keepdims=True)
    zc = z - mu
    var = jnp.mean(zc * zc, axis=-1, keepdims=True)
    return zc * lax.rsqrt(var + LN_EPS) * g + b


def _rms(x, g):
    return x * lax.rsqrt(jnp.mean(x * x, axis=-1, keepdims=True) + RMS_EPS) * g


def _seg_mean(sq, s_mat):
    hi = sq.astype(BF16)
    lo = (sq - hi.astype(F32)).astype(BF16)
    return (jnp.dot(hi, s_mat, preferred_element_type=F32)
            + jnp.dot(lo, s_mat, preferred_element_type=F32))


def _rope(x, cos, sin_signed, half):
    w = x.shape[1]
    lane = lax.broadcasted_iota(jnp.int32, x.shape, 1)
    first = jnp.bitwise_and(lane, 2 * half - 1) < half
    rot = jnp.where(first, pltpu.roll(x, w - half, 1), pltpu.roll(x, half, 1))
    return x * cos + rot * sin_signed


def _ada_kernel(c_ref, w_ref, b_ref, o_ref):
    c = c_ref[...]
    s = (c * jax.nn.sigmoid(c)).astype(BF16)
    o_ref[0] = jnp.dot(s, w_ref[0].astype(BF16), preferred_element_type=F32) + b_ref[0]


def _ada(cond8, ada_w, ada_b):
    depth, d, n = ada_w.shape
    tn = 1536
    return pl.pallas_call(
        _ada_kernel,
        grid=(depth, n // tn),
        in_specs=[pl.BlockSpec((8, d), lambda l, j: (0, 0)),
                  pl.BlockSpec((1, d, tn), lambda l, j: (l, 0, j)),
                  pl.BlockSpec((1, 1, tn), lambda l, j: (l, 0, j))],
        out_specs=pl.BlockSpec((1, 8, tn), lambda l, j: (l, 0, j)),
        out_shape=jax.ShapeDtypeStruct((depth, 8, n), F32),
        compiler_params=_cparams(("arbitrary", "arbitrary"), 32),
        name="ada_mod",
    )(cond8, ada_w, ada_b.reshape(depth, 1, n))


class _Tok:
    def __init__(self, bc, lc, bl, ll):
        self.bc, self.lc, self.bl, self.ll = bc, lc, bl, ll
        self.t_ctx = bc * lc
        self.t_lat = bl * ll
        self.t = self.t_ctx + self.t_lat

    def cond_map(self, tm):
        n_ctx, per_b = self.t_ctx // tm, self.ll // tm
        return lambda m: jnp.where(m < n_ctx, 0, 1 + (m - n_ctx) // per_b)

    def rope_map(self, tm):
        n_ctx, per_b = self.t_ctx // tm, self.ll // tm
        return lambda m: jnp.where(m < n_ctx, 0, 1 + (m - n_ctx) % per_b)

    def part_specs(self, tm, n_parts, width):
        if n_parts == 1:
            return [pl.BlockSpec((tm, width), lambda m: (m, 0))]
        n_ctx = self.t_ctx // tm
        return [pl.BlockSpec((tm, width), lambda m: (jnp.minimum(m, n_ctx - 1), 0)),
                pl.BlockSpec((tm, width), lambda m: (jnp.maximum(m - n_ctx, 0), 0))]


def _read_parts(refs, n_ctx_tiles):
    if len(refs) == 1:
        return refs[0][...]
    return jnp.where(pl.program_id(0) < n_ctx_tiles, refs[0][...], refs[1][...])


def _rope_tables(ll, tm, n_heads, head_w, rope_off, rope_dim, scale):
    half = rope_dim // 2
    quarter = half // 2
    t = jnp.arange(ll, dtype=F32)
    row = jnp.floor(t / GRID_W)
    col = t - row * GRID_W
    inv = ROPE_THETA ** (-jnp.arange(0, half, 2, dtype=F32) / half)
    ang_r = row[:, None] * inv[None, :]
    ang_c = col[:, None] * inv[None, :]
    ang = jnp.concatenate([ang_r, ang_r, ang_c, ang_c], axis=-1)
    sign = jnp.concatenate([-jnp.ones((quarter,), F32), jnp.ones((quarter,), F32)] * 2)
    pad = ((0, 0), (rope_off, head_w - rope_off - rope_dim))
    cos_h = jnp.pad(jnp.cos(ang), pad, constant_values=1.0)
    sin_h = jnp.pad(jnp.sin(ang) * sign, pad)
    cos_l = jnp.tile(cos_h, (1, n_heads))
    sin_l = jnp.tile(sin_h, (1, n_heads))
    w = n_heads * head_w
    cos = jnp.concatenate([jnp.ones((tm, w), F32), cos_l], axis=0) * scale
    sin = jnp.concatenate([jnp.zeros((tm, w), F32), sin_l], axis=0) * scale
    return cos, sin


EV_W = 2048


def _even_in_kernel(*refs, n_x, n_ctx_tiles):
    x_refs = refs[:n_x]
    (mod_ref, w_ref, gq_ref, wuq_ref, gkv_ref, cq_ref, sq_ref, ck_ref, sk_ref,
     uhy_ref, q_ref, ckr_ref) = refs[n_x:]
    x = _read_parts(x_refs, n_ctx_tiles)
    h = (x * (1.0 + mod_ref[0, 1:2, :]) + mod_ref[0, 0:1, :]).astype(BF16)
    u = jnp.dot(h, w_ref[...], preferred_element_type=F32)
    n_hy = 3 * HY_DIM
    uhy_ref[...] = u[:, :n_hy]
    cqn = _rms(u[:, n_hy:n_hy + MLA_Q_RANK], gq_ref[...])
    q = jnp.dot(cqn.astype(BF16), wuq_ref[...], preferred_element_type=F32)
    cq, sq = cq_ref[...], sq_ref[...]
    for hd in range(MLA_HEADS):
        cols = slice(hd * LANES, (hd + 1) * LANES)
        q_ref[:, cols] = _rope(q[:, cols], cq, sq, MLA_ROPE // 4).astype(BF16)
    o = n_hy + MLA_Q_RANK
    ckr_ref[:, :MLA_KV_RANK] = _rms(u[:, o:o + MLA_KV_RANK], gkv_ref[...])
    ckr_ref[:, MLA_KV_RANK:] = _rope(u[:, o + MLA_KV_RANK:], ck_ref[...], sk_ref[...], MLA_ROPE // 4)


def _even_in(tok, x_parts, mod, w_in_p, gq, wuq_p, gkv, tabs, tm):
    cq, sq, ck, sk = tabs
    d = x_parts[0].shape[1]
    cm, rm = tok.cond_map(tm), tok.rope_map(tm)
    qw = MLA_HEADS * LANES
    row = lambda m: (m, 0)
    full = lambda m: (0, 0)
    tab = pl.BlockSpec((tm, LANES), lambda m: (rm(m), 0))
    return pl.pallas_call(
        functools.partial(_even_in_kernel, n_x=len(x_parts), n_ctx_tiles=tok.t_ctx // tm),
        grid=(tok.t // tm,),
        in_specs=tok.part_specs(tm, len(x_parts), d) + [
            pl.BlockSpec((1, 6, d), lambda m: (cm(m), 0, 0)),
            pl.BlockSpec((d, EV_W), full),
            pl.BlockSpec((1, MLA_Q_RANK), full),
            pl.BlockSpec((MLA_Q_RANK, qw), full),
            pl.BlockSpec((1, MLA_KV_RANK), full),
            tab, tab, tab, tab],
        out_specs=[pl.BlockSpec((tm, 3 * HY_DIM), row),
                   pl.BlockSpec((tm, qw), row),
                   pl.BlockSpec((tm, 2 * LANES), row)],
        out_shape=[jax.ShapeDtypeStruct((tok.t, 3 * HY_DIM), F32),
                   jax.ShapeDtypeStruct((tok.t, qw), BF16),
                   jax.ShapeDtypeStruct((tok.t, 2 * LANES), F32)],
        compiler_params=_cparams(("arbitrary",), 48),
        name="even_in",
    )(*x_parts, mod, w_in_p, gq, wuq_p, gkv, cq, sq, ck, sk)


def _odd_in_kernel(x_ref, mod_ref, w_ref, gq_ref, gk_ref, seg_ref, cq_ref, sq_ref, ck_ref, sk_ref,
                   glu_ref, q_ref, kv_ref):
    h = (x_ref[...] * (1.0 + mod_ref[0, 1:2, :]) + mod_ref[0, 0:1, :]).astype(BF16)
    u = jnp.dot(h, w_ref[...], preferred_element_type=F32)
    n_q = GQA_HEADS * GQA_HD
    n_k = GQA_KV_HEADS * GQA_HD
    glu_ref[...] = (u[:, :CV_DIM] * jax.nn.sigmoid(u[:, CV_DIM:2 * CV_DIM])).astype(BF16)
    seg = seg_ref[...]
    gq, cq, sq = gq_ref[...], cq_ref[...], sq_ref[...]
    for j in range(n_q // LANES):
        q = u[:, 2 * CV_DIM + j * LANES:2 * CV_DIM + (j + 1) * LANES]
        qn = q * lax.rsqrt(_seg_mean(q * q, seg) + RMS_EPS) * gq
        q_ref[:, j * LANES:(j + 1) * LANES] = _rope(qn, cq, sq, GQA_HD // 4).astype(BF16)
    o = 2 * CV_DIM + n_q
    k = u[:, o:o + n_k]
    kn = k * lax.rsqrt(_seg_mean(k * k, seg) + RMS_EPS) * gk_ref[...]
    kv_ref[:, :n_k] = _rope(kn, ck_ref[...], sk_ref[...], GQA_HD // 4)
    kv_ref[:, n_k:] = u[:, o + n_k:]


def _odd_in(tok, x, mod, w_in, gq, gk, tabs, tm):
    cq, sq, ck, sk = tabs
    d, n = w_in.shape
    n_q = GQA_HEADS * GQA_HD
    n_k = GQA_KV_HEADS * GQA_HD
    cm, rm = tok.cond_map(tm), tok.rope_map(tm)
    i = jnp.arange(LANES)
    seg = jnp.where((i[:, None] // GQA_HD) == (i[None, :] // GQA_HD), 1.0 / GQA_HD, 0.0).astype(BF16)
    row = lambda m: (m, 0)
    full = lambda m: (0, 0)
    tab = pl.BlockSpec((tm, LANES), lambda m: (rm(m), 0))
    return pl.pallas_call(
        _odd_in_kernel,
        grid=(tok.t // tm,),
        in_specs=[pl.BlockSpec((tm, d), row),
                  pl.BlockSpec((1, 6, d), lambda m: (cm(m), 0, 0)),
                  pl.BlockSpec((d, n), full),
                  pl.BlockSpec((1, LANES), full),
                  pl.BlockSpec((1, LANES), full),
                  pl.BlockSpec((LANES, LANES), full),
                  tab, tab, tab, tab],
        out_specs=[pl.BlockSpec((tm, CV_DIM), row),
                   pl.BlockSpec((tm, n_q), row),
                   pl.BlockSpec((tm, 2 * n_k), row)],
        out_shape=[jax.ShapeDtypeStruct((tok.t, CV_DIM), BF16),
                   jax.ShapeDtypeStruct((tok.t, n_q), BF16),
                   jax.ShapeDtypeStruct((tok.t, 2 * n_k), F32)],
        compiler_params=_cparams(("arbitrary",), 48),
        name="odd_in",
    )(x, mod, w_in, gq, gk, seg, cq, sq, ck, sk)


def _softmax_pv(q, k, v):
    s = lax.dot_general(q, k, (((1,), (1,)), ((), ())), preferred_element_type=F32)
    p = jnp.exp(s - jnp.max(s, axis=-1, keepdims=True))
    l = jnp.sum(p, axis=-1, keepdims=True)
    return jnp.dot(p.astype(BF16), v, preferred_element_type=F32) * (1.0 / l)


def _mla_attn_kernel(*refs, seg_lens, has_cache, pps):
    if has_cache:
        q_ref, cckv_ref, ckr_ref, own_ref, wk_ref, wv_ref, o_ref, ck_scr, k_scr, v_scr = refs
    else:
        q_ref, own_ref, wk_ref, wv_ref, o_ref, ck_scr, k_scr, v_scr = refs
    npair = MLA_HEADS // 2

    @pl.when((pl.program_id(1) == 0) & (pl.program_id(2) == 0))
    def _():
        off = 0
        if has_cache:
            n = seg_lens[0]
            ck_scr[0:n, :MLA_KV_RANK] = cckv_ref[0, 0].astype(BF16)
            ck_scr[0:n, MLA_KV_RANK:] = jnp.zeros((n, LANES), BF16)
            ck_scr[0:n, MLA_KV_RANK:MLA_KV_RANK + MLA_ROPE] = ckr_ref[0, 0].astype(BF16)
            off = n
        ck_scr[off:off + seg_lens[-1], :] = own_ref[...].astype(BF16)
        ck = ck_scr[...]
        for hp in range(npair):
            k_scr[hp] = jnp.dot(ck, wk_ref[hp], preferred_element_type=F32).astype(BF16)
        for h in range(MLA_HEADS):
            v_scr[h] = jnp.dot(ck[:, :MLA_KV_RANK], wv_ref[h], preferred_element_type=F32).astype(BF16)

    for lp in range(pps):
        gp = lp if pps == npair else pl.program_id(2) * pps + lp
        kp = k_scr[gp]
        q0 = q_ref[:, 2 * lp * LANES:(2 * lp + 1) * LANES]
        q1 = q_ref[:, (2 * lp + 1) * LANES:(2 * lp + 2) * LANES]
        o = _softmax_pv(q0, kp[:, :LANES], v_scr[2 * gp]) + _softmax_pv(q1, kp[:, LANES:], v_scr[2 * gp + 1])
        o_ref[:, lp * LANES:(lp + 1) * LANES] = o.astype(o_ref.dtype)


def _mla_attn(q, ckr, wk, wv, tok_off, nb, lq, tq, pps, cache=None):
    nq = lq // tq
    qb0 = tok_off // tq
    ob0 = tok_off // lq
    npair = MLA_HEADS // 2
    seg_lens = (lq,) if cache is None else (cache[0].shape[2], lq)
    lk = sum(seg_lens)
    in_specs = [pl.BlockSpec((tq, 2 * LANES * pps), lambda b, i, p: (qb0 + b * nq + i, p))]
    args = [q]
    if cache is not None:
        cckv, ckr_c, layer = cache
        past = cckv.shape[2]
        in_specs += [pl.BlockSpec((1, 1, past, MLA_KV_RANK), lambda b, i, p: (b, layer, 0, 0)),
                     pl.BlockSpec((1, 1, past, MLA_ROPE), lambda b, i, p: (b, layer, 0, 0))]
        args += [cckv, ckr_c]
    in_specs += [pl.BlockSpec((lq, 2 * LANES), lambda b, i, p: (ob0 + b, 0)),
                 pl.BlockSpec((npair, 2 * LANES, 2 * LANES), lambda b, i, p: (0, 0, 0)),
                 pl.BlockSpec((MLA_HEADS, MLA_KV_RANK, LANES), lambda b, i, p: (0, 0, 0))]
    args += [ckr, wk, wv]
    return pl.pallas_call(
        functools.partial(_mla_attn_kernel, seg_lens=seg_lens, has_cache=cache is not None, pps=pps),
        grid=(nb, nq, npair // pps),
        in_specs=in_specs,
        out_specs=pl.BlockSpec((tq, LANES * pps), lambda b, i, p: (b * nq + i, p)),
        out_shape=jax.ShapeDtypeStruct((nb * lq, MLA_HEADS * MLA_V), BF16),
        scratch_shapes=[pltpu.VMEM((lk, 2 * LANES), BF16),
                        pltpu.VMEM((npair, lk, 2 * LANES), BF16),
                        pltpu.VMEM((MLA_HEADS, lk, LANES), BF16)],
        compiler_params=_cparams(("arbitrary", "arbitrary", "arbitrary"), 48),
        name="mla_attn_lat" if cache is not None else "mla_attn_ctx",
    )(*args)


def _gqa_attn_kernel(*refs, seg_lens, has_cache, pps):
    if has_cache:
        q_ref, ck_ref, cv_ref, own_ref, sel_ref, o_ref, kv_scr, k_scr, v_scr = refs
    else:
        q_ref, own_ref, sel_ref, o_ref, kv_scr, k_scr, v_scr = refs
    n_k = GQA_KV_HEADS * GQA_HD
    npair = GQA_HEADS // 2
    group_pairs = GQA_HEADS // GQA_KV_HEADS // 2

    @pl.when((pl.program_id(1) == 0) & (pl.program_id(2) == 0))
    def _():
        off = 0
        if has_cache:
            n = seg_lens[0]
            kv_scr[0:n, :n_k] = ck_ref[0, 0].astype(BF16)
            kv_scr[0:n, n_k:] = cv_ref[0, 0].astype(BF16)
            off = n
        kv_scr[off:off + seg_lens[-1], :] = own_ref[...].astype(BF16)
        kk = kv_scr[:, :n_k]
        vv = kv_scr[:, n_k:]
        for s in range(2 * GQA_KV_HEADS):
            k_scr[s] = jnp.dot(kk, sel_ref[s], preferred_element_type=F32).astype(BF16)
            v_scr[s] = jnp.dot(vv, sel_ref[s], preferred_element_type=F32).astype(BF16)

    for lp in range(pps):
        gp = lp if pps == npair else pl.program_id(2) * pps + lp
        s0 = 2 * (gp // group_pairs)
        q = q_ref[:, lp * LANES:(lp + 1) * LANES]
        o = _softmax_pv(q, k_scr[s0], v_scr[s0]) + _softmax_pv(q, k_scr[s0 + 1], v_scr[s0 + 1])
        o_ref[:, lp * LANES:(lp + 1) * LANES] = o.astype(o_ref.dtype)


def _gqa_attn(q, kv, tok_off, nb, lq, tq, pps, cache=None):
    nq = lq // tq
    qb0 = tok_off // tq
    ob0 = tok_off // lq
    npair = GQA_HEADS // 2
    n_k = GQA_KV_HEADS * GQA_HD
    seg_lens = (lq,) if cache is None else (cache[0].shape[2], lq)
    lk = sum(seg_lens)
    src = jnp.arange(n_k)[:, None]
    dst = jnp.arange(n_k)[None, :]
    sel = jnp.stack([jnp.where((src // GQA_HD == kvh) & (dst // GQA_HD == i) & (src % GQA_HD == dst % GQA_HD), 1.0, 0.0)
                     for kvh in range(GQA_KV_HEADS) for i in range(2)]).astype(BF16)
    in_specs = [pl.BlockSpec((tq, LANES * pps), lambda b, i, p: (qb0 + b * nq + i, p))]
    args = [q]
    if cache is not None:
        ck, cv, layer = cache
        past = ck.shape[2]
        in_specs += [pl.BlockSpec((1, 1, past, n_k), lambda b, i, p: (b, layer, 0, 0)),
                     pl.BlockSpec((1, 1, past, n_k), lambda b, i, p: (b, layer, 0, 0))]
        args += [ck, cv]
    in_specs += [pl.BlockSpec((lq, 2 * n_k), lambda b, i, p: (ob0 + b, 0)),
                 pl.BlockSpec((2 * GQA_KV_HEADS, n_k, n_k), lambda b, i, p: (0, 0, 0))]
    args += [kv, sel]
    return pl.pallas_call(
        functools.partial(_gqa_attn_kernel, seg_lens=seg_lens, has_cache=cache is not None, pps=pps),
        grid=(nb, nq, npair // pps),
        in_specs=in_specs,
        out_specs=pl.BlockSpec((tq, LANES * pps), lambda b, i, p: (b * nq + i, p)),
        out_shape=jax.ShapeDtypeStruct((nb * lq, GQA_HEADS * GQA_HD), BF16),
        scratch_shapes=[pltpu.VMEM((lk, 2 * n_k), BF16),
                        pltpu.VMEM((2 * GQA_KV_HEADS, lk, n_k), BF16),
                        pltpu.VMEM((2 * GQA_KV_HEADS, lk, n_k), BF16)],
        compiler_params=_cparams(("arbitrary", "arbitrary", "arbitrary"), 48),
        name="gqa_attn_lat" if cache is not None else "gqa_attn_ctx",
    )(*args)


def _dft_mats(lb):
    n = 2 * lb
    r = jnp.arange(n, dtype=jnp.int32)
    nyq = r == lb
    f = jnp.where(nyq, lb, r % lb)
    is_im = (r >= lb) & ~nyq
    s = jnp.arange(lb, dtype=jnp.int32)
    ang = ((f[:, None] * s[None, :]) % n).astype(F32) * (2.0 * math.pi / n)
    fwd = jnp.where(is_im[:, None], -jnp.sin(ang), jnp.cos(ang))
    wgt = jnp.where((r == 0) | nyq, 1.0 / n, 2.0 / n)
    inv = (fwd * wgt[:, None]).T
    return fwd.astype(BF16), inv.astype(BF16)


def _filter_segments(fwd, bwd, lb, dmax):
    l, c = fwd.shape
    k2 = jnp.concatenate([jnp.zeros((1, c), fwd.dtype), bwd[1:][::-1], fwd], axis=0)
    k2r = k2[::-1]
    ds = range(-dmax, dmax + 1)
    kf = jnp.stack([k2[l + lb * d:l + lb * d + lb] for d in ds])
    kb = jnp.stack([k2r[l - 1 - lb * d:l - 1 - lb * d + lb] for d in ds])
    row0 = lax.broadcasted_iota(jnp.int32, kb.shape, 1) == 0
    return kf, jnp.where(row0, jnp.zeros_like(kb), kb)


def _spec_kernel(f_ref, kf_ref, kb_ref, p_ref, q_ref, p2_ref, *, lb):
    tf = jnp.dot(f_ref[...], kf_ref[0], preferred_element_type=F32)
    tb = jnp.dot(f_ref[...], kb_ref[0], preferred_element_type=F32)
    re = tf[:lb] + tb[:lb]
    im = tf[lb:] - tb[lb:]
    nyq = tf[lb:] + tb[lb:]
    row0 = lax.broadcasted_iota(jnp.int32, re.shape, 0) == 0
    p_ref[0] = re
    q_ref[0] = jnp.where(row0, 0.0, im)
    p2_ref[0] = jnp.where(row0, nyq, re)


def _filter_spectrum(fwd_mat, kf, kb):
    n, lb = fwd_mat.shape
    nd, _, cw = kf.shape
    tc = 512
    out = jax.ShapeDtypeStruct((nd, lb, cw), F32)
    blk = pl.BlockSpec((1, lb, tc), lambda c, d: (d, 0, c))
    return pl.pallas_call(
        functools.partial(_spec_kernel, lb=lb),
        grid=(cw // tc, nd),
        in_specs=[pl.BlockSpec((n, lb), lambda c, d: (0, 0)), blk, blk],
        out_specs=[blk, blk, blk],
        out_shape=[out, out, out],
        compiler_params=_cparams(("arbitrary", "arbitrary"), 32),
        name="filter_spectrum",
    )(fwd_mat, kf, kb)


def _block_conv(zbf_ref, z_scr, f_ref, fi_ref, p_ref, q_ref, p2_ref, lb, nblk, nbk, dmax, emit):
    for r in range(nblk):
        z_scr[r] = jnp.dot(f_ref[...], zbf_ref[r * lb:(r + 1) * lb, :], preferred_element_type=F32)
    for r in range(nblk):
        s, i = divmod(r, nbk)
        ya = yb = None
        for j in range(nbk):
            d = i - j
            if abs(d) > dmax:
                continue
            re = z_scr[s * nbk + j, :lb, :]
            im = z_scr[s * nbk + j, lb:, :]
            p, q, p2 = p_ref[d + dmax], q_ref[d + dmax], p2_ref[d + dmax]
            ta = re * p - im * q
            tb = re * q + im * p2
            ya = ta if ya is None else ya + ta
            yb = tb if yb is None else yb + tb
        y = jnp.concatenate([ya, yb], axis=0).astype(BF16)
        emit(r, jnp.dot(fi_ref[...], y, preferred_element_type=F32))


def _short_conv(u, w_ref, b_ref, l):
    rows = u.shape[0]
    t = jnp.bitwise_and(lax.broadcasted_iota(jnp.int32, u.shape, 0), l - 1)
    prev = jnp.where(t == 0, 0.0, pltpu.roll(u, 1, 0))
    nxt = jnp.where(t == l - 1, 0.0, pltpu.roll(u, rows - 1, 0))
    return w_ref[0:1, :] * prev + w_ref[1:2, :] * u + w_ref[2:3, :] * nxt + b_ref[...]


def _hyena_kernel(*refs, l, lb, nseq, dmax, conv_z):
    if conv_z:
        (uz_ref, ug_ref, cwz_ref, cbz_ref, cwg_ref, cbg_ref, skip_ref,
         f_ref, fi_ref, p_ref, q_ref, p2_ref, o_ref, zf_scr, g_scr, zbf_scr, z_scr) = refs
    else:
        (uz_ref, ug_ref, cwg_ref, cbg_ref, skip_ref,
         f_ref, fi_ref, p_ref, q_ref, p2_ref, o_ref, zf_scr, g_scr, zbf_scr, z_scr) = refs
    z = _short_conv(uz_ref[...], cwz_ref, cbz_ref, l) if conv_z else uz_ref[...]
    zf_scr[...] = z
    zbf_scr[...] = z.astype(BF16)
    g_scr[...] = _short_conv(ug_ref[...], cwg_ref, cbg_ref, l)
    skip = skip_ref[...]

    def emit(r, y):
        rows = slice(r * lb, (r + 1) * lb)
        o_ref[rows, :] = (g_scr[rows, :] * (y + skip * zf_scr[rows, :])).astype(o_ref.dtype)

    nbk = l // lb
    _block_conv(zbf_scr, z_scr, f_ref, fi_ref, p_ref, q_ref, p2_ref, lb, nseq * nbk, nbk, dmax, emit)


def _hyena_stage(z_src, z_col, u_hy, g_col, conv_w, conv_b, skip, mats, spec, spec_col,
                 tok_off, nb, l, nseq, tc, conv_z, out_dtype):
    fwd_mat, inv_mat = mats
    p_arr, q_arr, p2_arr = spec
    n, lb = fwd_mat.shape
    nd = p_arr.shape[0]
    dmax = nd // 2
    rows = nseq * l
    rb0 = tok_off // rows
    zrb0 = rb0 if conv_z else 0
    cpb = HY_DIM // tc
    cvec = lambda g: pl.BlockSpec((1, tc), lambda c, b: (0, g * cpb + c))
    in_specs = [pl.BlockSpec((rows, tc), lambda c, b: (zrb0 + b, z_col * cpb + c)),
                pl.BlockSpec((rows, tc), lambda c, b: (rb0 + b, g_col * cpb + c))]
    args = [z_src, u_hy]
    if conv_z:
        in_specs += [pl.BlockSpec((3, tc), lambda c, b: (0, z_col * cpb + c)), cvec(z_col)]
        args += [conv_w, conv_b]
    in_specs += [pl.BlockSpec((3, tc), lambda c, b: (0, g_col * cpb + c)), cvec(g_col),
                 pl.BlockSpec((1, tc), lambda c, b: (0, c)),
                 pl.BlockSpec((n, lb), lambda c, b: (0, 0)),
                 pl.BlockSpec((lb, n), lambda c, b: (0, 0))]
    args += [conv_w, conv_b, skip, fwd_mat, inv_mat]
    sspec = pl.BlockSpec((nd, lb, tc), lambda c, b: (0, 0, spec_col * cpb + c), pipeline_mode=pl.Buffered(1))
    in_specs += [sspec, sspec, sspec]
    args += [p_arr, q_arr, p2_arr]
    nblk = rows // lb
    return pl.pallas_call(
        functools.partial(_hyena_kernel, l=l, lb=lb, nseq=nseq, dmax=dmax, conv_z=conv_z),
        grid=(HY_DIM // tc, nb // nseq),
        in_specs=in_specs,
        out_specs=pl.BlockSpec((rows, tc), lambda c, b: (b, c)),
        out_shape=jax.ShapeDtypeStruct((nb * l, HY_DIM), out_dtype),
        scratch_shapes=[pltpu.VMEM((rows, tc), F32), pltpu.VMEM((rows, tc), F32),
                        pltpu.VMEM((rows, tc), BF16), pltpu.VMEM((nblk, n, tc), F32)],
        compiler_params=_cparams(("arbitrary", "arbitrary"), 56),
        name="hyena_stage",
    )(*args)


def _conformer_kernel(glu_ref, b_ref, lng_ref, lnb_ref, f_ref, fi_ref, p_ref, q_ref, p2_ref, o_ref, z_scr,
                      *, l, lb, nseq, dmax):
    bias, lng, lnb = b_ref[...], lng_ref[...], lnb_ref[...]

    def emit(r, y):
        yn = _layer_norm(y + bias, lng, lnb)
        o_ref[r * lb:(r + 1) * lb, :] = (yn * jax.nn.sigmoid(yn)).astype(o_ref.dtype)

    nbk = l // lb
    _block_conv(glu_ref, z_scr, f_ref, fi_ref, p_ref, q_ref, p2_ref, lb, nseq * nbk, nbk, dmax, emit)


def _conformer(glu, bias, ln_g, ln_b, mats, spec, tok_off, nb, l, nseq):
    fwd_mat, inv_mat = mats
    n, lb = fwd_mat.shape
    nd = spec[0].shape[0]
    rows = nseq * l
    rb0 = tok_off // rows
    c = CV_DIM
    vec = pl.BlockSpec((1, c), lambda b: (0, 0))
    sspec = pl.BlockSpec((nd, lb, c), lambda b: (0, 0, 0))
    return pl.pallas_call(
        functools.partial(_conformer_kernel, l=l, lb=lb, nseq=nseq, dmax=nd // 2),
        grid=(nb // nseq,),
        in_specs=[pl.BlockSpec((rows, c), lambda b: (rb0 + b, 0)),
                  vec, vec, vec,
                  pl.BlockSpec((n, lb), lambda b: (0, 0)),
                  pl.BlockSpec((lb, n), lambda b: (0, 0)),
                  sspec, sspec, sspec],
        out_specs=pl.BlockSpec((rows, c), lambda b: (b, 0)),
        out_shape=jax.ShapeDtypeStruct((nb * l, c), BF16),
        scratch_shapes=[pltpu.VMEM((rows // lb, n, c), F32)],
        compiler_params=_cparams(("arbitrary",), 56),
        name="conformer",
    )(glu, bias, ln_g, ln_b, fwd_mat, inv_mat, *spec)


def _filter_kernel(z_ref, w1_ref, b1_ref, w2_ref, b2_ref, w3_ref, freq_ref, decay_ref, o_ref):
    hp = lax.Precision.HIGHEST
    z = z_ref[...]
    freq = freq_ref[...]
    hid = jnp.sin(freq * (jnp.dot(z, w1_ref[...], precision=hp, preferred_element_type=F32) + b1_ref[...]))
    hid = jnp.sin(freq * (jnp.dot(hid, w2_ref[...], precision=hp, preferred_element_type=F32) + b2_ref[...]))
    h = jnp.dot(hid, w3_ref[...], precision=hp, preferred_element_type=F32)
    o_ref[...] = (h * jnp.exp(-z[:, 0:1] * jnp.abs(decay_ref[...]))).astype(o_ref.dtype)


def _hyena_filter(l, w1, b1, w2, b2, w3, freq, decay):
    t = jnp.arange(l, dtype=F32) / l
    bands = jnp.arange(1, HY_BANDS + 1, dtype=F32)
    ang = 2.0 * math.pi * t[:, None] * bands[None, :]
    z = jnp.concatenate([t[:, None], jnp.cos(ang), jnp.sin(ang), jnp.zeros((l, LANES - HY_EMB), F32)], axis=-1)
    w1p = jnp.concatenate([w1, jnp.zeros((LANES - HY_EMB, HY_FILT_HID), F32)], axis=0)
    nout = w3.shape[1]
    tl = 256
    full = lambda i: (0, 0)
    return pl.pallas_call(
        _filter_kernel,
        grid=(l // tl,),
        in_specs=[pl.BlockSpec((tl, LANES), lambda i: (i, 0)),
                  pl.BlockSpec((LANES, HY_FILT_HID), full),
                  pl.BlockSpec((1, HY_FILT_HID), full),
                  pl.BlockSpec((HY_FILT_HID, HY_FILT_HID), full),
                  pl.BlockSpec((1, HY_FILT_HID), full),
                  pl.BlockSpec((HY_FILT_HID, nout), full),
                  pl.BlockSpec((1, HY_FILT_HID), full),
                  pl.BlockSpec((1, nout), full)],
        out_specs=pl.BlockSpec((tl, nout), lambda i: (i, 0)),
        out_shape=jax.ShapeDtypeStruct((l, nout), BF16),
        compiler_params=_cparams(("arbitrary",), 32),
        name="hyena_filter",
    )(z, w1p, b1[None], w2, b2[None], w3, freq[None], decay[None])


def _out_proj_kernel(*refs, n_x, n_ctx_tiles):
    ya_refs, yb_refs, x_refs = refs[0:2], refs[2:4], refs[4:4 + n_x]
    mod_ref, wa_ref, wb_ref, g_ref, b_ref, o_ref = refs[4 + n_x:]
    y = (jnp.dot(_read_parts(ya_refs, n_ctx_tiles), wa_ref[...], preferred_element_type=F32)
         + jnp.dot(_read_parts(yb_refs, n_ctx_tiles), wb_ref[...], preferred_element_type=F32))
    z = ALPHA * _read_parts(x_refs, n_ctx_tiles) + mod_ref[0, 2:3, :] * y
    o_ref[...] = _layer_norm(z, g_ref[...], b_ref[...])


def _out_proj(tok, ya_parts, yb_parts, x_parts, mod, w_out, ln_g, ln_b, tm):
    d = x_parts[0].shape[1]
    ka, kb = ya_parts[0].shape[1], yb_parts[0].shape[1]
    cm = tok.cond_map(tm)
    full = lambda m: (0, 0)
    return pl.pallas_call(
        functools.partial(_out_proj_kernel, n_x=len(x_parts), n_ctx_tiles=tok.t_ctx // tm),
        grid=(tok.t // tm,),
        in_specs=(tok.part_specs(tm, 2, ka) + tok.part_specs(tm, 2, kb) + tok.part_specs(tm, len(x_parts), d) + [
            pl.BlockSpec((1, 6, d), lambda m: (cm(m), 0, 0)),
            pl.BlockSpec((ka, d), lambda m: (0, 0)),
            pl.BlockSpec((kb, d), lambda m: (ka // kb, 0)),
            pl.BlockSpec((1, d), full),
            pl.BlockSpec((1, d), full)]),
        out_specs=pl.BlockSpec((tm, d), lambda m: (m, 0)),
        out_shape=jax.ShapeDtypeStruct((tok.t, d), F32),
        compiler_params=_cparams(("arbitrary",), 40),
        name="out_proj_ln",
    )(*ya_parts, *yb_parts, *x_parts, mod, w_out, w_out, ln_g, ln_b)


def _mlp_kernel(x_ref, mod_ref, w1_ref, b1_ref, w2_ref, b2_ref, g_ref, b_ref, o_ref, h_scr, acc_scr, *, nf):
    f = pl.program_id(1)

    @pl.when(f == 0)
    def _():
        h_scr[...] = (x_ref[...] * (1.0 + mod_ref[0, 4:5, :]) + mod_ref[0, 3:4, :]).astype(BF16)
        acc_scr[...] = jnp.zeros_like(acc_scr)

    a = jnp.maximum(jnp.dot(h_scr[...], w1_ref[...], preferred_element_type=F32) + b1_ref[...], 0.0)
    acc_scr[...] += jnp.dot((a * a).astype(BF16), w2_ref[...], preferred_element_type=F32)

    @pl.when(f == nf - 1)
    def _():
        z = ALPHA * x_ref[...] + mod_ref[0, 5:6, :] * (acc_scr[...] + b2_ref[...])
        o_ref[...] = _layer_norm(z, g_ref[...], b_ref[...])


def _mlp(tok, x, mod, w1, b1, w2, b2, ln_g, ln_b, tm, tf, tok_off, n_rows):
    d = x.shape[1]
    dff = w1.shape[1]
    nf = dff // tf
    m0 = tok_off // tm
    cm = tok.cond_map(tm)
    return pl.pallas_call(
        functools.partial(_mlp_kernel, nf=nf),
        grid=(n_rows // tm, nf),
        in_specs=[pl.BlockSpec((tm, d), lambda m, f: (m0 + m, 0)),
                  pl.BlockSpec((1, 6, d), lambda m, f: (cm(m0 + m), 0, 0)),
                  pl.BlockSpec((d, tf), lambda m, f: (0, f)),
                  pl.BlockSpec((1, tf), lambda m, f: (0, f)),
                  pl.BlockSpec((tf, d), lambda m, f: (f, 0)),
                  pl.BlockSpec((1, d), lambda m, f: (0, 0)),
                  pl.BlockSpec((1, d), lambda m, f: (0, 0)),
                  pl.BlockSpec((1, d), lambda m, f: (0, 0))],
        out_specs=pl.BlockSpec((tm, d), lambda m, f: (m, 0)),
        out_shape=jax.ShapeDtypeStruct((n_rows, d), F32),
        scratch_shapes=[pltpu.VMEM((tm, d), BF16), pltpu.VMEM((tm, d), F32)],
        compiler_params=_cparams(("arbitrary", "arbitrary"), 56),
        name="mlp_ln",
    )(x, mod, w1, b1, w2, b2, ln_g, ln_b)


def _mla_weights(w_uq, w_ukv):
    hd = MLA_NOPE + MLA_ROPE
    wq = w_uq.reshape(MLA_Q_RANK, MLA_HEADS, hd)
    wq = jnp.pad(wq, ((0, 0), (0, 0), (0, LANES - hd))).reshape(MLA_Q_RANK, MLA_HEADS * LANES)
    wkv = w_ukv.reshape(MLA_KV_RANK, MLA_HEADS, MLA_NOPE + MLA_V).transpose(1, 0, 2)
    w_nope, w_v = wkv[..., :MLA_NOPE], wkv[..., MLA_NOPE:]
    top = jnp.pad(w_nope, ((0, 0), (0, 0), (0, LANES - MLA_NOPE)))
    place = jnp.pad(jnp.eye(MLA_ROPE, dtype=F32), ((0, LANES - MLA_ROPE), (MLA_NOPE, LANES - MLA_NOPE - MLA_ROPE)))
    wk = jnp.concatenate([top, jnp.broadcast_to(place, (MLA_HEADS, LANES, LANES))], axis=1)
    wk = wk.reshape(MLA_HEADS // 2, 2, 2 * LANES, LANES).transpose(0, 2, 1, 3).reshape(MLA_HEADS // 2, 2 * LANES, 2 * LANES)
    w_v = w_v.reshape(MLA_HEADS // 2, 2, MLA_KV_RANK, MLA_V)
    wv = jnp.stack([jnp.pad(w_v[:, 0], ((0, 0), (0, 0), (0, LANES - MLA_V))),
                    jnp.pad(w_v[:, 1], ((0, 0), (0, 0), (LANES - MLA_V, 0)))], axis=1)
    wv = wv.reshape(MLA_HEADS, MLA_KV_RANK, LANES)
    return wq.astype(BF16), wk.astype(BF16), wv.astype(BF16)


def _conformer_taps(dw_w, l):
    k, c = dw_w.shape
    half = k // 2
    fwd = jnp.pad(dw_w[:half + 1][::-1], ((0, l - half - 1), (0, 0)))
    bwd = jnp.pad(dw_w[half + 1:], ((1, l - half - 1), (0, 0)))
    return fwd.astype(BF16), bwd.astype(BF16)


def kernel(x_prompt, x_sample, cache_mla_ckv, cache_mla_krope, cache_gqa_k, cache_gqa_v, c, c_ctx, ev_w_in, hy_conv_w, hy_conv_b, hy_filt_w1, hy_filt_b1, hy_filt_w2, hy_filt_b2, hy_filt_w3, hy_sin_freq, hy_decay, hy_skip, mla_q_norm_g, mla_w_uq, mla_kv_norm_g, mla_w_ukv, ev_w_out, od_w_in, cv_dw_w, cv_dw_b, cv_ln_g, cv_ln_b, gqa_q_norm_g, gqa_k_norm_g, od_w_out, ada_w, ada_b, ln_g, ln_b, mlp_w1, mlp_b1, mlp_w2, mlp_b2):
    bc, lc, d = x_prompt.shape
    bl, ll, _ = x_sample.shape
    tok = _Tok(bc, lc, bl, ll)
    assert lc & (lc - 1) == 0 and ll & (ll - 1) == 0 and tok.t_ctx % ll == 0 and bl < 8
    n_od = od_w_in.shape[0]
    tm_in = min(512, ll)
    tm_mlp = min(1024, ll)
    tf_mlp = 1024
    tq = 256
    seq_ctx = max(1, 1024 // lc)
    tc_lat = 256 if ll > 1024 else HY_DIM
    cv_half = cv_dw_w.shape[1] // 2

    cond8 = jnp.concatenate([c_ctx[None], c, jnp.zeros((7 - bl, d), F32)], axis=0)
    mods = _ada(cond8, ada_w, ada_b).reshape(DEPTH, 8, 6, d)

    mla_scale = (MLA_NOPE + MLA_ROPE) ** -0.5
    ev_tabs = (_rope_tables(ll, tm_in, 1, LANES, MLA_NOPE, MLA_ROPE, mla_scale)
               + _rope_tables(ll, tm_in, 1, LANES, 0, MLA_ROPE, 1.0))
    od_tabs = (_rope_tables(ll, tm_in, LANES // GQA_HD, GQA_HD, 0, GQA_HD, GQA_HD ** -0.5)
               + _rope_tables(ll, tm_in, LANES // GQA_HD, GQA_HD, 0, GQA_HD, 1.0))
    passes = []
    for off, nb, l, nseq, tc in ((0, bc, lc, seq_ctx, HY_DIM), (tok.t_ctx, bl, ll, 1, tc_lat)):
        lb = min(l, CONV_BLOCK)
        passes.append((off, nb, l, nseq, tc, lb, _dft_mats(lb)))

    x_parts = (x_prompt.reshape(tok.t_ctx, d), x_sample.reshape(tok.t_lat, d))
    ckv_list, krope_list, k_list, v_list = [], [], [], []

    for layer in range(DEPTH):
        i = layer // 2
        mod = mods[layer]
        if layer % 2 == 0:
            w_in_p = jnp.pad(ev_w_in[i], ((0, 0), (0, EV_W - ev_w_in.shape[2]))).astype(BF16)
            wq, wk, wv = _mla_weights(mla_w_uq[i], mla_w_ukv[i])
            u_hy, q, ckr = _even_in(tok, x_parts, mod, w_in_p, mla_q_norm_g[i][None], wq, mla_kv_norm_g[i][None],
                                    ev_tabs, tm_in)
            ckv_list.append(ckr[:tok.t_ctx, :MLA_KV_RANK].reshape(bc, lc, MLA_KV_RANK))
            krope_list.append(ckr[:tok.t_ctx, MLA_KV_RANK:MLA_KV_RANK + MLA_ROPE].reshape(bc, lc, MLA_ROPE))
            yb_parts = (_mla_attn(q, ckr, wk, wv, 0, bc, lc, min(tq, lc), MLA_HEADS // 2),
                        _mla_attn(q, ckr, wk, wv, tok.t_ctx, bl, ll, tq, 1,
                                  cache=(cache_mla_ckv, cache_mla_krope, i)))

            skip = hy_skip[i]
            conv_b = hy_conv_b[i][None]
            ya_parts = []
            for (off, nb, l, nseq, tc, lb, mats) in passes:
                hf = _hyena_filter(l, hy_filt_w1[i], hy_filt_b1[i], hy_filt_w2[i], hy_filt_b2[i],
                                   hy_filt_w3[i], hy_sin_freq[i], hy_decay[i])
                n_f = HY_ORDER * HY_DIM
                spec = _filter_spectrum(mats[0], *_filter_segments(hf[:, :n_f], hf[:, n_f:], lb, l // lb - 1))
                z1 = _hyena_stage(u_hy, 2, u_hy, 0, hy_conv_w[i], conv_b, skip[0:1], mats, spec, 0,
                                  off, nb, l, nseq, tc, True, F32)
                ya_parts.append(_hyena_stage(z1, 0, u_hy, 1, hy_conv_w[i], conv_b, skip[1:2], mats, spec, 1,
                                             off, nb, l, nseq, tc, False, BF16))
            w_out = ev_w_out[i].astype(BF16)
        else:
            (x,) = x_parts
            gq = jnp.tile(gqa_q_norm_g[i], LANES // GQA_HD)[None]
            gk = jnp.tile(gqa_k_norm_g[i], GQA_KV_HEADS)[None]
            glu, q, kv = _odd_in(tok, x, mod, od_w_in[i].astype(BF16), gq, gk, od_tabs, tm_in)
            n_k = GQA_KV_HEADS * GQA_HD
            k_list.append(kv[:tok.t_ctx, :n_k].reshape(bc, lc, GQA_KV_HEADS, GQA_HD))
            v_list.append(kv[:tok.t_ctx, n_k:].reshape(bc, lc, GQA_KV_HEADS, GQA_HD))
            past = cache_gqa_k.shape[2]
            cache = (cache_gqa_k.reshape(bl, n_od, past, n_k), cache_gqa_v.reshape(bl, n_od, past, n_k), i)
            yb_parts = (_gqa_attn(q, kv, 0, bc, lc, min(tq, lc), GQA_HEADS // 2),
                        _gqa_attn(q, kv, tok.t_ctx, bl, ll, tq, 1, cache=cache))

            ya_parts = []
            for (off, nb, l, nseq, tc, lb, mats) in passes:
                dmax = min(l // lb - 1, -(-cv_half // lb))
                spec = _filter_spectrum(mats[0], *_filter_segments(*_conformer_taps(cv_dw_w[i], l), lb, dmax))
                ya_parts.append(_conformer(glu, cv_dw_b[i][None], cv_ln_g[i][None], cv_ln_b[i][None], mats, spec,
                                           off, nb, l, nseq))
            w_out = od_w_out[i].astype(BF16)

        x = _out_proj(tok, ya_parts, yb_parts, x_parts, mod, w_out, ln_g[layer, 0][None], ln_b[layer, 0][None], tm_in)
        mlp_args = (mod, mlp_w1[layer].astype(BF16), mlp_b1[layer][None], mlp_w2[layer].astype(BF16),
                    mlp_b2[layer][None], ln_g[layer, 1][None], ln_b[layer, 1][None], tm_mlp, tf_mlp)
        if layer < DEPTH - 1:
            x_parts = (_mlp(tok, x, *mlp_args, 0, tok.t),)
        else:
            y_prompt = _mlp(tok, x, *mlp_args, 0, tok.t_ctx).reshape(bc, lc, d)
            y_sample = _mlp(tok, x, *mlp_args, tok.t_ctx, tok.t_lat).reshape(bl, ll, d)

    return (y_prompt, y_sample, jnp.stack(ckv_list, axis=1), jnp.stack(krope_list, axis=1),
            jnp.stack(k_list, axis=1), jnp.stack(v_list, axis=1))
```

```python
import functools
import math

import jax
import jax.numpy as jnp
from jax import lax
from jax.experimental import pallas as pl
from jax.experimental.pallas import tpu as pltpu

F32 = jnp.float32
BF16 = jnp.bfloat16

DEPTH = 4
GRID_W = 64
ALPHA = (2.0 * DEPTH) ** 0.25
LN_EPS = 1e-5
RMS_EPS = 1e-6
ROPE_THETA = 10000.0

HY_DIM = 512
HY_ORDER = 2
HY_BANDS = 16
HY_EMB = 2 * HY_BANDS + 1
HY_FILT_HID = 64

MLA_HEADS = 8
MLA_NOPE = 64
MLA_ROPE = 32
MLA_V = 64
MLA_Q_RANK = 256
MLA_KV_RANK = 128

CV_DIM = 512

GQA_HEADS = 8
GQA_KV_HEADS = 2
GQA_HD = 64

LANES = 128
CONV_BLOCK = 512
MIB = 2 ** 20


def _cparams(sem, vmem_mib):
    return pltpu.CompilerParams(dimension_semantics=sem, vmem_limit_bytes=vmem_mib * MIB)


def _layer_norm(z, g, b):
    mu = jnp.mean(z, axis=-1, keepdims=True)
    zc = z - mu
    var = jnp.mean(zc * zc, axis=-1, keepdims=True)
    return zc * lax.rsqrt(var + LN_EPS) * g + b


def _rms(x, g):
    return x * lax.rsqrt(jnp.mean(x * x, axis=-1, keepdims=True) + RMS_EPS) * g


def _seg_mean(sq, s_mat):
    hi = sq.astype(BF16)
    lo = (sq - hi.astype(F32)).astype(BF16)
    return (jnp.dot(hi, s_mat, preferred_element_type=F32)
            + jnp.dot(lo, s_mat, preferred_element_type=F32))


def _rope(x, cos, sin_signed, half):
    w = x.shape[1]
    lane = lax.broadcasted_iota(jnp.int32, x.shape, 1)
    first = jnp.bitwise_and(lane, 2 * half - 1) < half
    rot = jnp.where(first, pltpu.roll(x, w - half, 1), pltpu.roll(x, half, 1))
    return x * cos + rot * sin_signed


def _ada_kernel(c_ref, w_ref, b_ref, o_ref):
    c = c_ref[...]
    s = (c * jax.nn.sigmoid(c)).astype(BF16)
    o_ref[0] = jnp.dot(s, w_ref[0].astype(BF16), preferred_element_type=F32) + b_ref[0]


def _ada(cond8, ada_w, ada_b):
    depth, d, n = ada_w.shape
    tn = 1536
    return pl.pallas_call(
        _ada_kernel,
        grid=(depth, n // tn),
        in_specs=[pl.BlockSpec((8, d), lambda l, j: (0, 0)),
                  pl.BlockSpec((1, d, tn), lambda l, j: (l, 0, j)),
                  pl.BlockSpec((1, 1, tn), lambda l, j: (l, 0, j))],
        out_specs=pl.BlockSpec((1, 8, tn), lambda l, j: (l, 0, j)),
        out_shape=jax.ShapeDtypeStruct((depth, 8, n), F32),
        compiler_params=_cparams(("arbitrary", "arbitrary"), 32),
        name="ada_mod",
    )(cond8, ada_w, ada_b.reshape(depth, 1, n))


class _Tok:
    def __init__(self, bc, lc, bl, ll):
        self.bc, self.lc, self.bl, self.ll = bc, lc, bl, ll
        self.t_ctx = bc * lc
        self.t_lat = bl * ll
        self.t = self.t_ctx + self.t_lat

    def cond_map(self, tm):
        n_ctx, per_b = self.t_ctx // tm, self.ll // tm
        return lambda m: jnp.where(m < n_ctx, 0, 1 + (m - n_ctx) // per_b)

    def rope_map(self, tm):
        n_ctx, per_b = self.t_ctx // tm, self.ll // tm
        return lambda m: jnp.where(m < n_ctx, 0, 1 + (m - n_ctx) % per_b)

    def part_specs(self, tm, n_parts, width):
        if n_parts == 1:
            return [pl.BlockSpec((tm, width), lambda m: (m, 0))]
        n_ctx = self.t_ctx // tm
        return [pl.BlockSpec((tm, width), lambda m: (jnp.minimum(m, n_ctx - 1), 0)),
                pl.BlockSpec((tm, width), lambda m: (jnp.maximum(m - n_ctx, 0), 0))]


def _read_parts(refs, n_ctx_tiles, rows=slice(None)):
    if len(refs) == 1:
        return refs[0][rows, :]
    return jnp.where(pl.program_id(0) < n_ctx_tiles, refs[0][rows, :], refs[1][rows, :])


SUB_ROWS = 256


def _rope_tables(ll, tm, n_heads, head_w, rope_off, rope_dim, scale):
    half = rope_dim // 2
    quarter = half // 2
    t = jnp.arange(ll, dtype=F32)
    row = jnp.floor(t / GRID_W)
    col = t - row * GRID_W
    inv = ROPE_THETA ** (-jnp.arange(0, half, 2, dtype=F32) / half)
    ang_r = row[:, None] * inv[None, :]
    ang_c = col[:, None] * inv[None, :]
    ang = jnp.concatenate([ang_r, ang_r, ang_c, ang_c], axis=-1)
    sign = jnp.concatenate([-jnp.ones((quarter,), F32), jnp.ones((quarter,), F32)] * 2)
    pad = ((0, 0), (rope_off, head_w - rope_off - rope_dim))
    cos_h = jnp.pad(jnp.cos(ang), pad, constant_values=1.0)
    sin_h = jnp.pad(jnp.sin(ang) * sign, pad)
    cos_l = jnp.tile(cos_h, (1, n_heads))
    sin_l = jnp.tile(sin_h, (1, n_heads))
    w = n_heads * head_w
    cos = jnp.concatenate([jnp.ones((tm, w), F32), cos_l], axis=0) * scale
    sin = jnp.concatenate([jnp.zeros((tm, w), F32), sin_l], axis=0) * scale
    return cos, sin


EV_W = 2048


def _even_in_kernel(*refs, n_x, n_ctx_tiles):
    x_refs = refs[:n_x]
    (mod_ref, w_ref, gq_ref, wuq_ref, gkv_ref, cq_ref, sq_ref, ck_ref, sk_ref,
     uhy_ref, q_ref, ckr_ref) = refs[n_x:]
    n_hy = 3 * HY_DIM
    o = n_hy + MLA_Q_RANK
    for r in range(uhy_ref.shape[0] // SUB_ROWS):
        rows = slice(r * SUB_ROWS, (r + 1) * SUB_ROWS)
        x = _read_parts(x_refs, n_ctx_tiles, rows)
        h = (x * (1.0 + mod_ref[0, 1:2, :]) + mod_ref[0, 0:1, :]).astype(BF16)
        u = jnp.dot(h, w_ref[...], preferred_element_type=F32)
        uhy_ref[rows, :] = u[:, :n_hy]
        cqn = _rms(u[:, n_hy:n_hy + MLA_Q_RANK], gq_ref[...])
        q = jnp.dot(cqn.astype(BF16), wuq_ref[...], preferred_element_type=F32)
        cq, sq = cq_ref[rows, :], sq_ref[rows, :]
        for hd in range(MLA_HEADS):
            cols = slice(hd * LANES, (hd + 1) * LANES)
            q_ref[rows, cols] = _rope(q[:, cols], cq, sq, MLA_ROPE // 4).astype(BF16)
        ckr_ref[rows, :MLA_KV_RANK] = _rms(u[:, o:o + MLA_KV_RANK], gkv_ref[...])
        ckr_ref[rows, MLA_KV_RANK:] = _rope(u[:, o + MLA_KV_RANK:], ck_ref[rows, :], sk_ref[rows, :], MLA_ROPE // 4)


def _even_in(tok, x_parts, mod, w_in_p, gq, wuq_p, gkv, tabs, tm):
    cq, sq, ck, sk = tabs
    d = x_parts[0].shape[1]
    cm, rm = tok.cond_map(tm), tok.rope_map(tm)
    qw = MLA_HEADS * LANES
    row = lambda m: (m, 0)
    full = lambda m: (0, 0)
    tab = pl.BlockSpec((tm, LANES), lambda m: (rm(m), 0))
    return pl.pallas_call(
        functools.partial(_even_in_kernel, n_x=len(x_parts), n_ctx_tiles=tok.t_ctx // tm),
        grid=(tok.t // tm,),
        in_specs=tok.part_specs(tm, len(x_parts), d) + [
            pl.BlockSpec((1, 6, d), lambda m: (cm(m), 0, 0)),
            pl.BlockSpec((d, EV_W), full),
            pl.BlockSpec((1, MLA_Q_RANK), full),
            pl.BlockSpec((MLA_Q_RANK, qw), full),
            pl.BlockSpec((1, MLA_KV_RANK), full),
            tab, tab, tab, tab],
        out_specs=[pl.BlockSpec((tm, 3 * HY_DIM), row),
                   pl.BlockSpec((tm, qw), row),
                   pl.BlockSpec((tm, 2 * LANES), row)],
        out_shape=[jax.ShapeDtypeStruct((tok.t, 3 * HY_DIM), F32),
                   jax.ShapeDtypeStruct((tok.t, qw), BF16),
                   jax.ShapeDtypeStruct((tok.t, 2 * LANES), F32)],
        compiler_params=_cparams(("arbitrary",), 48),
        name="even_in",
    )(*x_parts, mod, w_in_p, gq, wuq_p, gkv, cq, sq, ck, sk)


def _odd_in_kernel(x_ref, mod_ref, w_ref, gq_ref, gk_ref, seg_ref, cq_ref, sq_ref, ck_ref, sk_ref,
                   glu_ref, q_ref, kv_ref):
    n_q = GQA_HEADS * GQA_HD
    n_k = GQA_KV_HEADS * GQA_HD
    o = 2 * CV_DIM + n_q
    seg = seg_ref[...]
    for r in range(glu_ref.shape[0] // SUB_ROWS):
        rows = slice(r * SUB_ROWS, (r + 1) * SUB_ROWS)
        h = (x_ref[rows, :] * (1.0 + mod_ref[0, 1:2, :]) + mod_ref[0, 0:1, :]).astype(BF16)
        u = jnp.dot(h, w_ref[...], preferred_element_type=F32)
        glu_ref[rows, :] = (u[:, :CV_DIM] * jax.nn.sigmoid(u[:, CV_DIM:2 * CV_DIM])).astype(BF16)
        gq, cq, sq = gq_ref[...], cq_ref[rows, :], sq_ref[rows, :]
        for j in range(n_q // LANES):
            q = u[:, 2 * CV_DIM + j * LANES:2 * CV_DIM + (j + 1) * LANES]
            qn = q * lax.rsqrt(_seg_mean(q * q, seg) + RMS_EPS) * gq
            q_ref[rows, j * LANES:(j + 1) * LANES] = _rope(qn, cq, sq, GQA_HD // 4).astype(BF16)
        k = u[:, o:o + n_k]
        kn = k * lax.rsqrt(_seg_mean(k * k, seg) + RMS_EPS) * gk_ref[...]
        kv_ref[rows, :n_k] = _rope(kn, ck_ref[rows, :], sk_ref[rows, :], GQA_HD // 4)
        kv_ref[rows, n_k:] = u[:, o + n_k:]


def _odd_in(tok, x, mod, w_in, gq, gk, tabs, tm):
    cq, sq, ck, sk = tabs
    d, n = w_in.shape
    n_q = GQA_HEADS * GQA_HD
    n_k = GQA_KV_HEADS * GQA_HD
    cm, rm = tok.cond_map(tm), tok.rope_map(tm)
    i = jnp.arange(LANES)
    seg = jnp.where((i[:, None] // GQA_HD) == (i[None, :] // GQA_HD), 1.0 / GQA_HD, 0.0).astype(BF16)
    row = lambda m: (m, 0)
    full = lambda m: (0, 0)
    tab = pl.BlockSpec((tm, LANES), lambda m: (rm(m), 0))
    return pl.pallas_call(
        _odd_in_kernel,
        grid=(tok.t // tm,),
        in_specs=[pl.BlockSpec((tm, d), row),
                  pl.BlockSpec((1, 6, d), lambda m: (cm(m), 0, 0)),
                  pl.BlockSpec((d, n), full),
                  pl.BlockSpec((1, LANES), full),
                  pl.BlockSpec((1, LANES), full),
                  pl.BlockSpec((LANES, LANES), full),
                  tab, tab, tab, tab],
        out_specs=[pl.BlockSpec((tm, CV_DIM), row),
                   pl.BlockSpec((tm, n_q), row),
                   pl.BlockSpec((tm, 2 * n_k), row)],
        out_shape=[jax.ShapeDtypeStruct((tok.t, CV_DIM), BF16),
                   jax.ShapeDtypeStruct((tok.t, n_q), BF16),
                   jax.ShapeDtypeStruct((tok.t, 2 * n_k), F32)],
        compiler_params=_cparams(("arbitrary",), 48),
        name="odd_in",
    )(x, mod, w_in, gq, gk, seg, cq, sq, ck, sk)


def _softmax_pv(q, k, v):
    s = lax.dot_general(q, k, (((1,), (1,)), ((), ())), preferred_element_type=F32)
    p = jnp.exp(s - jnp.max(s, axis=-1, keepdims=True))
    l = jnp.sum(p, axis=-1, keepdims=True)
    return jnp.dot(p.astype(BF16), v, preferred_element_type=F32) * (1.0 / l)


def _mla_attn_kernel(*refs, seg_lens, has_cache, pps):
    if has_cache:
        q_ref, cckv_ref, ckr_ref, own_ref, wk_ref, wv_ref, o_ref, ck_scr, k_scr, v_scr = refs
    else:
        q_ref, own_ref, wk_ref, wv_ref, o_ref, ck_scr, k_scr, v_scr = refs
    npair = MLA_HEADS // 2

    @pl.when((pl.program_id(1) == 0) & (pl.program_id(2) == 0))
    def _():
        off = 0
        if has_cache:
            n = seg_lens[0]
            ck_scr[0:n, :MLA_KV_RANK] = cckv_ref[0, 0].astype(BF16)
            ck_scr[0:n, MLA_KV_RANK:] = jnp.zeros((n, LANES), BF16)
            ck_scr[0:n, MLA_KV_RANK:MLA_KV_RANK + MLA_ROPE] = ckr_ref[0, 0].astype(BF16)
            off = n
        ck_scr[off:off + seg_lens[-1], :] = own_ref[...].astype(BF16)
        ck = ck_scr[...]
        for hp in range(npair):
            k_scr[hp] = jnp.dot(ck, wk_ref[hp], preferred_element_type=F32).astype(BF16)
        for h in range(MLA_HEADS):
            v_scr[h] = jnp.dot(ck[:, :MLA_KV_RANK], wv_ref[h], preferred_element_type=F32).astype(BF16)

    for lp in range(pps):
        gp = lp if pps == npair else pl.program_id(2) * pps + lp
        kp = k_scr[gp]
        q0 = q_ref[:, 2 * lp * LANES:(2 * lp + 1) * LANES]
        q1 = q_ref[:, (2 * lp + 1) * LANES:(2 * lp + 2) * LANES]
        o = _softmax_pv(q0, kp[:, :LANES], v_scr[2 * gp]) + _softmax_pv(q1, kp[:, LANES:], v_scr[2 * gp + 1])
        o_ref[:, lp * LANES:(lp + 1) * LANES] = o.astype(o_ref.dtype)


def _mla_attn(q, ckr, wk, wv, tok_off, nb, lq, tq, pps, cache=None):
    nq = lq // tq
    qb0 = tok_off // tq
    ob0 = tok_off // lq
    npair = MLA_HEADS // 2
    seg_lens = (lq,) if cache is None else (cache[0].shape[2], lq)
    lk = sum(seg_lens)
    in_specs = [pl.BlockSpec((tq, 2 * LANES * pps), lambda b, i, p: (qb0 + b * nq + i, p))]
    args = [q]
    if cache is not None:
        cckv, ckr_c, layer = cache
        past = cckv.shape[2]
        in_specs += [pl.BlockSpec((1, 1, past, MLA_KV_RANK), lambda b, i, p: (b, layer, 0, 0)),
                     pl.BlockSpec((1, 1, past, MLA_ROPE), lambda b, i, p: (b, layer, 0, 0))]
        args += [cckv, ckr_c]
    in_specs += [pl.BlockSpec((lq, 2 * LANES), lambda b, i, p: (ob0 + b, 0)),
                 pl.BlockSpec((npair, 2 * LANES, 2 * LANES), lambda b, i, p: (0, 0, 0)),
                 pl.BlockSpec((MLA_HEADS, MLA_KV_RANK, LANES), lambda b, i, p: (0, 0, 0))]
    args += [ckr, wk, wv]
    return pl.pallas_call(
        functools.partial(_mla_attn_kernel, seg_lens=seg_lens, has_cache=cache is not None, pps=pps),
        grid=(nb, nq, npair // pps),
        in_specs=in_specs,
        out_specs=pl.BlockSpec((tq, LANES * pps), lambda b, i, p: (b * nq + i, p)),
        out_shape=jax.ShapeDtypeStruct((nb * lq, MLA_HEADS * MLA_V), BF16),
        scratch_shapes=[pltpu.VMEM((lk, 2 * LANES), BF16),
                        pltpu.VMEM((npair, lk, 2 * LANES), BF16),
                        pltpu.VMEM((MLA_HEADS, lk, LANES), BF16)],
        compiler_params=_cparams(("arbitrary", "arbitrary", "arbitrary"), 48),
        name="mla_attn_lat" if cache is not None else "mla_attn_ctx",
    )(*args)


def _gqa_attn_kernel(*refs, seg_lens, has_cache, pps):
    if has_cache:
        q_ref, ck_ref, cv_ref, own_ref, sel_ref, o_ref, kv_scr, k_scr, v_scr = refs
    else:
        q_ref, own_ref, sel_ref, o_ref, kv_scr, k_scr, v_scr = refs
    n_k = GQA_KV_HEADS * GQA_HD
    npair = GQA_HEADS // 2
    group_pairs = GQA_HEADS // GQA_KV_HEADS // 2

    @pl.when((pl.program_id(1) == 0) & (pl.program_id(2) == 0))
    def _():
        off = 0
        if has_cache:
            n = seg_lens[0]
            kv_scr[0:n, :n_k] = ck_ref[0, 0].astype(BF16)
            kv_scr[0:n, n_k:] = cv_ref[0, 0].astype(BF16)
            off = n
        kv_scr[off:off + seg_lens[-1], :] = own_ref[...].astype(BF16)
        kk = kv_scr[:, :n_k]
        vv = kv_scr[:, n_k:]
        for s in range(2 * GQA_KV_HEADS):
            k_scr[s] = jnp.dot(kk, sel_ref[s], preferred_element_type=F32).astype(BF16)
            v_scr[s] = jnp.dot(vv, sel_ref[s], preferred_element_type=F32).astype(BF16)

    for lp in range(pps):
        gp = lp if pps == npair else pl.program_id(2) * pps + lp
        s0 = 2 * (gp // group_pairs)
        q = q_ref[:, lp * LANES:(lp + 1) * LANES]
        o = _softmax_pv(q, k_scr[s0], v_scr[s0]) + _softmax_pv(q, k_scr[s0 + 1], v_scr[s0 + 1])
        o_ref[:, lp * LANES:(lp + 1) * LANES] = o.astype(o_ref.dtype)


def _gqa_attn(q, kv, tok_off, nb, lq, tq, pps, cache=None):
    nq = lq // tq
    qb0 = tok_off // tq
    ob0 = tok_off // lq
    npair = GQA_HEADS // 2
    n_k = GQA_KV_HEADS * GQA_HD
    seg_lens = (lq,) if cache is None else (cache[0].shape[2], lq)
    lk = sum(seg_lens)
    src = jnp.arange(n_k)[:, None]
    dst = jnp.arange(n_k)[None, :]
    sel = jnp.stack([jnp.where((src // GQA_HD == kvh) & (dst // GQA_HD == i) & (src % GQA_HD == dst % GQA_HD), 1.0, 0.0)
                     for kvh in range(GQA_KV_HEADS) for i in range(2)]).astype(BF16)
    in_specs = [pl.BlockSpec((tq, LANES * pps), lambda b, i, p: (qb0 + b * nq + i, p))]
    args = [q]
    if cache is not None:
        ck, cv, layer = cache
        past = ck.shape[2]
        in_specs += [pl.BlockSpec((1, 1, past, n_k), lambda b, i, p: (b, layer, 0, 0)),
                     pl.BlockSpec((1, 1, past, n_k), lambda b, i, p: (b, layer, 0, 0))]
        args += [ck, cv]
    in_specs += [pl.BlockSpec((lq, 2 * n_k), lambda b, i, p: (ob0 + b, 0)),
                 pl.BlockSpec((2 * GQA_KV_HEADS, n_k, n_k), lambda b, i, p: (0, 0, 0))]
    args += [kv, sel]
    return pl.pallas_call(
        functools.partial(_gqa_attn_kernel, seg_lens=seg_lens, has_cache=cache is not None, pps=pps),
        grid=(nb, nq, npair // pps),
        in_specs=in_specs,
        out_specs=pl.BlockSpec((tq, LANES * pps), lambda b, i, p: (b * nq + i, p)),
        out_shape=jax.ShapeDtypeStruct((nb * lq, GQA_HEADS * GQA_HD), BF16),
        scratch_shapes=[pltpu.VMEM((lk, 2 * n_k), BF16),
                        pltpu.VMEM((2 * GQA_KV_HEADS, lk, n_k), BF16),
                        pltpu.VMEM((2 * GQA_KV_HEADS, lk, n_k), BF16)],
        compiler_params=_cparams(("arbitrary", "arbitrary", "arbitrary"), 48),
        name="gqa_attn_lat" if cache is not None else "gqa_attn_ctx",
    )(*args)


def _dft_mats(lb):
    n = 2 * lb
    r = jnp.arange(n, dtype=jnp.int32)
    nyq = r == lb
    f = jnp.where(nyq, lb, r % lb)
    is_im = (r >= lb) & ~nyq
    s = jnp.arange(lb, dtype=jnp.int32)

    def mat(pos):
        ang = ((f[:, None] * pos[None, :]) % n).astype(F32) * (2.0 * math.pi / n)
        return jnp.where(is_im[:, None], -jnp.sin(ang), jnp.cos(ang))

    fwd = mat(s)
    fwd_rev = jnp.where(s[None, :] == 0, 0.0, mat(lb - s))
    wgt = jnp.where((r == 0) | nyq, 1.0 / n, 2.0 / n)
    inv = (fwd * wgt[:, None]).T
    return fwd.astype(BF16), inv.astype(BF16), fwd_rev.astype(BF16)


def _spec_kernel(f_ref, fr_ref, ka_ref, kb_ref, p_ref, q_ref, p2_ref, *, lb):
    tf = jnp.dot(f_ref[...], ka_ref[...], preferred_element_type=F32)
    tb = jnp.dot(fr_ref[...], kb_ref[...], preferred_element_type=F32)
    re = tf[:lb] + tb[:lb]
    im = tf[lb:] - tb[lb:]
    nyq = tf[lb:] + tb[lb:]
    row0 = lax.broadcasted_iota(jnp.int32, re.shape, 0) == 0
    p_ref[0] = re
    q_ref[0] = jnp.where(row0, 0.0, im)
    p2_ref[0] = jnp.where(row0, nyq, re)


def _filter_spectrum(mats, k2, dmax):
    fwd_mat, _, fwd_rev = mats
    n, lb = fwd_mat.shape
    cw = k2.shape[1]
    nbk = k2.shape[0] // n
    nd = 2 * dmax + 1
    tc = 512
    out = jax.ShapeDtypeStruct((nd, lb, cw), F32)
    blk = pl.BlockSpec((1, lb, tc), lambda c, d: (d, 0, c))
    mat = pl.BlockSpec((n, lb), lambda c, d: (0, 0))
    return pl.pallas_call(
        functools.partial(_spec_kernel, lb=lb),
        grid=(cw // tc, nd),
        in_specs=[mat, mat,
                  pl.BlockSpec((lb, tc), lambda c, d: (nbk + d - dmax, c)),
                  pl.BlockSpec((lb, tc), lambda c, d: (nbk + d - dmax - 1, c))],
        out_specs=[blk, blk, blk],
        out_shape=[out, out, out],
        compiler_params=_cparams(("arbitrary", "arbitrary"), 32),
        name="filter_spectrum",
    )(fwd_mat, fwd_rev, k2, k2)


def _block_conv(zbf_ref, z_scr, f_ref, fi_ref, p_ref, q_ref, p2_ref, lb, nblk, nbk, dmax, emit):
    for r in range(nblk):
        z_scr[r] = jnp.dot(f_ref[...], zbf_ref[r * lb:(r + 1) * lb, :], preferred_element_type=F32)
    for r in range(nblk):
        s, i = divmod(r, nbk)
        ya = yb = None
        for j in range(nbk):
            d = i - j
            if abs(d) > dmax:
                continue
            re = z_scr[s * nbk + j, :lb, :]
            im = z_scr[s * nbk + j, lb:, :]
            p, q, p2 = p_ref[d + dmax], q_ref[d + dmax], p2_ref[d + dmax]
            ta = re * p - im * q
            tb = re * q + im * p2
            ya = ta if ya is None else ya + ta
            yb = tb if yb is None else yb + tb
        y = jnp.concatenate([ya, yb], axis=0).astype(BF16)
        emit(r, jnp.dot(fi_ref[...], y, preferred_element_type=F32))


def _short_conv(u, w_ref, b_ref, l):
    rows = u.shape[0]
    t = jnp.bitwise_and(lax.broadcasted_iota(jnp.int32, u.shape, 0), l - 1)
    prev = jnp.where(t == 0, 0.0, pltpu.roll(u, 1, 0))
    nxt = jnp.where(t == l - 1, 0.0, pltpu.roll(u, rows - 1, 0))
    return w_ref[0:1, :] * prev + w_ref[1:2, :] * u + w_ref[2:3, :] * nxt + b_ref[...]


def _hyena_kernel(*refs, l, lb, nseq, dmax, conv_z):
    if conv_z:
        (uz_ref, ug_ref, cwz_ref, cbz_ref, cwg_ref, cbg_ref, skip_ref,
         f_ref, fi_ref, p_ref, q_ref, p2_ref, o_ref, zf_scr, g_scr, zbf_scr, z_scr) = refs
    else:
        (uz_ref, ug_ref, cwg_ref, cbg_ref, skip_ref,
         f_ref, fi_ref, p_ref, q_ref, p2_ref, o_ref, zf_scr, g_scr, zbf_scr, z_scr) = refs
    z = _short_conv(uz_ref[...], cwz_ref, cbz_ref, l) if conv_z else uz_ref[...]
    zf_scr[...] = z
    zbf_scr[...] = z.astype(BF16)
    g_scr[...] = _short_conv(ug_ref[...], cwg_ref, cbg_ref, l)
    skip = skip_ref[...]

    def emit(r, y):
        rows = slice(r * lb, (r + 1) * lb)
        o_ref[rows, :] = (g_scr[rows, :] * (y + skip * zf_scr[rows, :])).astype(o_ref.dtype)

    nbk = l // lb
    _block_conv(zbf_scr, z_scr, f_ref, fi_ref, p_ref, q_ref, p2_ref, lb, nseq * nbk, nbk, dmax, emit)


def _hyena_stage(z_src, z_col, u_hy, g_col, conv_w, conv_b, skip, mats, spec, spec_col,
                 tok_off, nb, l, nseq, tc, conv_z, out_dtype):
    fwd_mat, inv_mat, _ = mats
    p_arr, q_arr, p2_arr = spec
    n, lb = fwd_mat.shape
    nd = p_arr.shape[0]
    dmax = nd // 2
    rows = nseq * l
    rb0 = tok_off // rows
    zrb0 = rb0 if conv_z else 0
    cpb = HY_DIM // tc
    cvec = lambda g: pl.BlockSpec((1, tc), lambda c, b: (0, g * cpb + c))
    in_specs = [pl.BlockSpec((rows, tc), lambda c, b: (zrb0 + b, z_col * cpb + c)),
                pl.BlockSpec((rows, tc), lambda c, b: (rb0 + b, g_col * cpb + c))]
    args = [z_src, u_hy]
    if conv_z:
        in_specs += [pl.BlockSpec((3, tc), lambda c, b: (0, z_col * cpb + c)), cvec(z_col)]
        args += [conv_w, conv_b]
    in_specs += [pl.BlockSpec((3, tc), lambda c, b: (0, g_col * cpb + c)), cvec(g_col),
                 pl.BlockSpec((1, tc), lambda c, b: (0, c)),
                 pl.BlockSpec((n, lb), lambda c, b: (0, 0)),
                 pl.BlockSpec((lb, n), lambda c, b: (0, 0))]
    args += [conv_w, conv_b, skip, fwd_mat, inv_mat]
    sspec = pl.BlockSpec((nd, lb, tc), lambda c, b: (0, 0, spec_col * cpb + c), pipeline_mode=pl.Buffered(1))
    in_specs += [sspec, sspec, sspec]
    args += [p_arr, q_arr, p2_arr]
    nblk = rows // lb
    return pl.pallas_call(
        functools.partial(_hyena_kernel, l=l, lb=lb, nseq=nseq, dmax=dmax, conv_z=conv_z),
        grid=(HY_DIM // tc, nb // nseq),
        in_specs=in_specs,
        out_specs=pl.BlockSpec((rows, tc), lambda c, b: (b, c)),
        out_shape=jax.ShapeDtypeStruct((nb * l, HY_DIM), out_dtype),
        scratch_shapes=[pltpu.VMEM((rows, tc), F32), pltpu.VMEM((rows, tc), F32),
                        pltpu.VMEM((rows, tc), BF16), pltpu.VMEM((nblk, n, tc), F32)],
        compiler_params=_cparams(("arbitrary", "arbitrary"), 56),
        name="hyena_stage",
    )(*args)


def _conformer_kernel(glu_ref, b_ref, lng_ref, lnb_ref, f_ref, fi_ref, p_ref, q_ref, p2_ref, o_ref, z_scr,
                      *, l, lb, nseq, dmax):
    bias, lng, lnb = b_ref[...], lng_ref[...], lnb_ref[...]

    def emit(r, y):
        yn = _layer_norm(y + bias, lng, lnb)
        o_ref[r * lb:(r + 1) * lb, :] = (yn * jax.nn.sigmoid(yn)).astype(o_ref.dtype)

    nbk = l // lb
    _block_conv(glu_ref, z_scr, f_ref, fi_ref, p_ref, q_ref, p2_ref, lb, nseq * nbk, nbk, dmax, emit)


def _conformer(glu, bias, ln_g, ln_b, mats, spec, tok_off, nb, l, nseq):
    fwd_mat, inv_mat, _ = mats
    n, lb = fwd_mat.shape
    nd = spec[0].shape[0]
    rows = nseq * l
    rb0 = tok_off // rows
    c = CV_DIM
    vec = pl.BlockSpec((1, c), lambda b: (0, 0))
    sspec = pl.BlockSpec((nd, lb, c), lambda b: (0, 0, 0))
    return pl.pallas_call(
        functools.partial(_conformer_kernel, l=l, lb=lb, nseq=nseq, dmax=nd // 2),
        grid=(nb // nseq,),
        in_specs=[pl.BlockSpec((rows, c), lambda b: (rb0 + b, 0)),
                  vec, vec, vec,
                  pl.BlockSpec((n, lb), lambda b: (0, 0)),
                  pl.BlockSpec((lb, n), lambda b: (0, 0)),
                  sspec, sspec, sspec],
        out_specs=pl.BlockSpec((rows, c), lambda b: (b, 0)),
        out_shape=jax.ShapeDtypeStruct((nb * l, c), BF16),
        scratch_shapes=[pltpu.VMEM((rows // lb, n, c), F32)],
        compiler_params=_cparams(("arbitrary",), 56),
        name="conformer",
    )(glu, bias, ln_g, ln_b, fwd_mat, inv_mat, *spec)


def _filter_kernel(z_ref, w1_ref, b1_ref, w2_ref, b2_ref, w3_ref, freq_ref, decay_ref, o_ref):
    hp = lax.Precision.HIGHEST
    z = z_ref[...]
    freq = freq_ref[...]
    hid = jnp.sin(freq * (jnp.dot(z, w1_ref[...], precision=hp, preferred_element_type=F32) + b1_ref[...]))
    hid = jnp.sin(freq * (jnp.dot(hid, w2_ref[...], precision=hp, preferred_element_type=F32) + b2_ref[...]))
    h = jnp.dot(hid, w3_ref[...], precision=hp, preferred_element_type=F32)
    h = h * jnp.exp(-z[:, 0:1] * jnp.abs(decay_ref[...]))
    row = lax.broadcasted_iota(jnp.int32, h.shape, 0) + pl.program_id(0) * h.shape[0]
    o_ref[...] = jnp.where(row == 0, 0.0, h).astype(o_ref.dtype)


def _hyena_filter(l, w1, b1, w2, b2, w3, freq, decay):
    j = jnp.arange(2 * l, dtype=jnp.int32)
    t = jnp.abs(j - l).astype(F32) / l
    bands = jnp.arange(1, HY_BANDS + 1, dtype=F32)
    ang = 2.0 * math.pi * t[:, None] * bands[None, :]
    z = jnp.concatenate([t[:, None], jnp.cos(ang), jnp.sin(ang), jnp.zeros((2 * l, LANES - HY_EMB), F32)], axis=-1)
    w1p = jnp.concatenate([w1, jnp.zeros((LANES - HY_EMB, HY_FILT_HID), F32)], axis=0)
    nout = w3.shape[1] // 2
    tl = 256
    n_bwd = l // tl
    full = lambda i: (0, 0)
    half = lambda i: (0, jnp.where(i < n_bwd, 1, 0))
    return pl.pallas_call(
        _filter_kernel,
        grid=(2 * l // tl,),
        in_specs=[pl.BlockSpec((tl, LANES), lambda i: (i, 0)),
                  pl.BlockSpec((LANES, HY_FILT_HID), full),
                  pl.BlockSpec((1, HY_FILT_HID), full),
                  pl.BlockSpec((HY_FILT_HID, HY_FILT_HID), full),
                  pl.BlockSpec((1, HY_FILT_HID), full),
                  pl.BlockSpec((HY_FILT_HID, nout), half),
                  pl.BlockSpec((1, HY_FILT_HID), full),
                  pl.BlockSpec((1, nout), half)],
        out_specs=pl.BlockSpec((tl, nout), lambda i: (i, 0)),
        out_shape=jax.ShapeDtypeStruct((2 * l, nout), BF16),
        compiler_params=_cparams(("arbitrary",), 32),
        name="hyena_filter",
    )(z, w1p, b1[None], w2, b2[None], w3, freq[None], decay[None])


def _out_proj_kernel(*refs, n_x, n_ctx_tiles):
    ya_refs, yb_refs, x_refs = refs[0:2], refs[2:4], refs[4:4 + n_x]
    mod_ref, wa_ref, wb_ref, g_ref, b_ref, o_ref = refs[4 + n_x:]
    for r in range(o_ref.shape[0] // SUB_ROWS):
        rows = slice(r * SUB_ROWS, (r + 1) * SUB_ROWS)
        y = (jnp.dot(_read_parts(ya_refs, n_ctx_tiles, rows), wa_ref[...], preferred_element_type=F32)
             + jnp.dot(_read_parts(yb_refs, n_ctx_tiles, rows), wb_ref[...], preferred_element_type=F32))
        z = ALPHA * _read_parts(x_refs, n_ctx_tiles, rows) + mod_ref[0, 2:3, :] * y
        o_ref[rows, :] = _layer_norm(z, g_ref[...], b_ref[...])


def _out_proj(tok, ya_parts, yb_parts, x_parts, mod, w_out, ln_g, ln_b, tm):
    d = x_parts[0].shape[1]
    ka, kb = ya_parts[0].shape[1], yb_parts[0].shape[1]
    cm = tok.cond_map(tm)
    full = lambda m: (0, 0)
    return pl.pallas_call(
        functools.partial(_out_proj_kernel, n_x=len(x_parts), n_ctx_tiles=tok.t_ctx // tm),
        grid=(tok.t // tm,),
        in_specs=(tok.part_specs(tm, 2, ka) + tok.part_specs(tm, 2, kb) + tok.part_specs(tm, len(x_parts), d) + [
            pl.BlockSpec((1, 6, d), lambda m: (cm(m), 0, 0)),
            pl.BlockSpec((ka, d), lambda m: (0, 0)),
            pl.BlockSpec((kb, d), lambda m: (ka // kb, 0)),
            pl.BlockSpec((1, d), full),
            pl.BlockSpec((1, d), full)]),
        out_specs=pl.BlockSpec((tm, d), lambda m: (m, 0)),
        out_shape=jax.ShapeDtypeStruct((tok.t, d), F32),
        compiler_params=_cparams(("arbitrary",), 40),
        name="out_proj_ln",
    )(*ya_parts, *yb_parts, *x_parts, mod, w_out, w_out, ln_g, ln_b)


MLP_CHUNK = 512


def _mlp_kernel(x_ref, mod_ref, w1_ref, b1_ref, w2_ref, b2_ref, g_ref, b_ref, o_ref, h_scr, acc_scr, *, nf):
    f = pl.program_id(1)

    @pl.when(f == 0)
    def _():
        h_scr[...] = (x_ref[...] * (1.0 + mod_ref[0, 4:5, :]) + mod_ref[0, 3:4, :]).astype(BF16)
        acc_scr[...] = jnp.zeros_like(acc_scr)

    part = None
    for c0 in range(0, w1_ref.shape[1], MLP_CHUNK):
        cols = slice(c0, c0 + MLP_CHUNK)
        a = jnp.maximum(jnp.dot(h_scr[...], w1_ref[:, cols], preferred_element_type=F32) + b1_ref[:, cols], 0.0)
        y = jnp.dot((a * a).astype(BF16), w2_ref[cols, :], preferred_element_type=F32)
        part = y if part is None else part + y
    acc_scr[...] += part

    @pl.when(f == nf - 1)
    def _():
        z = ALPHA * x_ref[...] + mod_ref[0, 5:6, :] * (acc_scr[...] + b2_ref[...])
        o_ref[...] = _layer_norm(z, g_ref[...], b_ref[...])


def _mlp(tok, x, mod, w1, b1, w2, b2, ln_g, ln_b, tm, tf, tok_off, n_rows):
    d = x.shape[1]
    dff = w1.shape[1]
    nf = dff // tf
    m0 = tok_off // tm
    cm = tok.cond_map(tm)
    return pl.pallas_call(
        functools.partial(_mlp_kernel, nf=nf),
        grid=(n_rows // tm, nf),
        in_specs=[pl.BlockSpec((tm, d), lambda m, f: (m0 + m, 0)),
                  pl.BlockSpec((1, 6, d), lambda m, f: (cm(m0 + m), 0, 0)),
                  pl.BlockSpec((d, tf), lambda m, f: (0, f)),
                  pl.BlockSpec((1, tf), lambda m, f: (0, f)),
                  pl.BlockSpec((tf, d), lambda m, f: (f, 0)),
                  pl.BlockSpec((1, d), lambda m, f: (0, 0)),
                  pl.BlockSpec((1, d), lambda m, f: (0, 0)),
                  pl.BlockSpec((1, d), lambda m, f: (0, 0))],
        out_specs=pl.BlockSpec((tm, d), lambda m, f: (m, 0)),
        out_shape=jax.ShapeDtypeStruct((n_rows, d), F32),
        scratch_shapes=[pltpu.VMEM((tm, d), BF16), pltpu.VMEM((tm, d), F32)],
        compiler_params=_cparams(("arbitrary", "arbitrary"), 56),
        name="mlp_ln",
    )(x, mod, w1, b1, w2, b2, ln_g, ln_b)


def _mla_weights(w_uq, w_ukv):
    hd = MLA_NOPE + MLA_ROPE
    wq = w_uq.reshape(MLA_Q_RANK, MLA_HEADS, hd)
    wq = jnp.pad(wq, ((0, 0), (0, 0), (0, LANES - hd))).reshape(MLA_Q_RANK, MLA_HEADS * LANES)
    wkv = w_ukv.reshape(MLA_KV_RANK, MLA_HEADS, MLA_NOPE + MLA_V).transpose(1, 0, 2)
    w_nope, w_v = wkv[..., :MLA_NOPE], wkv[..., MLA_NOPE:]
    top = jnp.pad(w_nope, ((0, 0), (0, 0), (0, LANES - MLA_NOPE)))
    place = jnp.pad(jnp.eye(MLA_ROPE, dtype=F32), ((0, LANES - MLA_ROPE), (MLA_NOPE, LANES - MLA_NOPE - MLA_ROPE)))
    wk = jnp.concatenate([top, jnp.broadcast_to(place, (MLA_HEADS, LANES, LANES))], axis=1)
    wk = wk.reshape(MLA_HEADS // 2, 2, 2 * LANES, LANES).transpose(0, 2, 1, 3).reshape(MLA_HEADS // 2, 2 * LANES, 2 * LANES)
    w_v = w_v.reshape(MLA_HEADS // 2, 2, MLA_KV_RANK, MLA_V)
    wv = jnp.stack([jnp.pad(w_v[:, 0], ((0, 0), (0, 0), (0, LANES - MLA_V))),
                    jnp.pad(w_v[:, 1], ((0, 0), (0, 0), (LANES - MLA_V, 0)))], axis=1)
    wv = wv.reshape(MLA_HEADS, MLA_KV_RANK, LANES)
    return wq.astype(BF16), wk.astype(BF16), wv.astype(BF16)


def _conformer_taps(dw_w, l):
    k, c = dw_w.shape
    half = k // 2
    return jnp.pad(dw_w[::-1], ((l - half, l - (k - half)), (0, 0))).astype(BF16)


def kernel(x_prompt, x_sample, cache_mla_ckv, cache_mla_krope, cache_gqa_k, cache_gqa_v, c, c_ctx, ev_w_in, hy_conv_w, hy_conv_b, hy_filt_w1, hy_filt_b1, hy_filt_w2, hy_filt_b2, hy_filt_w3, hy_sin_freq, hy_decay, hy_skip, mla_q_norm_g, mla_w_uq, mla_kv_norm_g, mla_w_ukv, ev_w_out, od_w_in, cv_dw_w, cv_dw_b, cv_ln_g, cv_ln_b, gqa_q_norm_g, gqa_k_norm_g, od_w_out, ada_w, ada_b, ln_g, ln_b, mlp_w1, mlp_b1, mlp_w2, mlp_b2):
    bc, lc, d = x_prompt.shape
    bl, ll, _ = x_sample.shape
    tok = _Tok(bc, lc, bl, ll)
    assert lc & (lc - 1) == 0 and ll & (ll - 1) == 0 and tok.t_ctx % ll == 0 and bl < 8
    n_od = od_w_in.shape[0]
    tm_in = min(512, ll)
    tm_mlp = min(1024, ll)
    tf_mlp = 1024
    tq = 256
    tq_lat = min(512, ll)
    seq_ctx = max(1, 1024 // lc)
    tc_lat = 256 if ll > 1024 else HY_DIM
    cv_half = cv_dw_w.shape[1] // 2

    cond8 = jnp.concatenate([c_ctx[None], c, jnp.zeros((7 - bl, d), F32)], axis=0)
    mods = _ada(cond8, ada_w, ada_b).reshape(DEPTH, 8, 6, d)

    mla_scale = (MLA_NOPE + MLA_ROPE) ** -0.5
    ev_tabs = (_rope_tables(ll, tm_in, 1, LANES, MLA_NOPE, MLA_ROPE, mla_scale)
               + _rope_tables(ll, tm_in, 1, LANES, 0, MLA_ROPE, 1.0))
    od_tabs = (_rope_tables(ll, tm_in, LANES // GQA_HD, GQA_HD, 0, GQA_HD, GQA_HD ** -0.5)
               + _rope_tables(ll, tm_in, LANES // GQA_HD, GQA_HD, 0, GQA_HD, 1.0))
    passes = []
    for off, nb, l, nseq, tc in ((0, bc, lc, seq_ctx, HY_DIM), (tok.t_ctx, bl, ll, 1, tc_lat)):
        lb = min(l, CONV_BLOCK)
        passes.append((off, nb, l, nseq, tc, lb, _dft_mats(lb)))

    x_parts = (x_prompt.reshape(tok.t_ctx, d), x_sample.reshape(tok.t_lat, d))
    ckv_list, krope_list, k_list, v_list = [], [], [], []

    for layer in range(DEPTH):
        i = layer // 2
        mod = mods[layer]
        if layer % 2 == 0:
            w_in_p = jnp.pad(ev_w_in[i], ((0, 0), (0, EV_W - ev_w_in.shape[2]))).astype(BF16)
            wq, wk, wv = _mla_weights(mla_w_uq[i], mla_w_ukv[i])
            u_hy, q, ckr = _even_in(tok, x_parts, mod, w_in_p, mla_q_norm_g[i][None], wq, mla_kv_norm_g[i][None],
                                    ev_tabs, tm_in)
            ckv_list.append(ckr[:tok.t_ctx, :MLA_KV_RANK].reshape(bc, lc, MLA_KV_RANK))
            krope_list.append(ckr[:tok.t_ctx, MLA_KV_RANK:MLA_KV_RANK + MLA_ROPE].reshape(bc, lc, MLA_ROPE))
            yb_parts = (_mla_attn(q, ckr, wk, wv, 0, bc, lc, min(tq, lc), MLA_HEADS // 2),
                        _mla_attn(q, ckr, wk, wv, tok.t_ctx, bl, ll, tq_lat, 1,
                                  cache=(cache_mla_ckv, cache_mla_krope, i)))

            skip = hy_skip[i]
            conv_b = hy_conv_b[i][None]
            ya_parts = []
            for (off, nb, l, nseq, tc, lb, mats) in passes:
                k2 = _hyena_filter(l, hy_filt_w1[i], hy_filt_b1[i], hy_filt_w2[i], hy_filt_b2[i],
                                   hy_filt_w3[i], hy_sin_freq[i], hy_decay[i])
                spec = _filter_spectrum(mats, k2, l // lb - 1)
                z1 = _hyena_stage(u_hy, 2, u_hy, 0, hy_conv_w[i], conv_b, skip[0:1], mats, spec, 0,
                                  off, nb, l, nseq, tc, True, F32)
                ya_parts.append(_hyena_stage(z1, 0, u_hy, 1, hy_conv_w[i], conv_b, skip[1:2], mats, spec, 1,
                                             off, nb, l, nseq, tc, False, BF16))
            w_out = ev_w_out[i].astype(BF16)
        else:
            (x,) = x_parts
            gq = jnp.tile(gqa_q_norm_g[i], LANES // GQA_HD)[None]
            gk = jnp.tile(gqa_k_norm_g[i], GQA_KV_HEADS)[None]
            glu, q, kv = _odd_in(tok, x, mod, od_w_in[i].astype(BF16), gq, gk, od_tabs, tm_in)
            n_k = GQA_KV_HEADS * GQA_HD
            k_list.append(kv[:tok.t_ctx, :n_k].reshape(bc, lc, GQA_KV_HEADS, GQA_HD))
            v_list.append(kv[:tok.t_ctx, n_k:].reshape(bc, lc, GQA_KV_HEADS, GQA_HD))
            past = cache_gqa_k.shape[2]
            cache = (cache_gqa_k.reshape(bl, n_od, past, n_k), cache_gqa_v.reshape(bl, n_od, past, n_k), i)
            yb_parts = (_gqa_attn(q, kv, 0, bc, lc, min(tq, lc), GQA_HEADS // 2),
                        _gqa_attn(q, kv, tok.t_ctx, bl, ll, tq_lat, 1, cache=cache))

            ya_parts = []
            for (off, nb, l, nseq, tc, lb, mats) in passes:
                dmax = min(l // lb - 1, -(-cv_half // lb))
                spec = _filter_spectrum(mats, _conformer_taps(cv_dw_w[i], l), dmax)
                ya_parts.append(_conformer(glu, cv_dw_b[i][None], cv_ln_g[i][None], cv_ln_b[i][None], mats, spec,
                                           off, nb, l, nseq))
            w_out = od_w_out[i].astype(BF16)

        x = _out_proj(tok, ya_parts, yb_parts, x_parts, mod, w_out, ln_g[layer, 0][None], ln_b[layer, 0][None], tm_in)
        mlp_args = (mod, mlp_w1[layer].astype(BF16), mlp_b1[layer][None], mlp_w2[layer].astype(BF16),
                    mlp_b2[layer][None], ln_g[layer, 1][None], ln_b[layer, 1][None], tm_mlp, tf_mlp)
        if layer < DEPTH - 1:
            x_parts = (_mlp(tok, x, *mlp_args, 0, tok.t),)
        else:
            y_prompt = _mlp(tok, x, *mlp_args, 0, tok.t_ctx).reshape(bc, lc, d)
            y_sample = _mlp(tok, x, *mlp_args, tok.t_ctx, tok.t_lat).reshape(bl, ll, d)

    return (y_prompt, y_sample, jnp.stack(ckv_list, axis=1), jnp.stack(krope_list, axis=1),
            jnp.stack(k_list, axis=1), jnp.stack(v_list, axis=1))
```

```python
import functools
import math

import jax
import jax.numpy as jnp
from jax import lax
from jax.experimental import pallas as pl
from jax.experimental.pallas import tpu as pltpu

F32 = jnp.float32
BF16 = jnp.bfloat16

DEPTH = 4
GRID_W = 64
ALPHA = (2.0 * DEPTH) ** 0.25
LN_EPS = 1e-5
RMS_EPS = 1e-6
ROPE_THETA = 10000.0

HY_DIM = 512
HY_ORDER = 2
HY_BANDS = 16
HY_EMB = 2 * HY_BANDS + 1
HY_FILT_HID = 64

MLA_HEADS = 8
MLA_NOPE = 64
MLA_ROPE = 32
MLA_V = 64
MLA_Q_RANK = 256
MLA_KV_RANK = 128

CV_DIM = 512

GQA_HEADS = 8
GQA_KV_HEADS = 2
GQA_HD = 64

LANES = 128
CONV_BLOCK = 512
MIB = 2 ** 20


def _cparams(sem, vmem_mib):
    return pltpu.CompilerParams(dimension_semantics=sem, vmem_limit_bytes=vmem_mib * MIB)


def _layer_norm(z, g, b):
    mu = jnp.mean(z, axis=-1, keepdims=True)
    zc = z - mu
    var = jnp.mean(zc * zc, axis=-1, keepdims=True)
    return zc * lax.rsqrt(var + LN_EPS) * g + b


def _rms(x, g):
    return x * lax.rsqrt(jnp.mean(x * x, axis=-1, keepdims=True) + RMS_EPS) * g


def _seg_mean(sq, s_mat):
    hi = sq.astype(BF16)
    lo = (sq - hi.astype(F32)).astype(BF16)
    return (jnp.dot(hi, s_mat, preferred_element_type=F32)
            + jnp.dot(lo, s_mat, preferred_element_type=F32))


def _rope(x, cos, sin_signed, half):
    w = x.shape[1]
    lane = lax.broadcasted_iota(jnp.int32, x.shape, 1)
    first = jnp.bitwise_and(lane, 2 * half - 1) < half
    rot = jnp.where(first, pltpu.roll(x, w - half, 1), pltpu.roll(x, half, 1))
    return x * cos + rot * sin_signed


def _ada_kernel(c_ref, w_ref, b_ref, o_ref):
    c = c_ref[...]
    s = (c * jax.nn.sigmoid(c)).astype(BF16)
    o_ref[0] = jnp.dot(s, w_ref[0].astype(BF16), preferred_element_type=F32) + b_ref[0]


def _ada(cond8, ada_w, ada_b):
    depth, d, n = ada_w.shape
    tn = 1536
    return pl.pallas_call(
        _ada_kernel,
        grid=(depth, n // tn),
        in_specs=[pl.BlockSpec((8, d), lambda l, j: (0, 0)),
                  pl.BlockSpec((1, d, tn), lambda l, j: (l, 0, j)),
                  pl.BlockSpec((1, 1, tn), lambda l, j: (l, 0, j))],
        out_specs=pl.BlockSpec((1, 8, tn), lambda l, j: (l, 0, j)),
        out_shape=jax.ShapeDtypeStruct((depth, 8, n), F32),
        compiler_params=_cparams(("arbitrary", "arbitrary"), 32),
        name="ada_mod",
    )(cond8, ada_w, ada_b.reshape(depth, 1, n))


class _Tok:
    def __init__(self, bc, lc, bl, ll):
        self.bc, self.lc, self.bl, self.ll = bc, lc, bl, ll
        self.t_ctx = bc * lc
        self.t_lat = bl * ll
        self.t = self.t_ctx + self.t_lat

    def cond_map(self, tm):
        n_ctx, per_b = self.t_ctx // tm, self.ll // tm
        return lambda m: jnp.where(m < n_ctx, 0, 1 + (m - n_ctx) // per_b)

    def rope_map(self, tm):
        n_ctx, per_b = self.t_ctx // tm, self.ll // tm
        return lambda m: jnp.where(m < n_ctx, 0, 1 + (m - n_ctx) % per_b)

    def part_specs(self, tm, n_parts, width):
        if n_parts == 1:
            return [pl.BlockSpec((tm, width), lambda m: (m, 0))]
        n_ctx = self.t_ctx // tm
        return [pl.BlockSpec((tm, width), lambda m: (jnp.minimum(m, n_ctx - 1), 0)),
                pl.BlockSpec((tm, width), lambda m: (jnp.maximum(m - n_ctx, 0), 0))]


def _read_parts(refs, n_ctx_tiles, rows=slice(None)):
    if len(refs) == 1:
        return refs[0][rows, :]
    return jnp.where(pl.program_id(0) < n_ctx_tiles, refs[0][rows, :], refs[1][rows, :])


SUB_ROWS = 256


def _rope_tables(ll, tm, n_heads, head_w, rope_off, rope_dim, scale):
    half = rope_dim // 2
    quarter = half // 2
    t = jnp.arange(ll, dtype=F32)
    row = jnp.floor(t / GRID_W)
    col = t - row * GRID_W
    inv = ROPE_THETA ** (-jnp.arange(0, half, 2, dtype=F32) / half)
    ang_r = row[:, None] * inv[None, :]
    ang_c = col[:, None] * inv[None, :]
    ang = jnp.concatenate([ang_r, ang_r, ang_c, ang_c], axis=-1)
    sign = jnp.concatenate([-jnp.ones((quarter,), F32), jnp.ones((quarter,), F32)] * 2)
    pad = ((0, 0), (rope_off, head_w - rope_off - rope_dim))
    cos_h = jnp.pad(jnp.cos(ang), pad, constant_values=1.0)
    sin_h = jnp.pad(jnp.sin(ang) * sign, pad)
    cos_l = jnp.tile(cos_h, (1, n_heads))
    sin_l = jnp.tile(sin_h, (1, n_heads))
    w = n_heads * head_w
    cos = jnp.concatenate([jnp.ones((tm, w), F32), cos_l], axis=0) * scale
    sin = jnp.concatenate([jnp.zeros((tm, w), F32), sin_l], axis=0) * scale
    return cos, sin


EV_W = 2048


def _even_in_kernel(*refs, n_x, n_ctx_tiles):
    x_refs = refs[:n_x]
    (mod_ref, w_ref, gq_ref, wuq_ref, gkv_ref, cq_ref, sq_ref, ck_ref, sk_ref,
     uhy_ref, q_ref, ckr_ref) = refs[n_x:]
    n_hy = 3 * HY_DIM
    o = n_hy + MLA_Q_RANK
    for r in range(uhy_ref.shape[0] // SUB_ROWS):
        rows = slice(r * SUB_ROWS, (r + 1) * SUB_ROWS)
        x = _read_parts(x_refs, n_ctx_tiles, rows)
        h = (x * (1.0 + mod_ref[0, 1:2, :]) + mod_ref[0, 0:1, :]).astype(BF16)
        u = jnp.dot(h, w_ref[...], preferred_element_type=F32)
        uhy_ref[rows, :] = u[:, :n_hy]
        cqn = _rms(u[:, n_hy:n_hy + MLA_Q_RANK], gq_ref[...])
        q = jnp.dot(cqn.astype(BF16), wuq_ref[...], preferred_element_type=F32)
        cq, sq = cq_ref[rows, :], sq_ref[rows, :]
        for hd in range(MLA_HEADS):
            cols = slice(hd * LANES, (hd + 1) * LANES)
            q_ref[rows, cols] = _rope(q[:, cols], cq, sq, MLA_ROPE // 4).astype(BF16)
        ckr_ref[rows, :MLA_KV_RANK] = _rms(u[:, o:o + MLA_KV_RANK], gkv_ref[...])
        ckr_ref[rows, MLA_KV_RANK:] = _rope(u[:, o + MLA_KV_RANK:], ck_ref[rows, :], sk_ref[rows, :], MLA_ROPE // 4)


def _even_in(tok, x_parts, mod, w_in_p, gq, wuq_p, gkv, tabs, tm):
    cq, sq, ck, sk = tabs
    d = x_parts[0].shape[1]
    cm, rm = tok.cond_map(tm), tok.rope_map(tm)
    qw = MLA_HEADS * LANES
    row = lambda m: (m, 0)
    full = lambda m: (0, 0)
    tab = pl.BlockSpec((tm, LANES), lambda m: (rm(m), 0))
    return pl.pallas_call(
        functools.partial(_even_in_kernel, n_x=len(x_parts), n_ctx_tiles=tok.t_ctx // tm),
        grid=(tok.t // tm,),
        in_specs=tok.part_specs(tm, len(x_parts), d) + [
            pl.BlockSpec((1, 6, d), lambda m: (cm(m), 0, 0)),
            pl.BlockSpec((d, EV_W), full),
            pl.BlockSpec((1, MLA_Q_RANK), full),
            pl.BlockSpec((MLA_Q_RANK, qw), full),
            pl.BlockSpec((1, MLA_KV_RANK), full),
            tab, tab, tab, tab],
        out_specs=[pl.BlockSpec((tm, 3 * HY_DIM), row),
                   pl.BlockSpec((tm, qw), row),
                   pl.BlockSpec((tm, 2 * LANES), row)],
        out_shape=[jax.ShapeDtypeStruct((tok.t, 3 * HY_DIM), F32),
                   jax.ShapeDtypeStruct((tok.t, qw), BF16),
                   jax.ShapeDtypeStruct((tok.t, 2 * LANES), F32)],
        compiler_params=_cparams(("arbitrary",), 48),
        name="even_in",
    )(*x_parts, mod, w_in_p, gq, wuq_p, gkv, cq, sq, ck, sk)


def _odd_in_kernel(x_ref, mod_ref, w_ref, gq_ref, gk_ref, seg_ref, cq_ref, sq_ref, ck_ref, sk_ref,
                   glu_ref, q_ref, kv_ref):
    n_q = GQA_HEADS * GQA_HD
    n_k = GQA_KV_HEADS * GQA_HD
    o = 2 * CV_DIM + n_q
    seg = seg_ref[...]
    for r in range(glu_ref.shape[0] // SUB_ROWS):
        rows = slice(r * SUB_ROWS, (r + 1) * SUB_ROWS)
        h = (x_ref[rows, :] * (1.0 + mod_ref[0, 1:2, :]) + mod_ref[0, 0:1, :]).astype(BF16)
        u = jnp.dot(h, w_ref[...], preferred_element_type=F32)
        glu_ref[rows, :] = (u[:, :CV_DIM] * jax.nn.sigmoid(u[:, CV_DIM:2 * CV_DIM])).astype(BF16)
        gq, cq, sq = gq_ref[...], cq_ref[rows, :], sq_ref[rows, :]
        for j in range(n_q // LANES):
            q = u[:, 2 * CV_DIM + j * LANES:2 * CV_DIM + (j + 1) * LANES]
            qn = q * lax.rsqrt(_seg_mean(q * q, seg) + RMS_EPS) * gq
            q_ref[rows, j * LANES:(j + 1) * LANES] = _rope(qn, cq, sq, GQA_HD // 4).astype(BF16)
        k = u[:, o:o + n_k]
        kn = k * lax.rsqrt(_seg_mean(k * k, seg) + RMS_EPS) * gk_ref[...]
        kv_ref[rows, :n_k] = _rope(kn, ck_ref[rows, :], sk_ref[rows, :], GQA_HD // 4)
        kv_ref[rows, n_k:] = u[:, o + n_k:]


def _odd_in(tok, x, mod, w_in, gq, gk, tabs, tm):
    cq, sq, ck, sk = tabs
    d, n = w_in.shape
    n_q = GQA_HEADS * GQA_HD
    n_k = GQA_KV_HEADS * GQA_HD
    cm, rm = tok.cond_map(tm), tok.rope_map(tm)
    i = jnp.arange(LANES)
    seg = jnp.where((i[:, None] // GQA_HD) == (i[None, :] // GQA_HD), 1.0 / GQA_HD, 0.0).astype(BF16)
    row = lambda m: (m, 0)
    full = lambda m: (0, 0)
    tab = pl.BlockSpec((tm, LANES), lambda m: (rm(m), 0))
    return pl.pallas_call(
        _odd_in_kernel,
        grid=(tok.t // tm,),
        in_specs=[pl.BlockSpec((tm, d), row),
                  pl.BlockSpec((1, 6, d), lambda m: (cm(m), 0, 0)),
                  pl.BlockSpec((d, n), full),
                  pl.BlockSpec((1, LANES), full),
                  pl.BlockSpec((1, LANES), full),
                  pl.BlockSpec((LANES, LANES), full),
                  tab, tab, tab, tab],
        out_specs=[pl.BlockSpec((tm, CV_DIM), row),
                   pl.BlockSpec((tm, n_q), row),
                   pl.BlockSpec((tm, 2 * n_k), row)],
        out_shape=[jax.ShapeDtypeStruct((tok.t, CV_DIM), BF16),
                   jax.ShapeDtypeStruct((tok.t, n_q), BF16),
                   jax.ShapeDtypeStruct((tok.t, 2 * n_k), F32)],
        compiler_params=_cparams(("arbitrary",), 48),
        name="odd_in",
    )(x, mod, w_in, gq, gk, seg, cq, sq, ck, sk)


def _softmax_pv(q, k, v):
    s = lax.dot_general(q, k, (((1,), (1,)), ((), ())), preferred_element_type=F32)
    p = jnp.exp(s - jnp.max(s, axis=-1, keepdims=True))
    l = jnp.sum(p, axis=-1, keepdims=True)
    return jnp.dot(p.astype(BF16), v, preferred_element_type=F32) * (1.0 / l)


def _softmax_pv_t(q, k, vt):
    st = lax.dot_general(k, q, (((1,), (1,)), ((), ())), preferred_element_type=F32)
    pt = jnp.exp(st - jnp.max(st, axis=0, keepdims=True))
    l = jnp.sum(pt, axis=0, keepdims=True)
    return jnp.dot(vt, pt.astype(BF16), preferred_element_type=F32) * (1.0 / l)


def _mla_attn_kernel(*refs, seg_lens, has_cache, pps, keys_on_rows):
    if has_cache:
        q_ref, cckv_ref, ckr_ref, own_ref, wk_ref, wv_ref, o_ref, ck_scr, k_scr, v_scr = refs
    else:
        q_ref, own_ref, wk_ref, wv_ref, o_ref, ck_scr, k_scr, v_scr = refs
    npair = MLA_HEADS // 2

    @pl.when((pl.program_id(1) == 0) & (pl.program_id(2) == 0))
    def _():
        off = 0
        if has_cache:
            n = seg_lens[0]
            ck_scr[0:n, :MLA_KV_RANK] = cckv_ref[0, 0].astype(BF16)
            ck_scr[0:n, MLA_KV_RANK:] = jnp.zeros((n, LANES), BF16)
            ck_scr[0:n, MLA_KV_RANK:MLA_KV_RANK + MLA_ROPE] = ckr_ref[0, 0].astype(BF16)
            off = n
        ck_scr[off:off + seg_lens[-1], :] = own_ref[...].astype(BF16)
        ck = ck_scr[...]
        for hp in range(npair):
            k_scr[hp] = jnp.dot(ck, wk_ref[hp], preferred_element_type=F32).astype(BF16)
        if keys_on_rows:
            vt = lax.dot_general(wv_ref[...], ck[:, :MLA_KV_RANK], (((1,), (1,)), ((), ())),
                                 preferred_element_type=F32).astype(BF16)
            for h in range(MLA_HEADS):
                v_scr[h] = vt[h * MLA_V:(h + 1) * MLA_V, :]
        else:
            for h in range(MLA_HEADS):
                v_scr[h] = jnp.dot(ck[:, :MLA_KV_RANK], wv_ref[h], preferred_element_type=F32).astype(BF16)

    for lp in range(pps):
        gp = lp if pps == npair else pl.program_id(2) * pps + lp
        kp = k_scr[gp]
        q0 = q_ref[:, 2 * lp * LANES:(2 * lp + 1) * LANES]
        q1 = q_ref[:, (2 * lp + 1) * LANES:(2 * lp + 2) * LANES]
        if keys_on_rows:
            o = jnp.concatenate([_softmax_pv_t(q0, kp[:, :LANES], v_scr[2 * gp]),
                                 _softmax_pv_t(q1, kp[:, LANES:], v_scr[2 * gp + 1])], axis=0).T
        else:
            o = _softmax_pv(q0, kp[:, :LANES], v_scr[2 * gp]) + _softmax_pv(q1, kp[:, LANES:], v_scr[2 * gp + 1])
        o_ref[:, lp * LANES:(lp + 1) * LANES] = o.astype(o_ref.dtype)


def _mla_attn(q, ckr, wk, wv, tok_off, nb, lq, tq, pps, cache=None):
    keys_on_rows = wv.ndim == 2
    nq = lq // tq
    qb0 = tok_off // tq
    ob0 = tok_off // lq
    npair = MLA_HEADS // 2
    seg_lens = (lq,) if cache is None else (cache[0].shape[2], lq)
    lk = sum(seg_lens)
    in_specs = [pl.BlockSpec((tq, 2 * LANES * pps), lambda b, i, p: (qb0 + b * nq + i, p))]
    args = [q]
    if cache is not None:
        cckv, ckr_c, layer = cache
        past = cckv.shape[2]
        in_specs += [pl.BlockSpec((1, 1, past, MLA_KV_RANK), lambda b, i, p: (b, layer, 0, 0)),
                     pl.BlockSpec((1, 1, past, MLA_ROPE), lambda b, i, p: (b, layer, 0, 0))]
        args += [cckv, ckr_c]
    in_specs += [pl.BlockSpec((lq, 2 * LANES), lambda b, i, p: (ob0 + b, 0)),
                 pl.BlockSpec((npair, 2 * LANES, 2 * LANES), lambda b, i, p: (0, 0, 0)),
                 pl.BlockSpec(wv.shape, lambda b, i, p: (0,) * wv.ndim)]
    args += [ckr, wk, wv]
    return pl.pallas_call(
        functools.partial(_mla_attn_kernel, seg_lens=seg_lens, has_cache=cache is not None, pps=pps,
                          keys_on_rows=keys_on_rows),
        grid=(nb, nq, npair // pps),
        in_specs=in_specs,
        out_specs=pl.BlockSpec((tq, LANES * pps), lambda b, i, p: (b * nq + i, p)),
        out_shape=jax.ShapeDtypeStruct((nb * lq, MLA_HEADS * MLA_V), BF16),
        scratch_shapes=[pltpu.VMEM((lk, 2 * LANES), BF16),
                        pltpu.VMEM((npair, lk, 2 * LANES), BF16),
                        pltpu.VMEM((MLA_HEADS, MLA_V, lk) if keys_on_rows else (MLA_HEADS, lk, LANES), BF16)],
        compiler_params=_cparams(("arbitrary", "arbitrary", "arbitrary"), 48),
        name="mla_attn_lat" if cache is not None else "mla_attn_ctx",
    )(*args)


def _gqa_attn_kernel(*refs, seg_lens, has_cache, pps, keys_on_rows):
    if has_cache:
        q_ref, ck_ref, cv_ref, own_ref, sel_ref, eye_ref, o_ref, kv_scr, k_scr, v_scr = refs
    else:
        q_ref, own_ref, sel_ref, eye_ref, o_ref, kv_scr, k_scr, v_scr = refs
    n_k = GQA_KV_HEADS * GQA_HD
    npair = GQA_HEADS // 2
    group_pairs = GQA_HEADS // GQA_KV_HEADS // 2

    @pl.when((pl.program_id(1) == 0) & (pl.program_id(2) == 0))
    def _():
        off = 0
        if has_cache:
            n = seg_lens[0]
            kv_scr[0:n, :n_k] = ck_ref[0, 0].astype(BF16)
            kv_scr[0:n, n_k:] = cv_ref[0, 0].astype(BF16)
            off = n
        kv_scr[off:off + seg_lens[-1], :] = own_ref[...].astype(BF16)
        kk = kv_scr[:, :n_k]
        vv = kv_scr[:, n_k:]
        for s in range(2 * GQA_KV_HEADS):
            k_scr[s] = jnp.dot(kk, sel_ref[s], preferred_element_type=F32).astype(BF16)
        if keys_on_rows:
            vt = lax.dot_general(eye_ref[...], vv, (((1,), (1,)), ((), ())), preferred_element_type=F32).astype(BF16)
            for kvh in range(GQA_KV_HEADS):
                v_scr[kvh] = vt[kvh * GQA_HD:(kvh + 1) * GQA_HD, :]
        else:
            for s in range(2 * GQA_KV_HEADS):
                v_scr[s] = jnp.dot(vv, sel_ref[s], preferred_element_type=F32).astype(BF16)

    for lp in range(pps):
        gp = lp if pps == npair else pl.program_id(2) * pps + lp
        kvh = gp // group_pairs
        q = q_ref[:, lp * LANES:(lp + 1) * LANES]
        if keys_on_rows:
            vt = v_scr[kvh]
            o = jnp.concatenate([_softmax_pv_t(q, k_scr[2 * kvh], vt),
                                 _softmax_pv_t(q, k_scr[2 * kvh + 1], vt)], axis=0).T
        else:
            o = (_softmax_pv(q, k_scr[2 * kvh], v_scr[2 * kvh])
                 + _softmax_pv(q, k_scr[2 * kvh + 1], v_scr[2 * kvh + 1]))
        o_ref[:, lp * LANES:(lp + 1) * LANES] = o.astype(o_ref.dtype)


def _gqa_attn(q, kv, tok_off, nb, lq, tq, pps, cache=None):
    keys_on_rows = cache is not None
    nq = lq // tq
    qb0 = tok_off // tq
    ob0 = tok_off // lq
    npair = GQA_HEADS // 2
    n_k = GQA_KV_HEADS * GQA_HD
    seg_lens = (lq,) if cache is None else (cache[0].shape[2], lq)
    lk = sum(seg_lens)
    src = jnp.arange(n_k)[:, None]
    dst = jnp.arange(n_k)[None, :]
    sel = jnp.stack([jnp.where((src // GQA_HD == kvh) & (dst // GQA_HD == i) & (src % GQA_HD == dst % GQA_HD), 1.0, 0.0)
                     for kvh in range(GQA_KV_HEADS) for i in range(2)]).astype(BF16)
    in_specs = [pl.BlockSpec((tq, LANES * pps), lambda b, i, p: (qb0 + b * nq + i, p))]
    args = [q]
    if cache is not None:
        ck, cv, layer = cache
        past = ck.shape[2]
        in_specs += [pl.BlockSpec((1, 1, past, n_k), lambda b, i, p: (b, layer, 0, 0)),
                     pl.BlockSpec((1, 1, past, n_k), lambda b, i, p: (b, layer, 0, 0))]
        args += [ck, cv]
    in_specs += [pl.BlockSpec((lq, 2 * n_k), lambda b, i, p: (ob0 + b, 0)),
                 pl.BlockSpec((2 * GQA_KV_HEADS, n_k, n_k), lambda b, i, p: (0, 0, 0)),
                 pl.BlockSpec((n_k, n_k), lambda b, i, p: (0, 0))]
    args += [kv, sel, jnp.eye(n_k, dtype=BF16)]
    return pl.pallas_call(
        functools.partial(_gqa_attn_kernel, seg_lens=seg_lens, has_cache=cache is not None, pps=pps,
                          keys_on_rows=keys_on_rows),
        grid=(nb, nq, npair // pps),
        in_specs=in_specs,
        out_specs=pl.BlockSpec((tq, LANES * pps), lambda b, i, p: (b * nq + i, p)),
        out_shape=jax.ShapeDtypeStruct((nb * lq, GQA_HEADS * GQA_HD), BF16),
        scratch_shapes=[pltpu.VMEM((lk, 2 * n_k), BF16),
                        pltpu.VMEM((2 * GQA_KV_HEADS, lk, n_k), BF16),
                        pltpu.VMEM((GQA_KV_HEADS, GQA_HD, lk) if keys_on_rows else (2 * GQA_KV_HEADS, lk, n_k), BF16)],
        compiler_params=_cparams(("arbitrary", "arbitrary", "arbitrary"), 48),
        name="gqa_attn_lat" if cache is not None else "gqa_attn_ctx",
    )(*args)


def _dft_mats(lb):
    n = 2 * lb
    r = jnp.arange(n, dtype=jnp.int32)
    nyq = r == lb
    f = jnp.where(nyq, lb, r % lb)
    is_im = (r >= lb) & ~nyq
    s = jnp.arange(lb, dtype=jnp.int32)

    def mat(pos):
        ang = ((f[:, None] * pos[None, :]) % n).astype(F32) * (2.0 * math.pi / n)
        return jnp.where(is_im[:, None], -jnp.sin(ang), jnp.cos(ang))

    fwd = mat(s)
    fwd_rev = jnp.where(s[None, :] == 0, 0.0, mat(lb - s))
    wgt = jnp.where((r == 0) | nyq, 1.0 / n, 2.0 / n)
    inv = (fwd * wgt[:, None]).T
    return fwd.astype(BF16), inv.astype(BF16), fwd_rev.astype(BF16)


def _spec_kernel(f_ref, fr_ref, ka_ref, kb_ref, p_ref, q_ref, p2_ref, *, lb):
    tf = jnp.dot(f_ref[...], ka_ref[...], preferred_element_type=F32)
    tb = jnp.dot(fr_ref[...], kb_ref[...], preferred_element_type=F32)
    re = tf[:lb] + tb[:lb]
    im = tf[lb:] - tb[lb:]
    nyq = tf[lb:] + tb[lb:]
    row0 = lax.broadcasted_iota(jnp.int32, re.shape, 0) == 0
    p_ref[0] = re
    q_ref[0] = jnp.where(row0, 0.0, im)
    p2_ref[0] = jnp.where(row0, nyq, re)


def _filter_spectrum(mats, k2, dmax):
    fwd_mat, _, fwd_rev = mats
    n, lb = fwd_mat.shape
    cw = k2.shape[1]
    nbk = k2.shape[0] // n
    nd = 2 * dmax + 1
    tc = 512
    out = jax.ShapeDtypeStruct((nd, lb, cw), F32)
    blk = pl.BlockSpec((1, lb, tc), lambda c, d: (d, 0, c))
    mat = pl.BlockSpec((n, lb), lambda c, d: (0, 0))
    return pl.pallas_call(
        functools.partial(_spec_kernel, lb=lb),
        grid=(cw // tc, nd),
        in_specs=[mat, mat,
                  pl.BlockSpec((lb, tc), lambda c, d: (nbk + d - dmax, c)),
                  pl.BlockSpec((lb, tc), lambda c, d: (nbk + d - dmax - 1, c))],
        out_specs=[blk, blk, blk],
        out_shape=[out, out, out],
        compiler_params=_cparams(("arbitrary", "arbitrary"), 32),
        name="filter_spectrum",
    )(fwd_mat, fwd_rev, k2, k2)


def _block_conv(zbf_ref, z_scr, f_ref, fi_ref, p_ref, q_ref, p2_ref, lb, nblk, nbk, dmax, emit):
    for r in range(nblk):
        z_scr[r] = jnp.dot(f_ref[...], zbf_ref[r * lb:(r + 1) * lb, :], preferred_element_type=F32)
    for r in range(nblk):
        s, i = divmod(r, nbk)
        ya = yb = None
        for j in range(nbk):
            d = i - j
            if abs(d) > dmax:
                continue
            re = z_scr[s * nbk + j, :lb, :]
            im = z_scr[s * nbk + j, lb:, :]
            p, q, p2 = p_ref[d + dmax], q_ref[d + dmax], p2_ref[d + dmax]
            ta = re * p - im * q
            tb = re * q + im * p2
            ya = ta if ya is None else ya + ta
            yb = tb if yb is None else yb + tb
        y = jnp.concatenate([ya, yb], axis=0).astype(BF16)
        emit(r, jnp.dot(fi_ref[...], y, preferred_element_type=F32))


def _short_conv(u, w_ref, b_ref, l):
    rows = u.shape[0]
    t = jnp.bitwise_and(lax.broadcasted_iota(jnp.int32, u.shape, 0), l - 1)
    prev = jnp.where(t == 0, 0.0, pltpu.roll(u, 1, 0))
    nxt = jnp.where(t == l - 1, 0.0, pltpu.roll(u, rows - 1, 0))
    return w_ref[0:1, :] * prev + w_ref[1:2, :] * u + w_ref[2:3, :] * nxt + b_ref[...]


def _hyena_kernel(*refs, l, lb, nseq, dmax, conv_z):
    if conv_z:
        (uz_ref, ug_ref, cwz_ref, cbz_ref, cwg_ref, cbg_ref, skip_ref,
         f_ref, fi_ref, p_ref, q_ref, p2_ref, o_ref, zf_scr, g_scr, zbf_scr, z_scr) = refs
    else:
        (uz_ref, ug_ref, cwg_ref, cbg_ref, skip_ref,
         f_ref, fi_ref, p_ref, q_ref, p2_ref, o_ref, zf_scr, g_scr, zbf_scr, z_scr) = refs
    z = _short_conv(uz_ref[...], cwz_ref, cbz_ref, l) if conv_z else uz_ref[...]
    zf_scr[...] = z
    zbf_scr[...] = z.astype(BF16)
    g_scr[...] = _short_conv(ug_ref[...], cwg_ref, cbg_ref, l)
    skip = skip_ref[...]

    def emit(r, y):
        rows = slice(r * lb, (r + 1) * lb)
        o_ref[rows, :] = (g_scr[rows, :] * (y + skip * zf_scr[rows, :])).astype(o_ref.dtype)

    nbk = l // lb
    _block_conv(zbf_scr, z_scr, f_ref, fi_ref, p_ref, q_ref, p2_ref, lb, nseq * nbk, nbk, dmax, emit)


def _hyena_stage(z_src, z_col, u_hy, g_col, conv_w, conv_b, skip, mats, spec, spec_col,
                 tok_off, nb, l, nseq, tc, conv_z, out_dtype):
    fwd_mat, inv_mat, _ = mats
    p_arr, q_arr, p2_arr = spec
    n, lb = fwd_mat.shape
    nd = p_arr.shape[0]
    dmax = nd // 2
    rows = nseq * l
    rb0 = tok_off // rows
    zrb0 = rb0 if conv_z else 0
    cpb = HY_DIM // tc
    cvec = lambda g: pl.BlockSpec((1, tc), lambda c, b: (0, g * cpb + c))
    in_specs = [pl.BlockSpec((rows, tc), lambda c, b: (zrb0 + b, z_col * cpb + c)),
                pl.BlockSpec((rows, tc), lambda c, b: (rb0 + b, g_col * cpb + c))]
    args = [z_src, u_hy]
    if conv_z:
        in_specs += [pl.BlockSpec((3, tc), lambda c, b: (0, z_col * cpb + c)), cvec(z_col)]
        args += [conv_w, conv_b]
    in_specs += [pl.BlockSpec((3, tc), lambda c, b: (0, g_col * cpb + c)), cvec(g_col),
                 pl.BlockSpec((1, tc), lambda c, b: (0, c)),
                 pl.BlockSpec((n, lb), lambda c, b: (0, 0)),
                 pl.BlockSpec((lb, n), lambda c, b: (0, 0))]
    args += [conv_w, conv_b, skip, fwd_mat, inv_mat]
    sspec = pl.BlockSpec((nd, lb, tc), lambda c, b: (0, 0, spec_col * cpb + c), pipeline_mode=pl.Buffered(1))
    in_specs += [sspec, sspec, sspec]
    args += [p_arr, q_arr, p2_arr]
    nblk = rows // lb
    return pl.pallas_call(
        functools.partial(_hyena_kernel, l=l, lb=lb, nseq=nseq, dmax=dmax, conv_z=conv_z),
        grid=(HY_DIM // tc, nb // nseq),
        in_specs=in_specs,
        out_specs=pl.BlockSpec((rows, tc), lambda c, b: (b, c)),
        out_shape=jax.ShapeDtypeStruct((nb * l, HY_DIM), out_dtype),
        scratch_shapes=[pltpu.VMEM((rows, tc), F32), pltpu.VMEM((rows, tc), F32),
                        pltpu.VMEM((rows, tc), BF16), pltpu.VMEM((nblk, n, tc), F32)],
        compiler_params=_cparams(("arbitrary", "arbitrary"), 56),
        name="hyena_stage",
    )(*args)


def _conformer_kernel(glu_ref, b_ref, lng_ref, lnb_ref, f_ref, fi_ref, p_ref, q_ref, p2_ref, o_ref, z_scr,
                      *, l, lb, nseq, dmax):
    bias, lng, lnb = b_ref[...], lng_ref[...], lnb_ref[...]

    def emit(r, y):
        yn = _layer_norm(y + bias, lng, lnb)
        o_ref[r * lb:(r + 1) * lb, :] = (yn * jax.nn.sigmoid(yn)).astype(o_ref.dtype)

    nbk = l // lb
    _block_conv(glu_ref, z_scr, f_ref, fi_ref, p_ref, q_ref, p2_ref, lb, nseq * nbk, nbk, dmax, emit)


def _conformer(glu, bias, ln_g, ln_b, mats, spec, tok_off, nb, l, nseq):
    fwd_mat, inv_mat, _ = mats
    n, lb = fwd_mat.shape
    nd = spec[0].shape[0]
    rows = nseq * l
    rb0 = tok_off // rows
    c = CV_DIM
    vec = pl.BlockSpec((1, c), lambda b: (0, 0))
    sspec = pl.BlockSpec((nd, lb, c), lambda b: (0, 0, 0))
    return pl.pallas_call(
        functools.partial(_conformer_kernel, l=l, lb=lb, nseq=nseq, dmax=nd // 2),
        grid=(nb // nseq,),
        in_specs=[pl.BlockSpec((rows, c), lambda b: (rb0 + b, 0)),
                  vec, vec, vec,
                  pl.BlockSpec((n, lb), lambda b: (0, 0)),
                  pl.BlockSpec((lb, n), lambda b: (0, 0)),
                  sspec, sspec, sspec],
        out_specs=pl.BlockSpec((rows, c), lambda b: (b, 0)),
        out_shape=jax.ShapeDtypeStruct((nb * l, c), BF16),
        scratch_shapes=[pltpu.VMEM((rows // lb, n, c), F32)],
        compiler_params=_cparams(("arbitrary",), 56),
        name="conformer",
    )(glu, bias, ln_g, ln_b, fwd_mat, inv_mat, *spec)


def _filter_kernel(z_ref, w1_ref, b1_ref, w2_ref, b2_ref, w3_ref, freq_ref, decay_ref, o_ref):
    hp = lax.Precision.HIGHEST
    z = z_ref[...]
    freq = freq_ref[...]
    hid = jnp.sin(freq * (jnp.dot(z, w1_ref[...], precision=hp, preferred_element_type=F32) + b1_ref[...]))
    hid = jnp.sin(freq * (jnp.dot(hid, w2_ref[...], precision=hp, preferred_element_type=F32) + b2_ref[...]))
    h = jnp.dot(hid, w3_ref[...], precision=hp, preferred_element_type=F32)
    h = h * jnp.exp(-z[:, 0:1] * jnp.abs(decay_ref[...]))
    row = lax.broadcasted_iota(jnp.int32, h.shape, 0) + pl.program_id(0) * h.shape[0]
    o_ref[...] = jnp.where(row == 0, 0.0, h).astype(o_ref.dtype)


def _hyena_filter(l, w1, b1, w2, b2, w3, freq, decay):
    j = jnp.arange(2 * l, dtype=jnp.int32)
    t = jnp.abs(j - l).astype(F32) / l
    bands = jnp.arange(1, HY_BANDS + 1, dtype=F32)
    ang = 2.0 * math.pi * t[:, None] * bands[None, :]
    z = jnp.concatenate([t[:, None], jnp.cos(ang), jnp.sin(ang), jnp.zeros((2 * l, LANES - HY_EMB), F32)], axis=-1)
    w1p = jnp.concatenate([w1, jnp.zeros((LANES - HY_EMB, HY_FILT_HID), F32)], axis=0)
    nout = w3.shape[1] // 2
    tl = 256
    n_bwd = l // tl
    full = lambda i: (0, 0)
    half = lambda i: (0, jnp.where(i < n_bwd, 1, 0))
    return pl.pallas_call(
        _filter_kernel,
        grid=(2 * l // tl,),
        in_specs=[pl.BlockSpec((tl, LANES), lambda i: (i, 0)),
                  pl.BlockSpec((LANES, HY_FILT_HID), full),
                  pl.BlockSpec((1, HY_FILT_HID), full),
                  pl.BlockSpec((HY_FILT_HID, HY_FILT_HID), full),
                  pl.BlockSpec((1, HY_FILT_HID), full),
                  pl.BlockSpec((HY_FILT_HID, nout), half),
                  pl.BlockSpec((1, HY_FILT_HID), full),
                  pl.BlockSpec((1, nout), half)],
        out_specs=pl.BlockSpec((tl, nout), lambda i: (i, 0)),
        out_shape=jax.ShapeDtypeStruct((2 * l, nout), BF16),
        compiler_params=_cparams(("arbitrary",), 32),
        name="hyena_filter",
    )(z, w1p, b1[None], w2, b2[None], w3, freq[None], decay[None])


def _out_proj_kernel(*refs, n_x, n_ctx_tiles):
    ya_refs, yb_refs, x_refs = refs[0:2], refs[2:4], refs[4:4 + n_x]
    mod_ref, wa_ref, wb_ref, g_ref, b_ref, o_ref = refs[4 + n_x:]
    for r in range(o_ref.shape[0] // SUB_ROWS):
        rows = slice(r * SUB_ROWS, (r + 1) * SUB_ROWS)
        y = (jnp.dot(_read_parts(ya_refs, n_ctx_tiles, rows), wa_ref[...], preferred_element_type=F32)
             + jnp.dot(_read_parts(yb_refs, n_ctx_tiles, rows), wb_ref[...], preferred_element_type=F32))
        z = ALPHA * _read_parts(x_refs, n_ctx_tiles, rows) + mod_ref[0, 2:3, :] * y
        o_ref[rows, :] = _layer_norm(z, g_ref[...], b_ref[...])


def _out_proj(tok, ya_parts, yb_parts, x_parts, mod, w_out, ln_g, ln_b, tm):
    d = x_parts[0].shape[1]
    ka, kb = ya_parts[0].shape[1], yb_parts[0].shape[1]
    cm = tok.cond_map(tm)
    full = lambda m: (0, 0)
    return pl.pallas_call(
        functools.partial(_out_proj_kernel, n_x=len(x_parts), n_ctx_tiles=tok.t_ctx // tm),
        grid=(tok.t // tm,),
        in_specs=(tok.part_specs(tm, 2, ka) + tok.part_specs(tm, 2, kb) + tok.part_specs(tm, len(x_parts), d) + [
            pl.BlockSpec((1, 6, d), lambda m: (cm(m), 0, 0)),
            pl.BlockSpec((ka, d), lambda m: (0, 0)),
            pl.BlockSpec((kb, d), lambda m: (ka // kb, 0)),
            pl.BlockSpec((1, d), full),
            pl.BlockSpec((1, d), full)]),
        out_specs=pl.BlockSpec((tm, d), lambda m: (m, 0)),
        out_shape=jax.ShapeDtypeStruct((tok.t, d), F32),
        compiler_params=_cparams(("arbitrary",), 40),
        name="out_proj_ln",
    )(*ya_parts, *yb_parts, *x_parts, mod, w_out, w_out, ln_g, ln_b)


MLP_CHUNK = 512


def _mlp_kernel(x_ref, mod_ref, w1_ref, b1_ref, w2_ref, b2_ref, g_ref, b_ref, o_ref, h_scr, acc_scr, *, nf):
    f = pl.program_id(1)

    @pl.when(f == 0)
    def _():
        h_scr[...] = (x_ref[...] * (1.0 + mod_ref[0, 4:5, :]) + mod_ref[0, 3:4, :]).astype(BF16)
        acc_scr[...] = jnp.zeros_like(acc_scr)

    part = None
    for c0 in range(0, w1_ref.shape[1], MLP_CHUNK):
        cols = slice(c0, c0 + MLP_CHUNK)
        a = jnp.maximum(jnp.dot(h_scr[...], w1_ref[:, cols], preferred_element_type=F32) + b1_ref[:, cols], 0.0)
        y = jnp.dot((a * a).astype(BF16), w2_ref[cols, :], preferred_element_type=F32)
        part = y if part is None else part + y
    acc_scr[...] += part

    @pl.when(f == nf - 1)
    def _():
        z = ALPHA * x_ref[...] + mod_ref[0, 5:6, :] * (acc_scr[...] + b2_ref[...])
        o_ref[...] = _layer_norm(z, g_ref[...], b_ref[...])


def _mlp(tok, x, mod, layer, w1, b1, w2, b2, ln_g, ln_b, tm, tf, tok_off, n_rows):
    d = x.shape[1]
    dff = w1.shape[2]
    nf = dff // tf
    m0 = tok_off // tm
    cm = tok.cond_map(tm)
    return pl.pallas_call(
        functools.partial(_mlp_kernel, nf=nf),
        grid=(n_rows // tm, nf),
        in_specs=[pl.BlockSpec((tm, d), lambda m, f: (m0 + m, 0)),
                  pl.BlockSpec((1, 6, d), lambda m, f: (cm(m0 + m), 0, 0)),
                  pl.BlockSpec((None, d, tf), lambda m, f: (layer, 0, f)),
                  pl.BlockSpec((None, 1, tf), lambda m, f: (layer, 0, f)),
                  pl.BlockSpec((None, tf, d), lambda m, f: (layer, f, 0)),
                  pl.BlockSpec((None, 1, d), lambda m, f: (layer, 0, 0)),
                  pl.BlockSpec((1, d), lambda m, f: (0, 0)),
                  pl.BlockSpec((1, d), lambda m, f: (0, 0))],
        out_specs=pl.BlockSpec((tm, d), lambda m, f: (m, 0)),
        out_shape=jax.ShapeDtypeStruct((n_rows, d), F32),
        scratch_shapes=[pltpu.VMEM((tm, d), BF16), pltpu.VMEM((tm, d), F32)],
        compiler_params=_cparams(("arbitrary", "arbitrary"), 56),
        name="mlp_ln",
    )(x, mod, w1, b1, w2, b2, ln_g, ln_b)


def _mla_weights(w_uq, w_ukv):
    hd = MLA_NOPE + MLA_ROPE
    wq = w_uq.reshape(MLA_Q_RANK, MLA_HEADS, hd)
    wq = jnp.pad(wq, ((0, 0), (0, 0), (0, LANES - hd))).reshape(MLA_Q_RANK, MLA_HEADS * LANES)
    wkv = w_ukv.reshape(MLA_KV_RANK, MLA_HEADS, MLA_NOPE + MLA_V).transpose(1, 0, 2)
    w_nope, w_v = wkv[..., :MLA_NOPE], wkv[..., MLA_NOPE:]
    top = jnp.pad(w_nope, ((0, 0), (0, 0), (0, LANES - MLA_NOPE)))
    place = jnp.pad(jnp.eye(MLA_ROPE, dtype=F32), ((0, LANES - MLA_ROPE), (MLA_NOPE, LANES - MLA_NOPE - MLA_ROPE)))
    wk = jnp.concatenate([top, jnp.broadcast_to(place, (MLA_HEADS, LANES, LANES))], axis=1)
    wk = wk.reshape(MLA_HEADS // 2, 2, 2 * LANES, LANES).transpose(0, 2, 1, 3).reshape(MLA_HEADS // 2, 2 * LANES, 2 * LANES)
    wv_t = w_v.transpose(0, 2, 1).reshape(MLA_HEADS * MLA_V, MLA_KV_RANK)
    w_v = w_v.reshape(MLA_HEADS // 2, 2, MLA_KV_RANK, MLA_V)
    wv = jnp.stack([jnp.pad(w_v[:, 0], ((0, 0), (0, 0), (0, LANES - MLA_V))),
                    jnp.pad(w_v[:, 1], ((0, 0), (0, 0), (LANES - MLA_V, 0)))], axis=1)
    wv = wv.reshape(MLA_HEADS, MLA_KV_RANK, LANES)
    return wq.astype(BF16), wk.astype(BF16), wv.astype(BF16), wv_t.astype(BF16)


def _conformer_taps(dw_w, l):
    k, c = dw_w.shape
    half = k // 2
    return jnp.pad(dw_w[::-1], ((l - half, l - (k - half)), (0, 0))).astype(BF16)


def kernel(x_prompt, x_sample, cache_mla_ckv, cache_mla_krope, cache_gqa_k, cache_gqa_v, c, c_ctx, ev_w_in, hy_conv_w, hy_conv_b, hy_filt_w1, hy_filt_b1, hy_filt_w2, hy_filt_b2, hy_filt_w3, hy_sin_freq, hy_decay, hy_skip, mla_q_norm_g, mla_w_uq, mla_kv_norm_g, mla_w_ukv, ev_w_out, od_w_in, cv_dw_w, cv_dw_b, cv_ln_g, cv_ln_b, gqa_q_norm_g, gqa_k_norm_g, od_w_out, ada_w, ada_b, ln_g, ln_b, mlp_w1, mlp_b1, mlp_w2, mlp_b2):
    bc, lc, d = x_prompt.shape
    bl, ll, _ = x_sample.shape
    tok = _Tok(bc, lc, bl, ll)
    assert lc & (lc - 1) == 0 and ll & (ll - 1) == 0 and tok.t_ctx % ll == 0 and bl < 8
    n_od = od_w_in.shape[0]
    tm_in = min(512, ll)
    tm_mlp = min(1024, ll)
    tf_mlp = 1024
    tq = 256
    tq_lat = min(256, ll)
    seq_ctx = max(1, 1024 // lc)
    tc_lat = 256 if ll > 1024 else HY_DIM
    cv_half = cv_dw_w.shape[1] // 2

    cond8 = jnp.concatenate([c_ctx[None], c, jnp.zeros((7 - bl, d), F32)], axis=0)
    mods = _ada(cond8, ada_w, ada_b).reshape(DEPTH, 8, 6, d)

    mla_scale = (MLA_NOPE + MLA_ROPE) ** -0.5
    ev_tabs = (_rope_tables(ll, tm_in, 1, LANES, MLA_NOPE, MLA_ROPE, mla_scale)
               + _rope_tables(ll, tm_in, 1, LANES, 0, MLA_ROPE, 1.0))
    od_tabs = (_rope_tables(ll, tm_in, LANES // GQA_HD, GQA_HD, 0, GQA_HD, GQA_HD ** -0.5)
               + _rope_tables(ll, tm_in, LANES // GQA_HD, GQA_HD, 0, GQA_HD, 1.0))
    passes = []
    for off, nb, l, nseq, tc in ((0, bc, lc, seq_ctx, HY_DIM), (tok.t_ctx, bl, ll, 1, tc_lat)):
        lb = min(l, CONV_BLOCK)
        passes.append((off, nb, l, nseq, tc, lb, _dft_mats(lb)))

    mlp_w1_bf = mlp_w1.astype(BF16)
    mlp_w2_bf = mlp_w2.astype(BF16)
    x_parts = (x_prompt.reshape(tok.t_ctx, d), x_sample.reshape(tok.t_lat, d))
    ckv_list, krope_list, k_list, v_list = [], [], [], []

    for layer in range(DEPTH):
        i = layer // 2
        mod = mods[layer]
        if layer % 2 == 0:
            w_in_p = jnp.pad(ev_w_in[i], ((0, 0), (0, EV_W - ev_w_in.shape[2]))).astype(BF16)
            wq, wk, wv, wv_t = _mla_weights(mla_w_uq[i], mla_w_ukv[i])
            u_hy, q, ckr = _even_in(tok, x_parts, mod, w_in_p, mla_q_norm_g[i][None], wq, mla_kv_norm_g[i][None],
                                    ev_tabs, tm_in)
            ckv_list.append(ckr[:tok.t_ctx, :MLA_KV_RANK].reshape(bc, lc, MLA_KV_RANK))
            krope_list.append(ckr[:tok.t_ctx, MLA_KV_RANK:MLA_KV_RANK + MLA_ROPE].reshape(bc, lc, MLA_ROPE))
            yb_parts = (_mla_attn(q, ckr, wk, wv, 0, bc, lc, min(tq, lc), MLA_HEADS // 2),
                        _mla_attn(q, ckr, wk, wv_t, tok.t_ctx, bl, ll, tq_lat, 1,
                                  cache=(cache_mla_ckv, cache_mla_krope, i)))

            skip = hy_skip[i]
            conv_b = hy_conv_b[i][None]
            ya_parts = []
            for (off, nb, l, nseq, tc, lb, mats) in passes:
                k2 = _hyena_filter(l, hy_filt_w1[i], hy_filt_b1[i], hy_filt_w2[i], hy_filt_b2[i],
                                   hy_filt_w3[i], hy_sin_freq[i], hy_decay[i])
                spec = _filter_spectrum(mats, k2, l // lb - 1)
                z1 = _hyena_stage(u_hy, 2, u_hy, 0, hy_conv_w[i], conv_b, skip[0:1], mats, spec, 0,
                                  off, nb, l, nseq, tc, True, F32)
                ya_parts.append(_hyena_stage(z1, 0, u_hy, 1, hy_conv_w[i], conv_b, skip[1:2], mats, spec, 1,
                                             off, nb, l, nseq, tc, False, BF16))
            w_out = ev_w_out[i].astype(BF16)
        else:
            (x,) = x_parts
            gq = jnp.tile(gqa_q_norm_g[i], LANES // GQA_HD)[None]
            gk = jnp.tile(gqa_k_norm_g[i], GQA_KV_HEADS)[None]
            glu, q, kv = _odd_in(tok, x, mod, od_w_in[i].astype(BF16), gq, gk, od_tabs, tm_in)
            n_k = GQA_KV_HEADS * GQA_HD
            k_list.append(kv[:tok.t_ctx, :n_k].reshape(bc, lc, GQA_KV_HEADS, GQA_HD))
            v_list.append(kv[:tok.t_ctx, n_k:].reshape(bc, lc, GQA_KV_HEADS, GQA_HD))
            past = cache_gqa_k.shape[2]
            cache = (cache_gqa_k.reshape(bl, n_od, past, n_k), cache_gqa_v.reshape(bl, n_od, past, n_k), i)
            yb_parts = (_gqa_attn(q, kv, 0, bc, lc, min(tq, lc), GQA_HEADS // 2),
                        _gqa_attn(q, kv, tok.t_ctx, bl, ll, tq_lat, 1, cache=cache))

            ya_parts = []
            for (off, nb, l, nseq, tc, lb, mats) in passes:
                dmax = min(l // lb - 1, -(-cv_half // lb))
                spec = _filter_spectrum(mats, _conformer_taps(cv_dw_w[i], l), dmax)
                ya_parts.append(_conformer(glu, cv_dw_b[i][None], cv_ln_g[i][None], cv_ln_b[i][None], mats, spec,
                                           off, nb, l, nseq))
            w_out = od_w_out[i].astype(BF16)

        x = _out_proj(tok, ya_parts, yb_parts, x_parts, mod, w_out, ln_g[layer, 0][None], ln_b[layer, 0][None], tm_in)
        mlp_args = (mod, layer, mlp_w1_bf, mlp_b1[:, None, :], mlp_w2_bf, mlp_b2[:, None, :], ln_g[layer, 1][None], ln_b[layer, 1][None], tm_mlp, tf_mlp)
        if layer < DEPTH - 1:
            x_parts = (_mlp(tok, x, *mlp_args, 0, tok.t),)
        else:
            y_prompt = _mlp(tok, x, *mlp_args, 0, tok.t_ctx).reshape(bc, lc, d)
            y_sample = _mlp(tok, x, *mlp_args, tok.t_ctx, tok.t_lat).reshape(bl, ll, d)

    return (y_prompt, y_sample, jnp.stack(ckv_list, axis=1), jnp.stack(krope_list, axis=1),
            jnp.stack(k_list, axis=1), jnp.stack(v_list, axis=1))
```

```python
import functools
import math

import jax
import jax.numpy as jnp
from jax import lax
from jax.experimental import pallas as pl
from jax.experimental.pallas import tpu as pltpu

F32 = jnp.float32
BF16 = jnp.bfloat16

DEPTH = 4
GRID_W = 64
ALPHA = (2.0 * DEPTH) ** 0.25
LN_EPS = 1e-5
RMS_EPS = 1e-6
ROPE_THETA = 10000.0

HY_DIM = 512
HY_ORDER = 2
HY_BANDS = 16
HY_EMB = 2 * HY_BANDS + 1
HY_FILT_HID = 64

MLA_HEADS = 8
MLA_NOPE = 64
MLA_ROPE = 32
MLA_V = 64
MLA_Q_RANK = 256
MLA_KV_RANK = 128

CV_DIM = 512

GQA_HEADS = 8
GQA_KV_HEADS = 2
GQA_HD = 64

LANES = 128
CONV_BLOCK = 512
MIB = 2 ** 20


def _cparams(sem, vmem_mib):
    return pltpu.CompilerParams(dimension_semantics=sem, vmem_limit_bytes=vmem_mib * MIB)


def _layer_norm(z, g, b):
    mu = jnp.mean(z, axis=-1, keepdims=True)
    zc = z - mu
    var = jnp.mean(zc * zc, axis=-1, keepdims=True)
    return zc * lax.rsqrt(var + LN_EPS) * g + b


def _rms(x, g):
    return x * lax.rsqrt(jnp.mean(x * x, axis=-1, keepdims=True) + RMS_EPS) * g


def _seg_mean(sq, s_mat):
    hi = sq.astype(BF16)
    lo = (sq - hi.astype(F32)).astype(BF16)
    return (jnp.dot(hi, s_mat, preferred_element_type=F32)
            + jnp.dot(lo, s_mat, preferred_element_type=F32))


def _rope(x, cos, sin_signed, half):
    w = x.shape[1]
    lane = lax.broadcasted_iota(jnp.int32, x.shape, 1)
    first = jnp.bitwise_and(lane, 2 * half - 1) < half
    rot = jnp.where(first, pltpu.roll(x, w - half, 1), pltpu.roll(x, half, 1))
    return x * cos + rot * sin_signed


def _ada_kernel(c_ref, w_ref, b_ref, o_ref):
    c = c_ref[...]
    s = (c * jax.nn.sigmoid(c)).astype(BF16)
    o_ref[0] = jnp.dot(s, w_ref[0].astype(BF16), preferred_element_type=F32) + b_ref[0]


def _ada(cond8, ada_w, ada_b):
    depth, d, n = ada_w.shape
    tn = 1536
    return pl.pallas_call(
        _ada_kernel,
        grid=(depth, n // tn),
        in_specs=[pl.BlockSpec((8, d), lambda l, j: (0, 0)),
                  pl.BlockSpec((1, d, tn), lambda l, j: (l, 0, j)),
                  pl.BlockSpec((1, 1, tn), lambda l, j: (l, 0, j))],
        out_specs=pl.BlockSpec((1, 8, tn), lambda l, j: (l, 0, j)),
        out_shape=jax.ShapeDtypeStruct((depth, 8, n), F32),
        compiler_params=_cparams(("arbitrary", "arbitrary"), 32),
        name="ada_mod",
    )(cond8, ada_w, ada_b.reshape(depth, 1, n))


class _Tok:
    def __init__(self, bc, lc, bl, ll):
        self.bc, self.lc, self.bl, self.ll = bc, lc, bl, ll
        self.t_ctx = bc * lc
        self.t_lat = bl * ll
        self.t = self.t_ctx + self.t_lat

    def cond_map(self, tm):
        n_ctx, per_b = self.t_ctx // tm, self.ll // tm
        return lambda m: jnp.where(m < n_ctx, 0, 1 + (m - n_ctx) // per_b)

    def rope_map(self, tm):
        n_ctx, per_b = self.t_ctx // tm, self.ll // tm
        return lambda m: jnp.where(m < n_ctx, 0, 1 + (m - n_ctx) % per_b)

    def part_specs(self, tm, n_parts, width):
        if n_parts == 1:
            return [pl.BlockSpec((tm, width), lambda m: (m, 0))]
        n_ctx = self.t_ctx // tm
        return [pl.BlockSpec((tm, width), lambda m: (jnp.minimum(m, n_ctx - 1), 0)),
                pl.BlockSpec((tm, width), lambda m: (jnp.maximum(m - n_ctx, 0), 0))]


def _read_parts(refs, n_ctx_tiles, rows=slice(None)):
    if len(refs) == 1:
        return refs[0][rows, :]
    return jnp.where(pl.program_id(0) < n_ctx_tiles, refs[0][rows, :], refs[1][rows, :])


SUB_ROWS = 256


def _rope_tables(ll, tm, n_heads, head_w, rope_off, rope_dim, scale):
    half = rope_dim // 2
    quarter = half // 2
    t = jnp.arange(ll, dtype=F32)
    row = jnp.floor(t / GRID_W)
    col = t - row * GRID_W
    inv = ROPE_THETA ** (-jnp.arange(0, half, 2, dtype=F32) / half)
    ang_r = row[:, None] * inv[None, :]
    ang_c = col[:, None] * inv[None, :]
    ang = jnp.concatenate([ang_r, ang_r, ang_c, ang_c], axis=-1)
    sign = jnp.concatenate([-jnp.ones((quarter,), F32), jnp.ones((quarter,), F32)] * 2)
    pad = ((0, 0), (rope_off, head_w - rope_off - rope_dim))
    cos_h = jnp.pad(jnp.cos(ang), pad, constant_values=1.0)
    sin_h = jnp.pad(jnp.sin(ang) * sign, pad)
    cos_l = jnp.tile(cos_h, (1, n_heads))
    sin_l = jnp.tile(sin_h, (1, n_heads))
    w = n_heads * head_w
    cos = jnp.concatenate([jnp.ones((tm, w), F32), cos_l], axis=0) * scale
    sin = jnp.concatenate([jnp.zeros((tm, w), F32), sin_l], axis=0) * scale
    return cos, sin


EV_W = 2048


def _even_in_kernel(*refs, n_x, n_ctx_tiles):
    x_refs = refs[:n_x]
    (mod_ref, w_ref, gq_ref, wuq_ref, gkv_ref, cq_ref, sq_ref, ck_ref, sk_ref,
     uhy_ref, q_ref, ckr_ref) = refs[n_x:]
    n_hy = 3 * HY_DIM
    o = n_hy + MLA_Q_RANK
    for r in range(uhy_ref.shape[0] // SUB_ROWS):
        rows = slice(r * SUB_ROWS, (r + 1) * SUB_ROWS)
        x = _read_parts(x_refs, n_ctx_tiles, rows)
        h = (x * (1.0 + mod_ref[0, 1:2, :]) + mod_ref[0, 0:1, :]).astype(BF16)
        u = jnp.dot(h, w_ref[...], preferred_element_type=F32)
        uhy_ref[rows, :] = u[:, :n_hy]
        cqn = _rms(u[:, n_hy:n_hy + MLA_Q_RANK], gq_ref[...])
        q = jnp.dot(cqn.astype(BF16), wuq_ref[...], preferred_element_type=F32)
        cq, sq = cq_ref[rows, :], sq_ref[rows, :]
        for hd in range(MLA_HEADS):
            cols = slice(hd * LANES, (hd + 1) * LANES)
            q_ref[rows, cols] = _rope(q[:, cols], cq, sq, MLA_ROPE // 4).astype(BF16)
        ckr_ref[rows, :MLA_KV_RANK] = _rms(u[:, o:o + MLA_KV_RANK], gkv_ref[...])
        ckr_ref[rows, MLA_KV_RANK:] = _rope(u[:, o + MLA_KV_RANK:], ck_ref[rows, :], sk_ref[rows, :], MLA_ROPE // 4)


def _even_in(tok, x_parts, mod, w_in_p, gq, wuq_p, gkv, tabs, tm):
    cq, sq, ck, sk = tabs
    d = x_parts[0].shape[1]
    cm, rm = tok.cond_map(tm), tok.rope_map(tm)
    qw = MLA_HEADS * LANES
    row = lambda m: (m, 0)
    full = lambda m: (0, 0)
    tab = pl.BlockSpec((tm, LANES), lambda m: (rm(m), 0))
    return pl.pallas_call(
        functools.partial(_even_in_kernel, n_x=len(x_parts), n_ctx_tiles=tok.t_ctx // tm),
        grid=(tok.t // tm,),
        in_specs=tok.part_specs(tm, len(x_parts), d) + [
            pl.BlockSpec((1, 6, d), lambda m: (cm(m), 0, 0)),
            pl.BlockSpec((d, EV_W), full),
            pl.BlockSpec((1, MLA_Q_RANK), full),
            pl.BlockSpec((MLA_Q_RANK, qw), full),
            pl.BlockSpec((1, MLA_KV_RANK), full),
            tab, tab, tab, tab],
        out_specs=[pl.BlockSpec((tm, 3 * HY_DIM), row),
                   pl.BlockSpec((tm, qw), row),
                   pl.BlockSpec((tm, 2 * LANES), row)],
        out_shape=[jax.ShapeDtypeStruct((tok.t, 3 * HY_DIM), F32),
                   jax.ShapeDtypeStruct((tok.t, qw), BF16),
                   jax.ShapeDtypeStruct((tok.t, 2 * LANES), F32)],
        compiler_params=_cparams(("arbitrary",), 48),
        name="even_in",
    )(*x_parts, mod, w_in_p, gq, wuq_p, gkv, cq, sq, ck, sk)


def _odd_in_kernel(x_ref, mod_ref, w_ref, gq_ref, gk_ref, seg_ref, cq_ref, sq_ref, ck_ref, sk_ref,
                   glu_ref, q_ref, kv_ref):
    n_q = GQA_HEADS * GQA_HD
    n_k = GQA_KV_HEADS * GQA_HD
    o = 2 * CV_DIM + n_q
    seg = seg_ref[...]
    for r in range(glu_ref.shape[0] // SUB_ROWS):
        rows = slice(r * SUB_ROWS, (r + 1) * SUB_ROWS)
        h = (x_ref[rows, :] * (1.0 + mod_ref[0, 1:2, :]) + mod_ref[0, 0:1, :]).astype(BF16)
        u = jnp.dot(h, w_ref[...], preferred_element_type=F32)
        glu_ref[rows, :] = (u[:, :CV_DIM] * jax.nn.sigmoid(u[:, CV_DIM:2 * CV_DIM])).astype(BF16)
        gq, cq, sq = gq_ref[...], cq_ref[rows, :], sq_ref[rows, :]
        for j in range(n_q // LANES):
            q = u[:, 2 * CV_DIM + j * LANES:2 * CV_DIM + (j + 1) * LANES]
            qn = q * lax.rsqrt(_seg_mean(q * q, seg) + RMS_EPS) * gq
            q_ref[rows, j * LANES:(j + 1) * LANES] = _rope(qn, cq, sq, GQA_HD // 4).astype(BF16)
        k = u[:, o:o + n_k]
        kn = k * lax.rsqrt(_seg_mean(k * k, seg) + RMS_EPS) * gk_ref[...]
        kv_ref[rows, :n_k] = _rope(kn, ck_ref[rows, :], sk_ref[rows, :], GQA_HD // 4)
        kv_ref[rows, n_k:] = u[:, o + n_k:]


def _odd_in(tok, x, mod, w_in, gq, gk, tabs, tm):
    cq, sq, ck, sk = tabs
    d, n = w_in.shape
    n_q = GQA_HEADS * GQA_HD
    n_k = GQA_KV_HEADS * GQA_HD
    cm, rm = tok.cond_map(tm), tok.rope_map(tm)
    i = jnp.arange(LANES)
    seg = jnp.where((i[:, None] // GQA_HD) == (i[None, :] // GQA_HD), 1.0 / GQA_HD, 0.0).astype(BF16)
    row = lambda m: (m, 0)
    full = lambda m: (0, 0)
    tab = pl.BlockSpec((tm, LANES), lambda m: (rm(m), 0))
    return pl.pallas_call(
        _odd_in_kernel,
        grid=(tok.t // tm,),
        in_specs=[pl.BlockSpec((tm, d), row),
                  pl.BlockSpec((1, 6, d), lambda m: (cm(m), 0, 0)),
                  pl.BlockSpec((d, n), full),
                  pl.BlockSpec((1, LANES), full),
                  pl.BlockSpec((1, LANES), full),
                  pl.BlockSpec((LANES, LANES), full),
                  tab, tab, tab, tab],
        out_specs=[pl.BlockSpec((tm, CV_DIM), row),
                   pl.BlockSpec((tm, n_q), row),
                   pl.BlockSpec((tm, 2 * n_k), row)],
        out_shape=[jax.ShapeDtypeStruct((tok.t, CV_DIM), BF16),
                   jax.ShapeDtypeStruct((tok.t, n_q), BF16),
                   jax.ShapeDtypeStruct((tok.t, 2 * n_k), F32)],
        compiler_params=_cparams(("arbitrary",), 48),
        name="odd_in",
    )(x, mod, w_in, gq, gk, seg, cq, sq, ck, sk)


def _softmax_pv(q, k, v):
    s = lax.dot_general(q, k, (((1,), (1,)), ((), ())), preferred_element_type=F32)
    p = jnp.exp(s - jnp.max(s, axis=-1, keepdims=True))
    l = jnp.sum(p, axis=-1, keepdims=True)
    return jnp.dot(p.astype(BF16), v, preferred_element_type=F32) * (1.0 / l)


def _softmax_pv_t(q, k, vt):
    st = lax.dot_general(k, q, (((1,), (1,)), ((), ())), preferred_element_type=F32)
    pt = jnp.exp(st - jnp.max(st, axis=0, keepdims=True))
    l = jnp.sum(pt, axis=0, keepdims=True)
    return jnp.dot(vt, pt.astype(BF16), preferred_element_type=F32) * (1.0 / l)


def _mla_attn_kernel(*refs, seg_lens, has_cache, pps, keys_on_rows):
    if has_cache:
        q_ref, cckv_ref, ckr_ref, own_ref, wk_ref, wv_ref, o_ref, ck_scr, k_scr, v_scr = refs
    else:
        q_ref, own_ref, wk_ref, wv_ref, o_ref, ck_scr, k_scr, v_scr = refs
    npair = MLA_HEADS // 2

    @pl.when((pl.program_id(1) == 0) & (pl.program_id(2) == 0))
    def _():
        off = 0
        if has_cache:
            n = seg_lens[0]
            ck_scr[0:n, :MLA_KV_RANK] = cckv_ref[0, 0].astype(BF16)
            ck_scr[0:n, MLA_KV_RANK:] = jnp.zeros((n, LANES), BF16)
            ck_scr[0:n, MLA_KV_RANK:MLA_KV_RANK + MLA_ROPE] = ckr_ref[0, 0].astype(BF16)
            off = n
        ck_scr[off:off + seg_lens[-1], :] = own_ref[...].astype(BF16)
        ck = ck_scr[...]
        for hp in range(npair):
            k_scr[hp] = jnp.dot(ck, wk_ref[hp], preferred_element_type=F32).astype(BF16)
        if keys_on_rows:
            vt = lax.dot_general(wv_ref[...], ck[:, :MLA_KV_RANK], (((1,), (1,)), ((), ())),
                                 preferred_element_type=F32).astype(BF16)
            for h in range(MLA_HEADS):
                v_scr[h] = vt[h * MLA_V:(h + 1) * MLA_V, :]
        else:
            for h in range(MLA_HEADS):
                v_scr[h] = jnp.dot(ck[:, :MLA_KV_RANK], wv_ref[h], preferred_element_type=F32).astype(BF16)

    for lp in range(pps):
        gp = lp if pps == npair else pl.program_id(2) * pps + lp
        kp = k_scr[gp]
        q0 = q_ref[:, 2 * lp * LANES:(2 * lp + 1) * LANES]
        q1 = q_ref[:, (2 * lp + 1) * LANES:(2 * lp + 2) * LANES]
        if keys_on_rows:
            o = jnp.concatenate([_softmax_pv_t(q0, kp[:, :LANES], v_scr[2 * gp]),
                                 _softmax_pv_t(q1, kp[:, LANES:], v_scr[2 * gp + 1])], axis=0).T
        else:
            o = _softmax_pv(q0, kp[:, :LANES], v_scr[2 * gp]) + _softmax_pv(q1, kp[:, LANES:], v_scr[2 * gp + 1])
        o_ref[:, lp * LANES:(lp + 1) * LANES] = o.astype(o_ref.dtype)


def _mla_attn(q, ckr, wk, wv, tok_off, nb, lq, tq, pps, cache=None):
    keys_on_rows = wv.ndim == 2
    nq = lq // tq
    qb0 = tok_off // tq
    ob0 = tok_off // lq
    npair = MLA_HEADS // 2
    seg_lens = (lq,) if cache is None else (cache[0].shape[2], lq)
    lk = sum(seg_lens)
    in_specs = [pl.BlockSpec((tq, 2 * LANES * pps), lambda b, i, p: (qb0 + b * nq + i, p))]
    args = [q]
    if cache is not None:
        cckv, ckr_c, layer = cache
        past = cckv.shape[2]
        in_specs += [pl.BlockSpec((1, 1, past, MLA_KV_RANK), lambda b, i, p: (b, layer, 0, 0)),
                     pl.BlockSpec((1, 1, past, MLA_ROPE), lambda b, i, p: (b, layer, 0, 0))]
        args += [cckv, ckr_c]
    in_specs += [pl.BlockSpec((lq, 2 * LANES), lambda b, i, p: (ob0 + b, 0)),
                 pl.BlockSpec((npair, 2 * LANES, 2 * LANES), lambda b, i, p: (0, 0, 0)),
                 pl.BlockSpec(wv.shape, lambda b, i, p: (0,) * wv.ndim)]
    args += [ckr, wk, wv]
    return pl.pallas_call(
        functools.partial(_mla_attn_kernel, seg_lens=seg_lens, has_cache=cache is not None, pps=pps,
                          keys_on_rows=keys_on_rows),
        grid=(nb, nq, npair // pps),
        in_specs=in_specs,
        out_specs=pl.BlockSpec((tq, LANES * pps), lambda b, i, p: (b * nq + i, p)),
        out_shape=jax.ShapeDtypeStruct((nb * lq, MLA_HEADS * MLA_V), BF16),
        scratch_shapes=[pltpu.VMEM((lk, 2 * LANES), BF16),
                        pltpu.VMEM((npair, lk, 2 * LANES), BF16),
                        pltpu.VMEM((MLA_HEADS, MLA_V, lk) if keys_on_rows else (MLA_HEADS, lk, LANES), BF16)],
        compiler_params=_cparams(("arbitrary", "arbitrary", "arbitrary"), 48),
        name="mla_attn_lat" if cache is not None else "mla_attn_ctx",
    )(*args)


def _gqa_attn_kernel(*refs, seg_lens, has_cache, pps, keys_on_rows):
    if has_cache:
        q_ref, ck_ref, cv_ref, own_ref, sel_ref, eye_ref, o_ref, kv_scr, k_scr, v_scr = refs
    else:
        q_ref, own_ref, sel_ref, eye_ref, o_ref, kv_scr, k_scr, v_scr = refs
    n_k = GQA_KV_HEADS * GQA_HD
    npair = GQA_HEADS // 2
    group_pairs = GQA_HEADS // GQA_KV_HEADS // 2

    @pl.when((pl.program_id(1) == 0) & (pl.program_id(2) == 0))
    def _():
        off = 0
        if has_cache:
            n = seg_lens[0]
            kv_scr[0:n, :n_k] = ck_ref[0, 0].astype(BF16)
            kv_scr[0:n, n_k:] = cv_ref[0, 0].astype(BF16)
            off = n
        kv_scr[off:off + seg_lens[-1], :] = own_ref[...].astype(BF16)
        kk = kv_scr[:, :n_k]
        vv = kv_scr[:, n_k:]
        for s in range(2 * GQA_KV_HEADS):
            k_scr[s] = jnp.dot(kk, sel_ref[s], preferred_element_type=F32).astype(BF16)
        if keys_on_rows:
            vt = lax.dot_general(eye_ref[...], vv, (((1,), (1,)), ((), ())), preferred_element_type=F32).astype(BF16)
            for kvh in range(GQA_KV_HEADS):
                v_scr[kvh] = vt[kvh * GQA_HD:(kvh + 1) * GQA_HD, :]
        else:
            for s in range(2 * GQA_KV_HEADS):
                v_scr[s] = jnp.dot(vv, sel_ref[s], preferred_element_type=F32).astype(BF16)

    for lp in range(pps):
        gp = lp if pps == npair else pl.program_id(2) * pps + lp
        kvh = gp // group_pairs
        q = q_ref[:, lp * LANES:(lp + 1) * LANES]
        if keys_on_rows:
            vt = v_scr[kvh]
            o = jnp.concatenate([_softmax_pv_t(q, k_scr[2 * kvh], vt),
                                 _softmax_pv_t(q, k_scr[2 * kvh + 1], vt)], axis=0).T
        else:
            o = (_softmax_pv(q, k_scr[2 * kvh], v_scr[2 * kvh])
                 + _softmax_pv(q, k_scr[2 * kvh + 1], v_scr[2 * kvh + 1]))
        o_ref[:, lp * LANES:(lp + 1) * LANES] = o.astype(o_ref.dtype)


def _gqa_attn(q, kv, tok_off, nb, lq, tq, pps, cache=None):
    keys_on_rows = False
    nq = lq // tq
    qb0 = tok_off // tq
    ob0 = tok_off // lq
    npair = GQA_HEADS // 2
    n_k = GQA_KV_HEADS * GQA_HD
    seg_lens = (lq,) if cache is None else (cache[0].shape[2], lq)
    lk = sum(seg_lens)
    src = jnp.arange(n_k)[:, None]
    dst = jnp.arange(n_k)[None, :]
    sel = jnp.stack([jnp.where((src // GQA_HD == kvh) & (dst // GQA_HD == i) & (src % GQA_HD == dst % GQA_HD), 1.0, 0.0)
                     for kvh in range(GQA_KV_HEADS) for i in range(2)]).astype(BF16)
    in_specs = [pl.BlockSpec((tq, LANES * pps), lambda b, i, p: (qb0 + b * nq + i, p))]
    args = [q]
    if cache is not None:
        ck, cv, layer = cache
        past = ck.shape[2]
        in_specs += [pl.BlockSpec((1, 1, past, n_k), lambda b, i, p: (b, layer, 0, 0)),
                     pl.BlockSpec((1, 1, past, n_k), lambda b, i, p: (b, layer, 0, 0))]
        args += [ck, cv]
    in_specs += [pl.BlockSpec((lq, 2 * n_k), lambda b, i, p: (ob0 + b, 0)),
                 pl.BlockSpec((2 * GQA_KV_HEADS, n_k, n_k), lambda b, i, p: (0, 0, 0)),
                 pl.BlockSpec((n_k, n_k), lambda b, i, p: (0, 0))]
    args += [kv, sel, jnp.eye(n_k, dtype=BF16)]
    return pl.pallas_call(
        functools.partial(_gqa_attn_kernel, seg_lens=seg_lens, has_cache=cache is not None, pps=pps,
                          keys_on_rows=keys_on_rows),
        grid=(nb, nq, npair // pps),
        in_specs=in_specs,
        out_specs=pl.BlockSpec((tq, LANES * pps), lambda b, i, p: (b * nq + i, p)),
        out_shape=jax.ShapeDtypeStruct((nb * lq, GQA_HEADS * GQA_HD), BF16),
        scratch_shapes=[pltpu.VMEM((lk, 2 * n_k), BF16),
                        pltpu.VMEM((2 * GQA_KV_HEADS, lk, n_k), BF16),
                        pltpu.VMEM((GQA_KV_HEADS, GQA_HD, lk) if keys_on_rows else (2 * GQA_KV_HEADS, lk, n_k), BF16)],
        compiler_params=_cparams(("arbitrary", "arbitrary", "arbitrary"), 48),
        name="gqa_attn_lat" if cache is not None else "gqa_attn_ctx",
    )(*args)


def _dft_mats(lb):
    n = 2 * lb
    r = jnp.arange(n, dtype=jnp.int32)
    nyq = r == lb
    f = jnp.where(nyq, lb, r % lb)
    is_im = (r >= lb) & ~nyq
    s = jnp.arange(lb, dtype=jnp.int32)

    def mat(pos):
        ang = ((f[:, None] * pos[None, :]) % n).astype(F32) * (2.0 * math.pi / n)
        return jnp.where(is_im[:, None], -jnp.sin(ang), jnp.cos(ang))

    fwd = mat(s)
    fwd_rev = jnp.where(s[None, :] == 0, 0.0, mat(lb - s))
    wgt = jnp.where((r == 0) | nyq, 1.0 / n, 2.0 / n)
    inv = (fwd * wgt[:, None]).T
    return fwd.astype(BF16), inv.astype(BF16), fwd_rev.astype(BF16)


def _spec_kernel(f_ref, fr_ref, ka_ref, kb_ref, p_ref, q_ref, p2_ref, *, lb):
    tf = jnp.dot(f_ref[...], ka_ref[...], preferred_element_type=F32)
    tb = jnp.dot(fr_ref[...], kb_ref[...], preferred_element_type=F32)
    re = tf[:lb] + tb[:lb]
    im = tf[lb:] - tb[lb:]
    nyq = tf[lb:] + tb[lb:]
    row0 = lax.broadcasted_iota(jnp.int32, re.shape, 0) == 0
    p_ref[0] = re
    q_ref[0] = jnp.where(row0, 0.0, im)
    p2_ref[0] = jnp.where(row0, nyq, re)


def _filter_spectrum(mats, k2, dmax):
    fwd_mat, _, fwd_rev = mats
    n, lb = fwd_mat.shape
    cw = k2.shape[1]
    nbk = k2.shape[0] // n
    nd = 2 * dmax + 1
    tc = 512
    out = jax.ShapeDtypeStruct((nd, lb, cw), F32)
    blk = pl.BlockSpec((1, lb, tc), lambda c, d: (d, 0, c))
    mat = pl.BlockSpec((n, lb), lambda c, d: (0, 0))
    return pl.pallas_call(
        functools.partial(_spec_kernel, lb=lb),
        grid=(cw // tc, nd),
        in_specs=[mat, mat,
                  pl.BlockSpec((lb, tc), lambda c, d: (nbk + d - dmax, c)),
                  pl.BlockSpec((lb, tc), lambda c, d: (nbk + d - dmax - 1, c))],
        out_specs=[blk, blk, blk],
        out_shape=[out, out, out],
        compiler_params=_cparams(("arbitrary", "arbitrary"), 32),
        name="filter_spectrum",
    )(fwd_mat, fwd_rev, k2, k2)


def _block_conv(zbf_ref, z_scr, f_ref, fi_ref, p_ref, q_ref, p2_ref, lb, nblk, nbk, dmax, emit):
    for r in range(nblk):
        z_scr[r] = jnp.dot(f_ref[...], zbf_ref[r * lb:(r + 1) * lb, :], preferred_element_type=F32)
    for r in range(nblk):
        s, i = divmod(r, nbk)
        ya = yb = None
        for j in range(nbk):
            d = i - j
            if abs(d) > dmax:
                continue
            re = z_scr[s * nbk + j, :lb, :]
            im = z_scr[s * nbk + j, lb:, :]
            p, q, p2 = p_ref[d + dmax], q_ref[d + dmax], p2_ref[d + dmax]
            ta = re * p - im * q
            tb = re * q + im * p2
            ya = ta if ya is None else ya + ta
            yb = tb if yb is None else yb + tb
        y = jnp.concatenate([ya, yb], axis=0).astype(BF16)
        emit(r, jnp.dot(fi_ref[...], y, preferred_element_type=F32))


def _short_conv(u, w_ref, b_ref, l):
    rows = u.shape[0]
    t = jnp.bitwise_and(lax.broadcasted_iota(jnp.int32, u.shape, 0), l - 1)
    prev = jnp.where(t == 0, 0.0, pltpu.roll(u, 1, 0))
    nxt = jnp.where(t == l - 1, 0.0, pltpu.roll(u, rows - 1, 0))
    return w_ref[0:1, :] * prev + w_ref[1:2, :] * u + w_ref[2:3, :] * nxt + b_ref[...]


def _hyena_kernel(*refs, l, lb, nseq, dmax, conv_z):
    if conv_z:
        (uz_ref, ug_ref, cwz_ref, cbz_ref, cwg_ref, cbg_ref, skip_ref,
         f_ref, fi_ref, p_ref, q_ref, p2_ref, o_ref, zf_scr, g_scr, zbf_scr, z_scr) = refs
    else:
        (uz_ref, ug_ref, cwg_ref, cbg_ref, skip_ref,
         f_ref, fi_ref, p_ref, q_ref, p2_ref, o_ref, zf_scr, g_scr, zbf_scr, z_scr) = refs
    z = _short_conv(uz_ref[...], cwz_ref, cbz_ref, l) if conv_z else uz_ref[...]
    zf_scr[...] = z
    zbf_scr[...] = z.astype(BF16)
    g_scr[...] = _short_conv(ug_ref[...], cwg_ref, cbg_ref, l)
    skip = skip_ref[...]

    def emit(r, y):
        rows = slice(r * lb, (r + 1) * lb)
        o_ref[rows, :] = (g_scr[rows, :] * (y + skip * zf_scr[rows, :])).astype(o_ref.dtype)

    nbk = l // lb
    _block_conv(zbf_scr, z_scr, f_ref, fi_ref, p_ref, q_ref, p2_ref, lb, nseq * nbk, nbk, dmax, emit)


def _hyena_stage(z_src, z_col, u_hy, g_col, conv_w, conv_b, skip, mats, spec, spec_col,
                 tok_off, nb, l, nseq, tc, conv_z, out_dtype):
    fwd_mat, inv_mat, _ = mats
    p_arr, q_arr, p2_arr = spec
    n, lb = fwd_mat.shape
    nd = p_arr.shape[0]
    dmax = nd // 2
    rows = nseq * l
    rb0 = tok_off // rows
    zrb0 = rb0 if conv_z else 0
    cpb = HY_DIM // tc
    cvec = lambda g: pl.BlockSpec((1, tc), lambda c, b: (0, g * cpb + c))
    in_specs = [pl.BlockSpec((rows, tc), lambda c, b: (zrb0 + b, z_col * cpb + c)),
                pl.BlockSpec((rows, tc), lambda c, b: (rb0 + b, g_col * cpb + c))]
    args = [z_src, u_hy]
    if conv_z:
        in_specs += [pl.BlockSpec((3, tc), lambda c, b: (0, z_col * cpb + c)), cvec(z_col)]
        args += [conv_w, conv_b]
    in_specs += [pl.BlockSpec((3, tc), lambda c, b: (0, g_col * cpb + c)), cvec(g_col),
                 pl.BlockSpec((1, tc), lambda c, b: (0, c)),
                 pl.BlockSpec((n, lb), lambda c, b: (0, 0)),
                 pl.BlockSpec((lb, n), lambda c, b: (0, 0))]
    args += [conv_w, conv_b, skip, fwd_mat, inv_mat]
    sspec = pl.BlockSpec((nd, lb, tc), lambda c, b: (0, 0, spec_col * cpb + c), pipeline_mode=pl.Buffered(1))
    in_specs += [sspec, sspec, sspec]
    args += [p_arr, q_arr, p2_arr]
    nblk = rows // lb
    return pl.pallas_call(
        functools.partial(_hyena_kernel, l=l, lb=lb, nseq=nseq, dmax=dmax, conv_z=conv_z),
        grid=(HY_DIM // tc, nb // nseq),
        in_specs=in_specs,
        out_specs=pl.BlockSpec((rows, tc), lambda c, b: (b, c)),
        out_shape=jax.ShapeDtypeStruct((nb * l, HY_DIM), out_dtype),
        scratch_shapes=[pltpu.VMEM((rows, tc), F32), pltpu.VMEM((rows, tc), F32),
                        pltpu.VMEM((rows, tc), BF16), pltpu.VMEM((nblk, n, tc), F32)],
        compiler_params=_cparams(("arbitrary", "arbitrary"), 56),
        name="hyena_stage",
    )(*args)


def _conformer_kernel(glu_ref, b_ref, lng_ref, lnb_ref, f_ref, fi_ref, p_ref, q_ref, p2_ref, o_ref, z_scr,
                      *, l, lb, nseq, dmax):
    bias, lng, lnb = b_ref[...], lng_ref[...], lnb_ref[...]

    def emit(r, y):
        yn = _layer_norm(y + bias, lng, lnb)
        o_ref[r * lb:(r + 1) * lb, :] = (yn * jax.nn.sigmoid(yn)).astype(o_ref.dtype)

    nbk = l // lb
    _block_conv(glu_ref, z_scr, f_ref, fi_ref, p_ref, q_ref, p2_ref, lb, nseq * nbk, nbk, dmax, emit)


def _conformer(glu, bias, ln_g, ln_b, mats, spec, tok_off, nb, l, nseq):
    fwd_mat, inv_mat, _ = mats
    n, lb = fwd_mat.shape
    nd = spec[0].shape[0]
    rows = nseq * l
    rb0 = tok_off // rows
    c = CV_DIM
    vec = pl.BlockSpec((1, c), lambda b: (0, 0))
    sspec = pl.BlockSpec((nd, lb, c), lambda b: (0, 0, 0))
    return pl.pallas_call(
        functools.partial(_conformer_kernel, l=l, lb=lb, nseq=nseq, dmax=nd // 2),
        grid=(nb // nseq,),
        in_specs=[pl.BlockSpec((rows, c), lambda b: (rb0 + b, 0)),
                  vec, vec, vec,
                  pl.BlockSpec((n, lb), lambda b: (0, 0)),
                  pl.BlockSpec((lb, n), lambda b: (0, 0)),
                  sspec, sspec, sspec],
        out_specs=pl.BlockSpec((rows, c), lambda b: (b, 0)),
        out_shape=jax.ShapeDtypeStruct((nb * l, c), BF16),
        scratch_shapes=[pltpu.VMEM((rows // lb, n, c), F32)],
        compiler_params=_cparams(("arbitrary",), 56),
        name="conformer",
    )(glu, bias, ln_g, ln_b, fwd_mat, inv_mat, *spec)


def _filter_kernel(z_ref, w1_ref, b1_ref, w2_ref, b2_ref, w3_ref, freq_ref, decay_ref, o_ref):
    hp = lax.Precision.HIGHEST
    z = z_ref[...]
    freq = freq_ref[...]
    hid = jnp.sin(freq * (jnp.dot(z, w1_ref[...], precision=hp, preferred_element_type=F32) + b1_ref[...]))
    hid = jnp.sin(freq * (jnp.dot(hid, w2_ref[...], precision=hp, preferred_element_type=F32) + b2_ref[...]))
    h = jnp.dot(hid, w3_ref[...], precision=hp, preferred_element_type=F32)
    h = h * jnp.exp(-z[:, 0:1] * jnp.abs(decay_ref[...]))
    row = lax.broadcasted_iota(jnp.int32, h.shape, 0) + pl.program_id(0) * h.shape[0]
    o_ref[...] = jnp.where(row == 0, 0.0, h).astype(o_ref.dtype)


def _hyena_filter(l, w1, b1, w2, b2, w3, freq, decay):
    j = jnp.arange(2 * l, dtype=jnp.int32)
    t = jnp.abs(j - l).astype(F32) / l
    bands = jnp.arange(1, HY_BANDS + 1, dtype=F32)
    ang = 2.0 * math.pi * t[:, None] * bands[None, :]
    z = jnp.concatenate([t[:, None], jnp.cos(ang), jnp.sin(ang), jnp.zeros((2 * l, LANES - HY_EMB), F32)], axis=-1)
    w1p = jnp.concatenate([w1, jnp.zeros((LANES - HY_EMB, HY_FILT_HID), F32)], axis=0)
    nout = w3.shape[1] // 2
    tl = 256
    n_bwd = l // tl
    full = lambda i: (0, 0)
    half = lambda i: (0, jnp.where(i < n_bwd, 1, 0))
    return pl.pallas_call(
        _filter_kernel,
        grid=(2 * l // tl,),
        in_specs=[pl.BlockSpec((tl, LANES), lambda i: (i, 0)),
                  pl.BlockSpec((LANES, HY_FILT_HID), full),
                  pl.BlockSpec((1, HY_FILT_HID), full),
                  pl.BlockSpec((HY_FILT_HID, HY_FILT_HID), full),
                  pl.BlockSpec((1, HY_FILT_HID), full),
                  pl.BlockSpec((HY_FILT_HID, nout), half),
                  pl.BlockSpec((1, HY_FILT_HID), full),
                  pl.BlockSpec((1, nout), half)],
        out_specs=pl.BlockSpec((tl, nout), lambda i: (i, 0)),
        out_shape=jax.ShapeDtypeStruct((2 * l, nout), BF16),
        compiler_params=_cparams(("arbitrary",), 32),
        name="hyena_filter",
    )(z, w1p, b1[None], w2, b2[None], w3, freq[None], decay[None])


def _out_proj_kernel(*refs, n_x, n_ctx_tiles):
    ya_refs, yb_refs, x_refs = refs[0:2], refs[2:4], refs[4:4 + n_x]
    mod_ref, wa_ref, wb_ref, g_ref, b_ref, o_ref = refs[4 + n_x:]
    for r in range(o_ref.shape[0] // SUB_ROWS):
        rows = slice(r * SUB_ROWS, (r + 1) * SUB_ROWS)
        y = (jnp.dot(_read_parts(ya_refs, n_ctx_tiles, rows), wa_ref[...], preferred_element_type=F32)
             + jnp.dot(_read_parts(yb_refs, n_ctx_tiles, rows), wb_ref[...], preferred_element_type=F32))
        z = ALPHA * _read_parts(x_refs, n_ctx_tiles, rows) + mod_ref[0, 2:3, :] * y
        o_ref[rows, :] = _layer_norm(z, g_ref[...], b_ref[...])


def _out_proj(tok, ya_parts, yb_parts, x_parts, mod, w_out, ln_g, ln_b, tm):
    d = x_parts[0].shape[1]
    ka, kb = ya_parts[0].shape[1], yb_parts[0].shape[1]
    cm = tok.cond_map(tm)
    full = lambda m: (0, 0)
    return pl.pallas_call(
        functools.partial(_out_proj_kernel, n_x=len(x_parts), n_ctx_tiles=tok.t_ctx // tm),
        grid=(tok.t // tm,),
        in_specs=(tok.part_specs(tm, 2, ka) + tok.part_specs(tm, 2, kb) + tok.part_specs(tm, len(x_parts), d) + [
            pl.BlockSpec((1, 6, d), lambda m: (cm(m), 0, 0)),
            pl.BlockSpec((ka, d), lambda m: (0, 0)),
            pl.BlockSpec((kb, d), lambda m: (ka // kb, 0)),
            pl.BlockSpec((1, d), full),
            pl.BlockSpec((1, d), full)]),
        out_specs=pl.BlockSpec((tm, d), lambda m: (m, 0)),
        out_shape=jax.ShapeDtypeStruct((tok.t, d), F32),
        compiler_params=_cparams(("arbitrary",), 40),
        name="out_proj_ln",
    )(*ya_parts, *yb_parts, *x_parts, mod, w_out, w_out, ln_g, ln_b)


MLP_CHUNK = 512


def _mlp_kernel(x_ref, mod_ref, w1_ref, b1_ref, w2_ref, b2_ref, g_ref, b_ref, o_ref, h_scr, acc_scr, *, nf):
    f = pl.program_id(1)

    @pl.when(f == 0)
    def _():
        h_scr[...] = (x_ref[...] * (1.0 + mod_ref[0, 4:5, :]) + mod_ref[0, 3:4, :]).astype(BF16)
        acc_scr[...] = jnp.zeros_like(acc_scr)

    part = None
    for c0 in range(0, w1_ref.shape[1], MLP_CHUNK):
        cols = slice(c0, c0 + MLP_CHUNK)
        a = jnp.maximum(jnp.dot(h_scr[...], w1_ref[:, cols], preferred_element_type=F32) + b1_ref[:, cols], 0.0)
        y = jnp.dot((a * a).astype(BF16), w2_ref[cols, :], preferred_element_type=F32)
        part = y if part is None else part + y
    acc_scr[...] += part

    @pl.when(f == nf - 1)
    def _():
        z = ALPHA * x_ref[...] + mod_ref[0, 5:6, :] * (acc_scr[...] + b2_ref[...])
        o_ref[...] = _layer_norm(z, g_ref[...], b_ref[...])


def _mlp(tok, x, mod, layer, w1, b1, w2, b2, ln_g, ln_b, tm, tf, tok_off, n_rows):
    d = x.shape[1]
    dff = w1.shape[2]
    nf = dff // tf
    m0 = tok_off // tm
    cm = tok.cond_map(tm)
    return pl.pallas_call(
        functools.partial(_mlp_kernel, nf=nf),
        grid=(n_rows // tm, nf),
        in_specs=[pl.BlockSpec((tm, d), lambda m, f: (m0 + m, 0)),
                  pl.BlockSpec((1, 6, d), lambda m, f: (cm(m0 + m), 0, 0)),
                  pl.BlockSpec((None, d, tf), lambda m, f: (layer, 0, f)),
                  pl.BlockSpec((None, 1, tf), lambda m, f: (layer, 0, f)),
                  pl.BlockSpec((None, tf, d), lambda m, f: (layer, f, 0)),
                  pl.BlockSpec((None, 1, d), lambda m, f: (layer, 0, 0)),
                  pl.BlockSpec((1, d), lambda m, f: (0, 0)),
                  pl.BlockSpec((1, d), lambda m, f: (0, 0))],
        out_specs=pl.BlockSpec((tm, d), lambda m, f: (m, 0)),
        out_shape=jax.ShapeDtypeStruct((n_rows, d), F32),
        scratch_shapes=[pltpu.VMEM((tm, d), BF16), pltpu.VMEM((tm, d), F32)],
        compiler_params=_cparams(("arbitrary", "arbitrary"), 56),
        name="mlp_ln",
    )(x, mod, w1, b1, w2, b2, ln_g, ln_b)


def _mla_weights(w_uq, w_ukv):
    hd = MLA_NOPE + MLA_ROPE
    wq = w_uq.reshape(MLA_Q_RANK, MLA_HEADS, hd)
    wq = jnp.pad(wq, ((0, 0), (0, 0), (0, LANES - hd))).reshape(MLA_Q_RANK, MLA_HEADS * LANES)
    wkv = w_ukv.reshape(MLA_KV_RANK, MLA_HEADS, MLA_NOPE + MLA_V).transpose(1, 0, 2)
    w_nope, w_v = wkv[..., :MLA_NOPE], wkv[..., MLA_NOPE:]
    top = jnp.pad(w_nope, ((0, 0), (0, 0), (0, LANES - MLA_NOPE)))
    place = jnp.pad(jnp.eye(MLA_ROPE, dtype=F32), ((0, LANES - MLA_ROPE), (MLA_NOPE, LANES - MLA_NOPE - MLA_ROPE)))
    wk = jnp.concatenate([top, jnp.broadcast_to(place, (MLA_HEADS, LANES, LANES))], axis=1)
    wk = wk.reshape(MLA_HEADS // 2, 2, 2 * LANES, LANES).transpose(0, 2, 1, 3).reshape(MLA_HEADS // 2, 2 * LANES, 2 * LANES)
    wv_t = w_v.transpose(0, 2, 1).reshape(MLA_HEADS * MLA_V, MLA_KV_RANK)
    w_v = w_v.reshape(MLA_HEADS // 2, 2, MLA_KV_RANK, MLA_V)
    wv = jnp.stack([jnp.pad(w_v[:, 0], ((0, 0), (0, 0), (0, LANES - MLA_V))),
                    jnp.pad(w_v[:, 1], ((0, 0), (0, 0), (LANES - MLA_V, 0)))], axis=1)
    wv = wv.reshape(MLA_HEADS, MLA_KV_RANK, LANES)
    return wq.astype(BF16), wk.astype(BF16), wv.astype(BF16), wv_t.astype(BF16)


def _conformer_taps(dw_w, l):
    k, c = dw_w.shape
    half = k // 2
    return jnp.pad(dw_w[::-1], ((l - half, l - (k - half)), (0, 0))).astype(BF16)


def kernel(x_prompt, x_sample, cache_mla_ckv, cache_mla_krope, cache_gqa_k, cache_gqa_v, c, c_ctx, ev_w_in, hy_conv_w, hy_conv_b, hy_filt_w1, hy_filt_b1, hy_filt_w2, hy_filt_b2, hy_filt_w3, hy_sin_freq, hy_decay, hy_skip, mla_q_norm_g, mla_w_uq, mla_kv_norm_g, mla_w_ukv, ev_w_out, od_w_in, cv_dw_w, cv_dw_b, cv_ln_g, cv_ln_b, gqa_q_norm_g, gqa_k_norm_g, od_w_out, ada_w, ada_b, ln_g, ln_b, mlp_w1, mlp_b1, mlp_w2, mlp_b2):
    bc, lc, d = x_prompt.shape
    bl, ll, _ = x_sample.shape
    tok = _Tok(bc, lc, bl, ll)
    assert lc & (lc - 1) == 0 and ll & (ll - 1) == 0 and tok.t_ctx % ll == 0 and bl < 8
    n_od = od_w_in.shape[0]
    tm_in = min(512, ll)
    tm_mlp = min(1024, ll)
    tf_mlp = 1024
    tq = 256
    tq_lat = min(256, ll)
    seq_ctx = max(1, 1024 // lc)
    tc_lat = 256 if ll > 1024 else HY_DIM
    cv_half = cv_dw_w.shape[1] // 2

    cond8 = jnp.concatenate([c_ctx[None], c, jnp.zeros((7 - bl, d), F32)], axis=0)
    mods = _ada(cond8, ada_w, ada_b).reshape(DEPTH, 8, 6, d)

    mla_scale = (MLA_NOPE + MLA_ROPE) ** -0.5
    ev_tabs = (_rope_tables(ll, tm_in, 1, LANES, MLA_NOPE, MLA_ROPE, mla_scale)
               + _rope_tables(ll, tm_in, 1, LANES, 0, MLA_ROPE, 1.0))
    od_tabs = (_rope_tables(ll, tm_in, LANES // GQA_HD, GQA_HD, 0, GQA_HD, GQA_HD ** -0.5)
               + _rope_tables(ll, tm_in, LANES // GQA_HD, GQA_HD, 0, GQA_HD, 1.0))
    passes = []
    for off, nb, l, nseq, tc in ((0, bc, lc, seq_ctx, HY_DIM), (tok.t_ctx, bl, ll, 1, tc_lat)):
        lb = min(l, CONV_BLOCK)
        passes.append((off, nb, l, nseq, tc, lb, _dft_mats(lb)))

    mlp_w1_bf = mlp_w1.astype(BF16)
    mlp_w2_bf = mlp_w2.astype(BF16)
    x_parts = (x_prompt.reshape(tok.t_ctx, d), x_sample.reshape(tok.t_lat, d))
    ckv_list, krope_list, k_list, v_list = [], [], [], []

    for layer in range(DEPTH):
        i = layer // 2
        mod = mods[layer]
        if layer % 2 == 0:
            w_in_p = jnp.pad(ev_w_in[i], ((0, 0), (0, EV_W - ev_w_in.shape[2]))).astype(BF16)
            wq, wk, wv, wv_t = _mla_weights(mla_w_uq[i], mla_w_ukv[i])
            u_hy, q, ckr = _even_in(tok, x_parts, mod, w_in_p, mla_q_norm_g[i][None], wq, mla_kv_norm_g[i][None],
                                    ev_tabs, tm_in)
            ckv_list.append(ckr[:tok.t_ctx, :MLA_KV_RANK].reshape(bc, lc, MLA_KV_RANK))
            krope_list.append(ckr[:tok.t_ctx, MLA_KV_RANK:MLA_KV_RANK + MLA_ROPE].reshape(bc, lc, MLA_ROPE))
            yb_parts = (_mla_attn(q, ckr, wk, wv, 0, bc, lc, min(tq, lc), MLA_HEADS // 2),
                        _mla_attn(q, ckr, wk, wv, tok.t_ctx, bl, ll, tq_lat, 1,
                                  cache=(cache_mla_ckv, cache_mla_krope, i)))

            skip = hy_skip[i]
            conv_b = hy_conv_b[i][None]
            ya_parts = []
            for (off, nb, l, nseq, tc, lb, mats) in passes:
                k2 = _hyena_filter(l, hy_filt_w1[i], hy_filt_b1[i], hy_filt_w2[i], hy_filt_b2[i],
                                   hy_filt_w3[i], hy_sin_freq[i], hy_decay[i])
                spec = _filter_spectrum(mats, k2, l // lb - 1)
                z1 = _hyena_stage(u_hy, 2, u_hy, 0, hy_conv_w[i], conv_b, skip[0:1], mats, spec, 0,
                                  off, nb, l, nseq, tc, True, F32)
                ya_parts.append(_hyena_stage(z1, 0, u_hy, 1, hy_conv_w[i], conv_b, skip[1:2], mats, spec, 1,
                                             off, nb, l, nseq, tc, False, BF16))
            w_out = ev_w_out[i].astype(BF16)
        else:
            (x,) = x_parts
            gq = jnp.tile(gqa_q_norm_g[i], LANES // GQA_HD)[None]
            gk = jnp.tile(gqa_k_norm_g[i], GQA_KV_HEADS)[None]
            glu, q, kv = _odd_in(tok, x, mod, od_w_in[i].astype(BF16), gq, gk, od_tabs, tm_in)
            n_k = GQA_KV_HEADS * GQA_HD
            k_list.append(kv[:tok.t_ctx, :n_k].reshape(bc, lc, GQA_KV_HEADS, GQA_HD))
            v_list.append(kv[:tok.t_ctx, n_k:].reshape(bc, lc, GQA_KV_HEADS, GQA_HD))
            past = cache_gqa_k.shape[2]
            cache = (cache_gqa_k.reshape(bl, n_od, past, n_k), cache_gqa_v.reshape(bl, n_od, past, n_k), i)
            yb_parts = (_gqa_attn(q, kv, 0, bc, lc, min(tq, lc), GQA_HEADS // 2),
                        _gqa_attn(q, kv, tok.t_ctx, bl, ll, tq_lat, 1, cache=cache))

            ya_parts = []
            for (off, nb, l, nseq, tc, lb, mats) in passes:
                dmax = min(l // lb - 1, -(-cv_half // lb))
                spec = _filter_spectrum(mats, _conformer_taps(cv_dw_w[i], l), dmax)
                ya_parts.append(_conformer(glu, cv_dw_b[i][None], cv_ln_g[i][None], cv_ln_b[i][None], mats, spec,
                                           off, nb, l, nseq))
            w_out = od_w_out[i].astype(BF16)

        x = _out_proj(tok, ya_parts, yb_parts, x_parts, mod, w_out, ln_g[layer, 0][None], ln_b[layer, 0][None], tm_in)
        mlp_args = (mod, layer, mlp_w1_bf, mlp_b1[:, None, :], mlp_w2_bf, mlp_b2[:, None, :], ln_g[layer, 1][None], ln_b[layer, 1][None], tm_mlp, tf_mlp)
        if layer < DEPTH - 1:
            x_parts = (_mlp(tok, x, *mlp_args, 0, tok.t),)
        else:
            y_prompt = _mlp(tok, x, *mlp_args, 0, tok.t_ctx).reshape(bc, lc, d)
            y_sample = _mlp(tok, x, *mlp_args, tok.t_ctx, tok.t_lat).reshape(bl, ll, d)

    return (y_prompt, y_sample, jnp.stack(ckv_list, axis=1), jnp.stack(krope_list, axis=1),
            jnp.stack(k_list, axis=1), jnp.stack(v_list, axis=1))
```

```python
import functools
import math

import jax
import jax.numpy as jnp
import numpy as np
from jax import lax
from jax.experimental import pallas as pl
from jax.experimental.pallas import tpu as pltpu

F32 = jnp.float32
BF16 = jnp.bfloat16

DEPTH = 4
GRID_W = 64
ALPHA = (2.0 * DEPTH) ** 0.25
LN_EPS = 1e-5
RMS_EPS = 1e-6
ROPE_THETA = 10000.0

HY_DIM = 512
HY_ORDER = 2
HY_BANDS = 16
HY_EMB = 2 * HY_BANDS + 1
HY_FILT_HID = 64

MLA_HEADS = 8
MLA_NOPE = 64
MLA_ROPE = 32
MLA_V = 64
MLA_Q_RANK = 256
MLA_KV_RANK = 128

CV_DIM = 512

GQA_HEADS = 8
GQA_KV_HEADS = 2
GQA_HD = 64

LANES = 128
CONV_BLOCK = 512
MIB = 2 ** 20


def _cparams(sem, vmem_mib):
    return pltpu.CompilerParams(dimension_semantics=sem, vmem_limit_bytes=vmem_mib * MIB)


def _layer_norm(z, g, b):
    mu = jnp.mean(z, axis=-1, keepdims=True)
    zc = z - mu
    var = jnp.mean(zc * zc, axis=-1, keepdims=True)
    return zc * lax.rsqrt(var + LN_EPS) * g + b


def _rms(x, g):
    return x * lax.rsqrt(jnp.mean(x * x, axis=-1, keepdims=True) + RMS_EPS) * g


def _seg_mean(sq, s_mat):
    hi = sq.astype(BF16)
    lo = (sq - hi.astype(F32)).astype(BF16)
    return (jnp.dot(hi, s_mat, preferred_element_type=F32)
            + jnp.dot(lo, s_mat, preferred_element_type=F32))


def _rope(x, cos, sin_signed, half):
    w = x.shape[1]
    lane = lax.broadcasted_iota(jnp.int32, x.shape, 1)
    first = jnp.bitwise_and(lane, 2 * half - 1) < half
    rot = jnp.where(first, pltpu.roll(x, w - half, 1), pltpu.roll(x, half, 1))
    return x * cos + rot * sin_signed


def _ada_kernel(c_ref, w_ref, b_ref, o_ref):
    c = c_ref[...]
    s = (c * jax.nn.sigmoid(c)).astype(BF16)
    o_ref[0] = jnp.dot(s, w_ref[0].astype(BF16), preferred_element_type=F32) + b_ref[0]


def _ada(cond8, ada_w, ada_b):
    depth, d, n = ada_w.shape
    tn = 1536
    return pl.pallas_call(
        _ada_kernel,
        grid=(depth, n // tn),
        in_specs=[pl.BlockSpec((8, d), lambda l, j: (0, 0)),
                  pl.BlockSpec((1, d, tn), lambda l, j: (l, 0, j)),
                  pl.BlockSpec((1, 1, tn), lambda l, j: (l, 0, j))],
        out_specs=pl.BlockSpec((1, 8, tn), lambda l, j: (l, 0, j)),
        out_shape=jax.ShapeDtypeStruct((depth, 8, n), F32),
        compiler_params=_cparams(("arbitrary", "arbitrary"), 32),
        name="ada_mod",
    )(cond8, ada_w, ada_b.reshape(depth, 1, n))


class _Tok:
    def __init__(self, bc, lc, bl, ll):
        self.bc, self.lc, self.bl, self.ll = bc, lc, bl, ll
        self.t_ctx = bc * lc
        self.t_lat = bl * ll
        self.t = self.t_ctx + self.t_lat

    def cond_map(self, tm):
        n_ctx, per_b = self.t_ctx // tm, self.ll // tm
        return lambda m: jnp.where(m < n_ctx, 0, 1 + (m - n_ctx) // per_b)

    def rope_map(self, tm):
        n_ctx, per_b = self.t_ctx // tm, self.ll // tm
        return lambda m: jnp.where(m < n_ctx, 0, 1 + (m - n_ctx) % per_b)

    def part_specs(self, tm, n_parts, width):
        if n_parts == 1:
            return [pl.BlockSpec((tm, width), lambda m: (m, 0))]
        n_ctx = self.t_ctx // tm
        return [pl.BlockSpec((tm, width), lambda m: (jnp.minimum(m, n_ctx - 1), 0)),
                pl.BlockSpec((tm, width), lambda m: (jnp.maximum(m - n_ctx, 0), 0))]


def _read_parts(refs, n_ctx_tiles, rows=slice(None)):
    if len(refs) == 1:
        return refs[0][rows, :]
    return jnp.where(pl.program_id(0) < n_ctx_tiles, refs[0][rows, :], refs[1][rows, :])


SUB_ROWS = 256


def _rope_tables(ll, tm, n_heads, head_w, rope_off, rope_dim, scale):
    half = rope_dim // 2
    quarter = half // 2
    t = np.arange(ll)
    row, col = t // GRID_W, t % GRID_W
    inv = ROPE_THETA ** (-np.arange(0, half, 2, dtype=np.float64) / half)
    ang_r = row[:, None] * inv[None, :]
    ang_c = col[:, None] * inv[None, :]
    ang = np.concatenate([ang_r, ang_r, ang_c, ang_c], axis=-1)
    sign = np.concatenate([-np.ones(quarter), np.ones(quarter)] * 2)
    pad = ((0, 0), (rope_off, head_w - rope_off - rope_dim))
    cos_l = np.tile(np.pad(np.cos(ang), pad, constant_values=1.0), (1, n_heads))
    sin_l = np.tile(np.pad(np.sin(ang) * sign, pad), (1, n_heads))
    w = n_heads * head_w
    cos = np.concatenate([np.ones((tm, w)), cos_l], axis=0) * scale
    sin = np.concatenate([np.zeros((tm, w)), sin_l], axis=0) * scale
    return jnp.asarray(cos, F32), jnp.asarray(sin, F32)


EV_W = 2048


def _even_in_kernel(*refs, n_x, n_ctx_tiles):
    x_refs = refs[:n_x]
    (mod_ref, w_ref, gq_ref, wuq_ref, gkv_ref, cq_ref, sq_ref, ck_ref, sk_ref,
     uhy_ref, q_ref, ckr_ref) = refs[n_x:]
    n_hy = 3 * HY_DIM
    o = n_hy + MLA_Q_RANK
    for r in range(uhy_ref.shape[0] // SUB_ROWS):
        rows = slice(r * SUB_ROWS, (r + 1) * SUB_ROWS)
        x = _read_parts(x_refs, n_ctx_tiles, rows)
        h = (x * (1.0 + mod_ref[0, 1:2, :]) + mod_ref[0, 0:1, :]).astype(BF16)
        u = jnp.dot(h, w_ref[...], preferred_element_type=F32)
        uhy_ref[rows, :] = u[:, :n_hy]
        cqn = _rms(u[:, n_hy:n_hy + MLA_Q_RANK], gq_ref[...])
        q = jnp.dot(cqn.astype(BF16), wuq_ref[...], preferred_element_type=F32)
        cq, sq = cq_ref[rows, :], sq_ref[rows, :]
        for hd in range(MLA_HEADS):
            cols = slice(hd * LANES, (hd + 1) * LANES)
            q_ref[rows, cols] = _rope(q[:, cols], cq, sq, MLA_ROPE // 4).astype(BF16)
        ckr_ref[rows, :MLA_KV_RANK] = _rms(u[:, o:o + MLA_KV_RANK], gkv_ref[...])
        ckr_ref[rows, MLA_KV_RANK:] = _rope(u[:, o + MLA_KV_RANK:], ck_ref[rows, :], sk_ref[rows, :], MLA_ROPE // 4)


def _even_in(tok, x_parts, mod, w_in_p, gq, wuq_p, gkv, tabs, tm):
    cq, sq, ck, sk = tabs
    d = x_parts[0].shape[1]
    cm, rm = tok.cond_map(tm), tok.rope_map(tm)
    qw = MLA_HEADS * LANES
    row = lambda m: (m, 0)
    full = lambda m: (0, 0)
    tab = pl.BlockSpec((tm, LANES), lambda m: (rm(m), 0))
    return pl.pallas_call(
        functools.partial(_even_in_kernel, n_x=len(x_parts), n_ctx_tiles=tok.t_ctx // tm),
        grid=(tok.t // tm,),
        in_specs=tok.part_specs(tm, len(x_parts), d) + [
            pl.BlockSpec((1, 6, d), lambda m: (cm(m), 0, 0)),
            pl.BlockSpec((d, EV_W), full),
            pl.BlockSpec((1, MLA_Q_RANK), full),
            pl.BlockSpec((MLA_Q_RANK, qw), full),
            pl.BlockSpec((1, MLA_KV_RANK), full),
            tab, tab, tab, tab],
        out_specs=[pl.BlockSpec((tm, 3 * HY_DIM), row),
                   pl.BlockSpec((tm, qw), row),
                   pl.BlockSpec((tm, 2 * LANES), row)],
        out_shape=[jax.ShapeDtypeStruct((tok.t, 3 * HY_DIM), F32),
                   jax.ShapeDtypeStruct((tok.t, qw), BF16),
                   jax.ShapeDtypeStruct((tok.t, 2 * LANES), F32)],
        compiler_params=_cparams(("arbitrary",), 48),
        name="even_in",
    )(*x_parts, mod, w_in_p, gq, wuq_p, gkv, cq, sq, ck, sk)


def _odd_in_kernel(x_ref, mod_ref, w_ref, gq_ref, gk_ref, seg_ref, cq_ref, sq_ref, ck_ref, sk_ref,
                   glu_ref, q_ref, kv_ref):
    n_q = GQA_HEADS * GQA_HD
    n_k = GQA_KV_HEADS * GQA_HD
    o = 2 * CV_DIM + n_q
    seg = seg_ref[...]
    for r in range(glu_ref.shape[0] // SUB_ROWS):
        rows = slice(r * SUB_ROWS, (r + 1) * SUB_ROWS)
        h = (x_ref[rows, :] * (1.0 + mod_ref[0, 1:2, :]) + mod_ref[0, 0:1, :]).astype(BF16)
        u = jnp.dot(h, w_ref[...], preferred_element_type=F32)
        glu_ref[rows, :] = (u[:, :CV_DIM] * jax.nn.sigmoid(u[:, CV_DIM:2 * CV_DIM])).astype(BF16)
        gq, cq, sq = gq_ref[...], cq_ref[rows, :], sq_ref[rows, :]
        for j in range(n_q // LANES):
            q = u[:, 2 * CV_DIM + j * LANES:2 * CV_DIM + (j + 1) * LANES]
            qn = q * lax.rsqrt(_seg_mean(q * q, seg) + RMS_EPS) * gq
            q_ref[rows, j * LANES:(j + 1) * LANES] = _rope(qn, cq, sq, GQA_HD // 4).astype(BF16)
        k = u[:, o:o + n_k]
        kn = k * lax.rsqrt(_seg_mean(k * k, seg) + RMS_EPS) * gk_ref[...]
        kv_ref[rows, :n_k] = _rope(kn, ck_ref[rows, :], sk_ref[rows, :], GQA_HD // 4)
        kv_ref[rows, n_k:] = u[:, o + n_k:]


def _odd_in(tok, x, mod, w_in, gq, gk, tabs, tm):
    cq, sq, ck, sk = tabs
    d, n = w_in.shape
    n_q = GQA_HEADS * GQA_HD
    n_k = GQA_KV_HEADS * GQA_HD
    cm, rm = tok.cond_map(tm), tok.rope_map(tm)
    i = np.arange(LANES)
    seg = jnp.asarray(np.where((i[:, None] // GQA_HD) == (i[None, :] // GQA_HD), 1.0 / GQA_HD, 0.0), BF16)
    row = lambda m: (m, 0)
    full = lambda m: (0, 0)
    tab = pl.BlockSpec((tm, LANES), lambda m: (rm(m), 0))
    return pl.pallas_call(
        _odd_in_kernel,
        grid=(tok.t // tm,),
        in_specs=[pl.BlockSpec((tm, d), row),
                  pl.BlockSpec((1, 6, d), lambda m: (cm(m), 0, 0)),
                  pl.BlockSpec((d, n), full),
                  pl.BlockSpec((1, LANES), full),
                  pl.BlockSpec((1, LANES), full),
                  pl.BlockSpec((LANES, LANES), full),
                  tab, tab, tab, tab],
        out_specs=[pl.BlockSpec((tm, CV_DIM), row),
                   pl.BlockSpec((tm, n_q), row),
                   pl.BlockSpec((tm, 2 * n_k), row)],
        out_shape=[jax.ShapeDtypeStruct((tok.t, CV_DIM), BF16),
                   jax.ShapeDtypeStruct((tok.t, n_q), BF16),
                   jax.ShapeDtypeStruct((tok.t, 2 * n_k), F32)],
        compiler_params=_cparams(("arbitrary",), 48),
        name="odd_in",
    )(x, mod, w_in, gq, gk, seg, cq, sq, ck, sk)


LOG2E = math.log2(math.e)
SUM_LANE = (LANES // 2, 0)
LONG_KEYS = 1024


def _softmax_pv(q, k, v, sum_lane):
    s = lax.dot_general(q, k, (((1,), (1,)), ((), ())), preferred_element_type=F32)
    p = jnp.exp2(s - jnp.max(s, axis=-1, keepdims=True))
    pv = jnp.dot(p.astype(BF16), v, preferred_element_type=F32)
    if k.shape[0] >= LONG_KEYS:
        return pv * (1.0 / pv[:, sum_lane:sum_lane + 1])
    return pv * (1.0 / jnp.sum(p, axis=-1, keepdims=True))


def _pair_out(o0, o1):
    lane = lax.broadcasted_iota(jnp.int32, o0.shape, 1)
    return jnp.where(lane < LANES // 2, o0, o1)


def _ones_lane(shape, lane_idx):
    return (lax.broadcasted_iota(jnp.int32, shape, 1) == lane_idx).astype(F32)


def _mla_attn_kernel(*refs, seg_lens, has_cache, pps):
    if has_cache:
        q_ref, cckv_ref, ckr_ref, own_ref, wk_ref, wv_ref, o_ref, ck_scr, k_scr, v_scr = refs
    else:
        q_ref, own_ref, wk_ref, wv_ref, o_ref, ck_scr, k_scr, v_scr = refs
    npair = MLA_HEADS // 2

    @pl.when((pl.program_id(1) == 0) & (pl.program_id(2) == 0))
    def _():
        off = 0
        if has_cache:
            n = seg_lens[0]
            ck_scr[0:n, :MLA_KV_RANK] = cckv_ref[0, 0].astype(BF16)
            ck_scr[0:n, MLA_KV_RANK:] = jnp.zeros((n, LANES), BF16)
            ck_scr[0:n, MLA_KV_RANK:MLA_KV_RANK + MLA_ROPE] = ckr_ref[0, 0].astype(BF16)
            off = n
        ck_scr[off:off + seg_lens[-1], :] = own_ref[...].astype(BF16)
        ck = ck_scr[...]
        for hp in range(npair):
            k_scr[hp] = jnp.dot(ck, wk_ref[hp], preferred_element_type=F32).astype(BF16)
        for h in range(MLA_HEADS):
            v = jnp.dot(ck[:, :MLA_KV_RANK], wv_ref[h], preferred_element_type=F32)
            v_scr[h] = (v + _ones_lane(v.shape, SUM_LANE[h % 2])).astype(BF16)

    for lp in range(pps):
        gp = lp if pps == npair else pl.program_id(2) * pps + lp
        kp = k_scr[gp]
        q0 = q_ref[:, 2 * lp * LANES:(2 * lp + 1) * LANES]
        q1 = q_ref[:, (2 * lp + 1) * LANES:(2 * lp + 2) * LANES]
        o = _pair_out(_softmax_pv(q0, kp[:, :LANES], v_scr[2 * gp], SUM_LANE[0]),
                      _softmax_pv(q1, kp[:, LANES:], v_scr[2 * gp + 1], SUM_LANE[1]))
        o_ref[:, lp * LANES:(lp + 1) * LANES] = o.astype(o_ref.dtype)


def _mla_attn(q, ckr, wk, wv, tok_off, nb, lq, tq, pps, cache=None):
    nq = lq // tq
    qb0 = tok_off // tq
    ob0 = tok_off // lq
    npair = MLA_HEADS // 2
    seg_lens = (lq,) if cache is None else (cache[0].shape[2], lq)
    lk = sum(seg_lens)
    in_specs = [pl.BlockSpec((tq, 2 * LANES * pps), lambda b, i, p: (qb0 + b * nq + i, p))]
    args = [q]
    if cache is not None:
        cckv, ckr_c, layer = cache
        past = cckv.shape[2]
        in_specs += [pl.BlockSpec((1, 1, past, MLA_KV_RANK), lambda b, i, p: (b, layer, 0, 0)),
                     pl.BlockSpec((1, 1, past, MLA_ROPE), lambda b, i, p: (b, layer, 0, 0))]
        args += [cckv, ckr_c]
    in_specs += [pl.BlockSpec((lq, 2 * LANES), lambda b, i, p: (ob0 + b, 0)),
                 pl.BlockSpec((npair, 2 * LANES, 2 * LANES), lambda b, i, p: (0, 0, 0)),
                 pl.BlockSpec((MLA_HEADS, MLA_KV_RANK, LANES), lambda b, i, p: (0, 0, 0))]
    args += [ckr, wk, wv]
    return pl.pallas_call(
        functools.partial(_mla_attn_kernel, seg_lens=seg_lens, has_cache=cache is not None, pps=pps),
        grid=(nb, nq, npair // pps),
        in_specs=in_specs,
        out_specs=pl.BlockSpec((tq, LANES * pps), lambda b, i, p: (b * nq + i, p)),
        out_shape=jax.ShapeDtypeStruct((nb * lq, MLA_HEADS * MLA_V), BF16),
        scratch_shapes=[pltpu.VMEM((lk, 2 * LANES), BF16),
                        pltpu.VMEM((npair, lk, 2 * LANES), BF16),
                        pltpu.VMEM((MLA_HEADS, lk, LANES), BF16)],
        compiler_params=_cparams(("arbitrary", "arbitrary", "arbitrary"), 48),
        name="mla_attn_lat" if cache is not None else "mla_attn_ctx",
    )(*args)


def _gqa_attn_kernel(*refs, seg_lens, has_cache, pps):
    if has_cache:
        q_ref, ck_ref, cv_ref, own_ref, sel_ref, o_ref, kv_scr, k_scr, v_scr = refs
    else:
        q_ref, own_ref, sel_ref, o_ref, kv_scr, k_scr, v_scr = refs
    n_k = GQA_KV_HEADS * GQA_HD
    npair = GQA_HEADS // 2
    group_pairs = GQA_HEADS // GQA_KV_HEADS // 2

    @pl.when((pl.program_id(1) == 0) & (pl.program_id(2) == 0))
    def _():
        off = 0
        if has_cache:
            n = seg_lens[0]
            kv_scr[0:n, :n_k] = ck_ref[0, 0].astype(BF16)
            kv_scr[0:n, n_k:] = cv_ref[0, 0].astype(BF16)
            off = n
        kv_scr[off:off + seg_lens[-1], :] = own_ref[...].astype(BF16)
        kk = kv_scr[:, :n_k]
        vv = kv_scr[:, n_k:]
        for s in range(2 * GQA_KV_HEADS):
            k_scr[s] = jnp.dot(kk, sel_ref[s], preferred_element_type=F32).astype(BF16)
            v = jnp.dot(vv, sel_ref[s], preferred_element_type=F32)
            v_scr[s] = (v + _ones_lane(v.shape, SUM_LANE[s % 2])).astype(BF16)

    for lp in range(pps):
        gp = lp if pps == npair else pl.program_id(2) * pps + lp
        kvh = gp // group_pairs
        q = q_ref[:, lp * LANES:(lp + 1) * LANES]
        o = _pair_out(_softmax_pv(q, k_scr[2 * kvh], v_scr[2 * kvh], SUM_LANE[0]),
                      _softmax_pv(q, k_scr[2 * kvh + 1], v_scr[2 * kvh + 1], SUM_LANE[1]))
        o_ref[:, lp * LANES:(lp + 1) * LANES] = o.astype(o_ref.dtype)


def _gqa_attn(q, kv, tok_off, nb, lq, tq, pps, cache=None):
    nq = lq // tq
    qb0 = tok_off // tq
    ob0 = tok_off // lq
    npair = GQA_HEADS // 2
    n_k = GQA_KV_HEADS * GQA_HD
    seg_lens = (lq,) if cache is None else (cache[0].shape[2], lq)
    lk = sum(seg_lens)
    src = np.arange(n_k)[:, None]
    dst = np.arange(n_k)[None, :]
    sel = jnp.asarray(np.stack([np.where((src // GQA_HD == kvh) & (dst // GQA_HD == i) & (src % GQA_HD == dst % GQA_HD), 1.0, 0.0)
                                for kvh in range(GQA_KV_HEADS) for i in range(2)]), BF16)
    in_specs = [pl.BlockSpec((tq, LANES * pps), lambda b, i, p: (qb0 + b * nq + i, p))]
    args = [q]
    if cache is not None:
        ck, cv, layer = cache
        past = ck.shape[2]
        in_specs += [pl.BlockSpec((1, 1, past, n_k), lambda b, i, p: (b, layer, 0, 0)),
                     pl.BlockSpec((1, 1, past, n_k), lambda b, i, p: (b, layer, 0, 0))]
        args += [ck, cv]
    in_specs += [pl.BlockSpec((lq, 2 * n_k), lambda b, i, p: (ob0 + b, 0)),
                 pl.BlockSpec((2 * GQA_KV_HEADS, n_k, n_k), lambda b, i, p: (0, 0, 0))]
    args += [kv, sel]
    return pl.pallas_call(
        functools.partial(_gqa_attn_kernel, seg_lens=seg_lens, has_cache=cache is not None, pps=pps),
        grid=(nb, nq, npair // pps),
        in_specs=in_specs,
        out_specs=pl.BlockSpec((tq, LANES * pps), lambda b, i, p: (b * nq + i, p)),
        out_shape=jax.ShapeDtypeStruct((nb * lq, GQA_HEADS * GQA_HD), BF16),
        scratch_shapes=[pltpu.VMEM((lk, 2 * n_k), BF16),
                        pltpu.VMEM((2 * GQA_KV_HEADS, lk, n_k), BF16),
                        pltpu.VMEM((2 * GQA_KV_HEADS, lk, n_k), BF16)],
        compiler_params=_cparams(("arbitrary", "arbitrary", "arbitrary"), 48),
        name="gqa_attn_lat" if cache is not None else "gqa_attn_ctx",
    )(*args)


def _dft_mats(lb):
    n = 2 * lb
    r = np.arange(n)
    nyq = r == lb
    f = np.where(nyq, lb, r % lb)
    is_im = (r >= lb) & ~nyq
    s = np.arange(lb)

    def mat(pos):
        ang = ((f[:, None] * pos[None, :]) % n) * (2.0 * math.pi / n)
        return np.where(is_im[:, None], -np.sin(ang), np.cos(ang))

    fwd = mat(s)
    fwd_rev = np.where(s[None, :] == 0, 0.0, mat(lb - s))
    wgt = np.where((r == 0) | nyq, 1.0 / n, 2.0 / n)
    inv = (fwd * wgt[:, None]).T
    return tuple(jnp.asarray(m.astype(np.float32)).astype(BF16) for m in (fwd, inv, fwd_rev))


def _spec_kernel(f_ref, fr_ref, ka_ref, kb_ref, p_ref, q_ref, p2_ref, *, lb):
    tf = jnp.dot(f_ref[...], ka_ref[...], preferred_element_type=F32)
    tb = jnp.dot(fr_ref[...], kb_ref[...], preferred_element_type=F32)
    re = tf[:lb] + tb[:lb]
    im = tf[lb:] - tb[lb:]
    nyq = tf[lb:] + tb[lb:]
    row0 = lax.broadcasted_iota(jnp.int32, re.shape, 0) == 0
    p_ref[0] = re
    q_ref[0] = jnp.where(row0, 0.0, im)
    p2_ref[0] = jnp.where(row0, nyq, re)


def _filter_spectrum(mats, k2, dmax):
    fwd_mat, _, fwd_rev = mats
    n, lb = fwd_mat.shape
    cw = k2.shape[1]
    nbk = k2.shape[0] // n
    nd = 2 * dmax + 1
    tc = 512
    out = jax.ShapeDtypeStruct((nd, lb, cw), F32)
    blk = pl.BlockSpec((1, lb, tc), lambda c, d: (d, 0, c))
    mat = pl.BlockSpec((n, lb), lambda c, d: (0, 0))
    return pl.pallas_call(
        functools.partial(_spec_kernel, lb=lb),
        grid=(cw // tc, nd),
        in_specs=[mat, mat,
                  pl.BlockSpec((lb, tc), lambda c, d: (nbk + d - dmax, c)),
                  pl.BlockSpec((lb, tc), lambda c, d: (nbk + d - dmax - 1, c))],
        out_specs=[blk, blk, blk],
        out_shape=[out, out, out],
        compiler_params=_cparams(("arbitrary", "arbitrary"), 32),
        name="filter_spectrum",
    )(fwd_mat, fwd_rev, k2, k2)


def _block_conv(zbf_ref, z_scr, f_ref, fi_ref, p_ref, q_ref, p2_ref, lb, nblk, nbk, dmax, emit):
    for r in range(nblk):
        z_scr[r] = jnp.dot(f_ref[...], zbf_ref[r * lb:(r + 1) * lb, :], preferred_element_type=F32)
    for r in range(nblk):
        s, i = divmod(r, nbk)
        ya = yb = None
        for j in range(nbk):
            d = i - j
            if abs(d) > dmax:
                continue
            re = z_scr[s * nbk + j, :lb, :]
            im = z_scr[s * nbk + j, lb:, :]
            p, q, p2 = p_ref[d + dmax], q_ref[d + dmax], p2_ref[d + dmax]
            ta = re * p - im * q
            tb = re * q + im * p2
            ya = ta if ya is None else ya + ta
            yb = tb if yb is None else yb + tb
        y = jnp.concatenate([ya, yb], axis=0).astype(BF16)
        emit(r, jnp.dot(fi_ref[...], y, preferred_element_type=F32))


def _short_conv(u, w_ref, b_ref, l):
    rows = u.shape[0]
    t = jnp.bitwise_and(lax.broadcasted_iota(jnp.int32, u.shape, 0), l - 1)
    prev = jnp.where(t == 0, 0.0, pltpu.roll(u, 1, 0))
    nxt = jnp.where(t == l - 1, 0.0, pltpu.roll(u, rows - 1, 0))
    return w_ref[0:1, :] * prev + w_ref[1:2, :] * u + w_ref[2:3, :] * nxt + b_ref[...]


def _hyena_kernel(*refs, l, lb, nseq, dmax, conv_z):
    if conv_z:
        (uz_ref, ug_ref, cwz_ref, cbz_ref, cwg_ref, cbg_ref, skip_ref,
         f_ref, fi_ref, p_ref, q_ref, p2_ref, o_ref, zf_scr, g_scr, zbf_scr, z_scr) = refs
    else:
        (uz_ref, ug_ref, cwg_ref, cbg_ref, skip_ref,
         f_ref, fi_ref, p_ref, q_ref, p2_ref, o_ref, zf_scr, g_scr, zbf_scr, z_scr) = refs
    z = _short_conv(uz_ref[...], cwz_ref, cbz_ref, l) if conv_z else uz_ref[...]
    zf_scr[...] = z
    zbf_scr[...] = z.astype(BF16)
    g_scr[...] = _short_conv(ug_ref[...], cwg_ref, cbg_ref, l)
    skip = skip_ref[...]

    def emit(r, y):
        rows = slice(r * lb, (r + 1) * lb)
        o_ref[rows, :] = (g_scr[rows, :] * (y + skip * zf_scr[rows, :])).astype(o_ref.dtype)

    nbk = l // lb
    _block_conv(zbf_scr, z_scr, f_ref, fi_ref, p_ref, q_ref, p2_ref, lb, nseq * nbk, nbk, dmax, emit)


def _hyena_stage(z_src, z_col, u_hy, g_col, conv_w, conv_b, skip, mats, spec, spec_col,
                 tok_off, nb, l, nseq, tc, conv_z, out_dtype):
    fwd_mat, inv_mat, _ = mats
    p_arr, q_arr, p2_arr = spec
    n, lb = fwd_mat.shape
    nd = p_arr.shape[0]
    dmax = nd // 2
    rows = nseq * l
    rb0 = tok_off // rows
    zrb0 = rb0 if conv_z else 0
    cpb = HY_DIM // tc
    cvec = lambda g: pl.BlockSpec((1, tc), lambda c, b: (0, g * cpb + c))
    in_specs = [pl.BlockSpec((rows, tc), lambda c, b: (zrb0 + b, z_col * cpb + c)),
                pl.BlockSpec((rows, tc), lambda c, b: (rb0 + b, g_col * cpb + c))]
    args = [z_src, u_hy]
    if conv_z:
        in_specs += [pl.BlockSpec((3, tc), lambda c, b: (0, z_col * cpb + c)), cvec(z_col)]
        args += [conv_w, conv_b]
    in_specs += [pl.BlockSpec((3, tc), lambda c, b: (0, g_col * cpb + c)), cvec(g_col),
                 pl.BlockSpec((1, tc), lambda c, b: (0, c)),
                 pl.BlockSpec((n, lb), lambda c, b: (0, 0)),
                 pl.BlockSpec((lb, n), lambda c, b: (0, 0))]
    args += [conv_w, conv_b, skip, fwd_mat, inv_mat]
    sspec = pl.BlockSpec((nd, lb, tc), lambda c, b: (0, 0, spec_col * cpb + c), pipeline_mode=pl.Buffered(1))
    in_specs += [sspec, sspec, sspec]
    args += [p_arr, q_arr, p2_arr]
    nblk = rows // lb
    return pl.pallas_call(
        functools.partial(_hyena_kernel, l=l, lb=lb, nseq=nseq, dmax=dmax, conv_z=conv_z),
        grid=(HY_DIM // tc, nb // nseq),
        in_specs=in_specs,
        out_specs=pl.BlockSpec((rows, tc), lambda c, b: (b, c)),
        out_shape=jax.ShapeDtypeStruct((nb * l, HY_DIM), out_dtype),
        scratch_shapes=[pltpu.VMEM((rows, tc), F32), pltpu.VMEM((rows, tc), F32),
                        pltpu.VMEM((rows, tc), BF16), pltpu.VMEM((nblk, n, tc), F32)],
        compiler_params=_cparams(("arbitrary", "arbitrary"), 56),
        name="hyena_stage",
    )(*args)


def _conformer_kernel(glu_ref, b_ref, lng_ref, lnb_ref, f_ref, fi_ref, p_ref, q_ref, p2_ref, o_ref, z_scr,
                      *, l, lb, nseq, dmax):
    bias, lng, lnb = b_ref[...], lng_ref[...], lnb_ref[...]

    def emit(r, y):
        yn = _layer_norm(y + bias, lng, lnb)
        o_ref[r * lb:(r + 1) * lb, :] = (yn * jax.nn.sigmoid(yn)).astype(o_ref.dtype)

    nbk = l // lb
    _block_conv(glu_ref, z_scr, f_ref, fi_ref, p_ref, q_ref, p2_ref, lb, nseq * nbk, nbk, dmax, emit)


def _conformer(glu, bias, ln_g, ln_b, mats, spec, tok_off, nb, l, nseq):
    fwd_mat, inv_mat, _ = mats
    n, lb = fwd_mat.shape
    nd = spec[0].shape[0]
    rows = nseq * l
    rb0 = tok_off // rows
    c = CV_DIM
    vec = pl.BlockSpec((1, c), lambda b: (0, 0))
    sspec = pl.BlockSpec((nd, lb, c), lambda b: (0, 0, 0))
    return pl.pallas_call(
        functools.partial(_conformer_kernel, l=l, lb=lb, nseq=nseq, dmax=nd // 2),
        grid=(nb // nseq,),
        in_specs=[pl.BlockSpec((rows, c), lambda b: (rb0 + b, 0)),
                  vec, vec, vec,
                  pl.BlockSpec((n, lb), lambda b: (0, 0)),
                  pl.BlockSpec((lb, n), lambda b: (0, 0)),
                  sspec, sspec, sspec],
        out_specs=pl.BlockSpec((rows, c), lambda b: (b, 0)),
        out_shape=jax.ShapeDtypeStruct((nb * l, c), BF16),
        scratch_shapes=[pltpu.VMEM((rows // lb, n, c), F32)],
        compiler_params=_cparams(("arbitrary",), 56),
        name="conformer",
    )(glu, bias, ln_g, ln_b, fwd_mat, inv_mat, *spec)


def _filter_kernel(z_ref, w1_ref, b1_ref, w2_ref, b2_ref, w3_ref, freq_ref, decay_ref, o_ref):
    hp = lax.Precision.HIGHEST
    z = z_ref[...]
    freq = freq_ref[...]
    hid = jnp.sin(freq * (jnp.dot(z, w1_ref[...], precision=hp, preferred_element_type=F32) + b1_ref[...]))
    hid = jnp.sin(freq * (jnp.dot(hid, w2_ref[...], precision=hp, preferred_element_type=F32) + b2_ref[...]))
    h = jnp.dot(hid, w3_ref[...], precision=hp, preferred_element_type=F32)
    h = h * jnp.exp(-z[:, 0:1] * jnp.abs(decay_ref[...]))
    row = lax.broadcasted_iota(jnp.int32, h.shape, 0) + pl.program_id(0) * h.shape[0]
    o_ref[...] = jnp.where(row == 0, 0.0, h).astype(o_ref.dtype)


def _hyena_filter(l, w1, b1, w2, b2, w3, freq, decay):
    t = np.abs(np.arange(2 * l) - l) / l
    ang = 2.0 * math.pi * t[:, None] * np.arange(1, HY_BANDS + 1)[None, :]
    z = jnp.asarray(np.concatenate([t[:, None], np.cos(ang), np.sin(ang), np.zeros((2 * l, LANES - HY_EMB))], axis=-1), F32)
    w1p = jnp.concatenate([w1, jnp.zeros((LANES - HY_EMB, HY_FILT_HID), F32)], axis=0)
    nout = w3.shape[1] // 2
    tl = 256
    n_bwd = l // tl
    full = lambda i: (0, 0)
    half = lambda i: (0, jnp.where(i < n_bwd, 1, 0))
    return pl.pallas_call(
        _filter_kernel,
        grid=(2 * l // tl,),
        in_specs=[pl.BlockSpec((tl, LANES), lambda i: (i, 0)),
                  pl.BlockSpec((LANES, HY_FILT_HID), full),
                  pl.BlockSpec((1, HY_FILT_HID), full),
                  pl.BlockSpec((HY_FILT_HID, HY_FILT_HID), full),
                  pl.BlockSpec((1, HY_FILT_HID), full),
                  pl.BlockSpec((HY_FILT_HID, nout), half),
                  pl.BlockSpec((1, HY_FILT_HID), full),
                  pl.BlockSpec((1, nout), half)],
        out_specs=pl.BlockSpec((tl, nout), lambda i: (i, 0)),
        out_shape=jax.ShapeDtypeStruct((2 * l, nout), BF16),
        compiler_params=_cparams(("arbitrary",), 32),
        name="hyena_filter",
    )(z, w1p, b1[None], w2, b2[None], w3, freq[None], decay[None])


def _out_proj_kernel(*refs, n_x, n_ctx_tiles):
    ya_refs, yb_refs, x_refs = refs[0:2], refs[2:4], refs[4:4 + n_x]
    mod_ref, wa_ref, wb_ref, g_ref, b_ref, o_ref = refs[4 + n_x:]
    for r in range(o_ref.shape[0] // SUB_ROWS):
        rows = slice(r * SUB_ROWS, (r + 1) * SUB_ROWS)
        y = (jnp.dot(_read_parts(ya_refs, n_ctx_tiles, rows), wa_ref[...], preferred_element_type=F32)
             + jnp.dot(_read_parts(yb_refs, n_ctx_tiles, rows), wb_ref[...], preferred_element_type=F32))
        z = ALPHA * _read_parts(x_refs, n_ctx_tiles, rows) + mod_ref[0, 2:3, :] * y
        o_ref[rows, :] = _layer_norm(z, g_ref[...], b_ref[...])


def _out_proj(tok, ya_parts, yb_parts, x_parts, mod, w_out, ln_g, ln_b, tm):
    d = x_parts[0].shape[1]
    ka, kb = ya_parts[0].shape[1], yb_parts[0].shape[1]
    cm = tok.cond_map(tm)
    full = lambda m: (0, 0)
    return pl.pallas_call(
        functools.partial(_out_proj_kernel, n_x=len(x_parts), n_ctx_tiles=tok.t_ctx // tm),
        grid=(tok.t // tm,),
        in_specs=(tok.part_specs(tm, 2, ka) + tok.part_specs(tm, 2, kb) + tok.part_specs(tm, len(x_parts), d) + [
            pl.BlockSpec((1, 6, d), lambda m: (cm(m), 0, 0)),
            pl.BlockSpec((ka, d), lambda m: (0, 0)),
            pl.BlockSpec((kb, d), lambda m: (ka // kb, 0)),
            pl.BlockSpec((1, d), full),
            pl.BlockSpec((1, d), full)]),
        out_specs=pl.BlockSpec((tm, d), lambda m: (m, 0)),
        out_shape=jax.ShapeDtypeStruct((tok.t, d), F32),
        compiler_params=_cparams(("arbitrary",), 40),
        name="out_proj_ln",
    )(*ya_parts, *yb_parts, *x_parts, mod, w_out, w_out, ln_g, ln_b)


MLP_CHUNK = 512


def _mlp_kernel(x_ref, mod_ref, w1_ref, b1_ref, w2_ref, b2_ref, g_ref, b_ref, o_ref, h_scr, acc_scr, *, nf):
    f = pl.program_id(1)

    @pl.when(f == 0)
    def _():
        h_scr[...] = (x_ref[...] * (1.0 + mod_ref[0, 4:5, :]) + mod_ref[0, 3:4, :]).astype(BF16)
        acc_scr[...] = jnp.zeros_like(acc_scr)

    part = None
    for c0 in range(0, w1_ref.shape[1], MLP_CHUNK):
        cols = slice(c0, c0 + MLP_CHUNK)
        a = jnp.maximum(jnp.dot(h_scr[...], w1_ref[:, cols].astype(BF16), preferred_element_type=F32)
                        + b1_ref[:, cols], 0.0)
        y = jnp.dot((a * a).astype(BF16), w2_ref[cols, :].astype(BF16), preferred_element_type=F32)
        part = y if part is None else part + y
    acc_scr[...] += part

    @pl.when(f == nf - 1)
    def _():
        z = ALPHA * x_ref[...] + mod_ref[0, 5:6, :] * (acc_scr[...] + b2_ref[...])
        o_ref[...] = _layer_norm(z, g_ref[...], b_ref[...])


def _mlp(tok, x, mod, layer, w1, b1, w2, b2, ln_g, ln_b, tm, tf, tok_off, n_rows):
    d = x.shape[1]
    dff = w1.shape[2]
    nf = dff // tf
    m0 = tok_off // tm
    cm = tok.cond_map(tm)
    return pl.pallas_call(
        functools.partial(_mlp_kernel, nf=nf),
        grid=(n_rows // tm, nf),
        in_specs=[pl.BlockSpec((tm, d), lambda m, f: (m0 + m, 0)),
                  pl.BlockSpec((1, 6, d), lambda m, f: (cm(m0 + m), 0, 0)),
                  pl.BlockSpec((None, d, tf), lambda m, f: (layer, 0, f)),
                  pl.BlockSpec((None, 1, tf), lambda m, f: (layer, 0, f)),
                  pl.BlockSpec((None, tf, d), lambda m, f: (layer, f, 0)),
                  pl.BlockSpec((None, 1, d), lambda m, f: (layer, 0, 0)),
                  pl.BlockSpec((1, d), lambda m, f: (0, 0)),
                  pl.BlockSpec((1, d), lambda m, f: (0, 0))],
        out_specs=pl.BlockSpec((tm, d), lambda m, f: (m, 0)),
        out_shape=jax.ShapeDtypeStruct((n_rows, d), F32),
        scratch_shapes=[pltpu.VMEM((tm, d), BF16), pltpu.VMEM((tm, d), F32)],
        compiler_params=_cparams(("arbitrary", "arbitrary"), 56),
        name="mlp_ln",
    )(x, mod, w1, b1, w2, b2, ln_g, ln_b)


def _mla_weights(w_uq, w_ukv):
    hd = MLA_NOPE + MLA_ROPE
    wq = w_uq.reshape(MLA_Q_RANK, MLA_HEADS, hd)
    wq = jnp.pad(wq, ((0, 0), (0, 0), (0, LANES - hd))).reshape(MLA_Q_RANK, MLA_HEADS * LANES)
    wkv = w_ukv.reshape(MLA_KV_RANK, MLA_HEADS, MLA_NOPE + MLA_V).transpose(1, 0, 2)
    w_nope, w_v = wkv[..., :MLA_NOPE], wkv[..., MLA_NOPE:]
    top = jnp.pad(w_nope, ((0, 0), (0, 0), (0, LANES - MLA_NOPE)))
    place = jnp.pad(jnp.eye(MLA_ROPE, dtype=F32), ((0, LANES - MLA_ROPE), (MLA_NOPE, LANES - MLA_NOPE - MLA_ROPE)))
    wk = jnp.concatenate([top, jnp.broadcast_to(place, (MLA_HEADS, LANES, LANES))], axis=1)
    wk = wk.reshape(MLA_HEADS // 2, 2, 2 * LANES, LANES).transpose(0, 2, 1, 3).reshape(MLA_HEADS // 2, 2 * LANES, 2 * LANES)
    w_v = w_v.reshape(MLA_HEADS // 2, 2, MLA_KV_RANK, MLA_V)
    wv = jnp.stack([jnp.pad(w_v[:, 0], ((0, 0), (0, 0), (0, LANES - MLA_V))),
                    jnp.pad(w_v[:, 1], ((0, 0), (0, 0), (LANES - MLA_V, 0)))], axis=1)
    wv = wv.reshape(MLA_HEADS, MLA_KV_RANK, LANES)
    return wq.astype(BF16), wk.astype(BF16), wv.astype(BF16)


def _conformer_taps(dw_w, l):
    k, c = dw_w.shape
    half = k // 2
    return jnp.pad(dw_w[::-1], ((l - half, l - (k - half)), (0, 0))).astype(BF16)


def kernel(x_prompt, x_sample, cache_mla_ckv, cache_mla_krope, cache_gqa_k, cache_gqa_v, c, c_ctx, ev_w_in, hy_conv_w, hy_conv_b, hy_filt_w1, hy_filt_b1, hy_filt_w2, hy_filt_b2, hy_filt_w3, hy_sin_freq, hy_decay, hy_skip, mla_q_norm_g, mla_w_uq, mla_kv_norm_g, mla_w_ukv, ev_w_out, od_w_in, cv_dw_w, cv_dw_b, cv_ln_g, cv_ln_b, gqa_q_norm_g, gqa_k_norm_g, od_w_out, ada_w, ada_b, ln_g, ln_b, mlp_w1, mlp_b1, mlp_w2, mlp_b2):
    bc, lc, d = x_prompt.shape
    bl, ll, _ = x_sample.shape
    tok = _Tok(bc, lc, bl, ll)
    assert lc & (lc - 1) == 0 and ll & (ll - 1) == 0 and tok.t_ctx % ll == 0 and bl < 8
    n_od = od_w_in.shape[0]
    tm_in = min(512, ll)
    tm_mlp = min(1024, ll)
    tf_mlp = 1024
    tq = 256
    tq_lat = min(256, ll)
    seq_ctx = max(1, 1024 // lc)
    tc_lat = 256 if ll > 1024 else HY_DIM
    cv_half = cv_dw_w.shape[1] // 2

    cond8 = jnp.concatenate([c_ctx[None], c, jnp.zeros((7 - bl, d), F32)], axis=0)
    mods = _ada(cond8, ada_w, ada_b).reshape(DEPTH, 8, 6, d)

    mla_scale = (MLA_NOPE + MLA_ROPE) ** -0.5 * LOG2E
    ev_tabs = (_rope_tables(ll, tm_in, 1, LANES, MLA_NOPE, MLA_ROPE, mla_scale)
               + _rope_tables(ll, tm_in, 1, LANES, 0, MLA_ROPE, 1.0))
    od_tabs = (_rope_tables(ll, tm_in, LANES // GQA_HD, GQA_HD, 0, GQA_HD, GQA_HD ** -0.5 * LOG2E)
               + _rope_tables(ll, tm_in, LANES // GQA_HD, GQA_HD, 0, GQA_HD, 1.0))
    passes = []
    for off, nb, l, nseq, tc in ((0, bc, lc, seq_ctx, HY_DIM), (tok.t_ctx, bl, ll, 1, tc_lat)):
        lb = min(l, CONV_BLOCK)
        passes.append((off, nb, l, nseq, tc, lb, _dft_mats(lb)))

    x_parts = (x_prompt.reshape(tok.t_ctx, d), x_sample.reshape(tok.t_lat, d))
    ckv_list, krope_list, k_list, v_list = [], [], [], []

    for layer in range(DEPTH):
        i = layer // 2
        mod = mods[layer]
        if layer % 2 == 0:
            w_in_p = jnp.pad(ev_w_in[i], ((0, 0), (0, EV_W - ev_w_in.shape[2]))).astype(BF16)
            wq, wk, wv = _mla_weights(mla_w_uq[i], mla_w_ukv[i])
            u_hy, q, ckr = _even_in(tok, x_parts, mod, w_in_p, mla_q_norm_g[i][None], wq, mla_kv_norm_g[i][None],
                                    ev_tabs, tm_in)
            ckv_list.append(ckr[:tok.t_ctx, :MLA_KV_RANK].reshape(bc, lc, MLA_KV_RANK))
            krope_list.append(ckr[:tok.t_ctx, MLA_KV_RANK:MLA_KV_RANK + MLA_ROPE].reshape(bc, lc, MLA_ROPE))
            yb_parts = (_mla_attn(q, ckr, wk, wv, 0, bc, lc, min(tq, lc), MLA_HEADS // 2),
                        _mla_attn(q, ckr, wk, wv, tok.t_ctx, bl, ll, tq_lat, 1,
                                  cache=(cache_mla_ckv, cache_mla_krope, i)))

            skip = hy_skip[i]
            conv_b = hy_conv_b[i][None]
            ya_parts = []
            for (off, nb, l, nseq, tc, lb, mats) in passes:
                k2 = _hyena_filter(l, hy_filt_w1[i], hy_filt_b1[i], hy_filt_w2[i], hy_filt_b2[i],
                                   hy_filt_w3[i], hy_sin_freq[i], hy_decay[i])
                spec = _filter_spectrum(mats, k2, l // lb - 1)
                z1 = _hyena_stage(u_hy, 2, u_hy, 0, hy_conv_w[i], conv_b, skip[0:1], mats, spec, 0,
                                  off, nb, l, nseq, tc, True, F32)
                ya_parts.append(_hyena_stage(z1, 0, u_hy, 1, hy_conv_w[i], conv_b, skip[1:2], mats, spec, 1,
                                             off, nb, l, nseq, tc, False, BF16))
            w_out = ev_w_out[i].astype(BF16)
        else:
            (x,) = x_parts
            gq = jnp.tile(gqa_q_norm_g[i], LANES // GQA_HD)[None]
            gk = jnp.tile(gqa_k_norm_g[i], GQA_KV_HEADS)[None]
            glu, q, kv = _odd_in(tok, x, mod, od_w_in[i].astype(BF16), gq, gk, od_tabs, tm_in)
            n_k = GQA_KV_HEADS * GQA_HD
            k_list.append(kv[:tok.t_ctx, :n_k].reshape(bc, lc, GQA_KV_HEADS, GQA_HD))
            v_list.append(kv[:tok.t_ctx, n_k:].reshape(bc, lc, GQA_KV_HEADS, GQA_HD))
            past = cache_gqa_k.shape[2]
            cache = (cache_gqa_k.reshape(bl, n_od, past, n_k), cache_gqa_v.reshape(bl, n_od, past, n_k), i)
            yb_parts = (_gqa_attn(q, kv, 0, bc, lc, min(tq, lc), GQA_HEADS // 2),
                        _gqa_attn(q, kv, tok.t_ctx, bl, ll, tq_lat, 1, cache=cache))

            ya_parts = []
            for (off, nb, l, nseq, tc, lb, mats) in passes:
                dmax = min(l // lb - 1, -(-cv_half // lb))
                spec = _filter_spectrum(mats, _conformer_taps(cv_dw_w[i], l), dmax)
                ya_parts.append(_conformer(glu, cv_dw_b[i][None], cv_ln_g[i][None], cv_ln_b[i][None], mats, spec,
                                           off, nb, l, nseq))
            w_out = od_w_out[i].astype(BF16)

        x = _out_proj(tok, ya_parts, yb_parts, x_parts, mod, w_out, ln_g[layer, 0][None], ln_b[layer, 0][None], tm_in)
        mlp_args = (mod, layer, mlp_w1, mlp_b1[:, None, :], mlp_w2, mlp_b2[:, None, :],
                    ln_g[layer, 1][None], ln_b[layer, 1][None], tm_mlp, tf_mlp)
        if layer < DEPTH - 1:
            x_parts = (_mlp(tok, x, *mlp_args, 0, tok.t),)
        else:
            y_prompt = _mlp(tok, x, *mlp_args, 0, tok.t_ctx).reshape(bc, lc, d)
            y_sample = _mlp(tok, x, *mlp_args, tok.t_ctx, tok.t_lat).reshape(bl, ll, d)

    return (y_prompt, y_sample, jnp.stack(ckv_list, axis=1), jnp.stack(krope_list, axis=1),
            jnp.stack(k_list, axis=1), jnp.stack(v_list, axis=1))
```

```python
import functools
import math

import jax
import jax.numpy as jnp
import numpy as np
from jax import lax
from jax.experimental import pallas as pl
from jax.experimental.pallas import tpu as pltpu

F32 = jnp.float32
BF16 = jnp.bfloat16

DEPTH = 4
GRID_W = 64
ALPHA = (2.0 * DEPTH) ** 0.25
LN_EPS = 1e-5
RMS_EPS = 1e-6
ROPE_THETA = 10000.0

HY_DIM = 512
HY_ORDER = 2
HY_BANDS = 16
HY_EMB = 2 * HY_BANDS + 1
HY_FILT_HID = 64

MLA_HEADS = 8
MLA_NOPE = 64
MLA_ROPE = 32
MLA_V = 64
MLA_Q_RANK = 256
MLA_KV_RANK = 128

CV_DIM = 512

GQA_HEADS = 8
GQA_KV_HEADS = 2
GQA_HD = 64

LANES = 128
CONV_BLOCK = 512
MIB = 2 ** 20


def _cparams(sem, vmem_mib):
    return pltpu.CompilerParams(dimension_semantics=sem, vmem_limit_bytes=vmem_mib * MIB)


def _layer_norm(z, g, b):
    mu = jnp.mean(z, axis=-1, keepdims=True)
    zc = z - mu
    var = jnp.mean(zc * zc, axis=-1, keepdims=True)
    return zc * lax.rsqrt(var + LN_EPS) * g + b


def _rms(x, g):
    return x * lax.rsqrt(jnp.mean(x * x, axis=-1, keepdims=True) + RMS_EPS) * g


def _seg_mean(sq, s_mat):
    hi = sq.astype(BF16)
    lo = (sq - hi.astype(F32)).astype(BF16)
    return (jnp.dot(hi, s_mat, preferred_element_type=F32)
            + jnp.dot(lo, s_mat, preferred_element_type=F32))


def _rope(x, cos, sin_signed, half):
    w = x.shape[1]
    lane = lax.broadcasted_iota(jnp.int32, x.shape, 1)
    first = jnp.bitwise_and(lane, 2 * half - 1) < half
    rot = jnp.where(first, pltpu.roll(x, w - half, 1), pltpu.roll(x, half, 1))
    return x * cos + rot * sin_signed


def _ada_kernel(c_ref, w_ref, b_ref, o_ref):
    c = c_ref[...]
    s = (c * jax.nn.sigmoid(c)).astype(BF16)
    o_ref[0] = jnp.dot(s, w_ref[0].astype(BF16), preferred_element_type=F32) + b_ref[0]


def _ada(cond8, ada_w, ada_b):
    depth, d, n = ada_w.shape
    tn = 1536
    return pl.pallas_call(
        _ada_kernel,
        grid=(depth, n // tn),
        in_specs=[pl.BlockSpec((8, d), lambda l, j: (0, 0)),
                  pl.BlockSpec((1, d, tn), lambda l, j: (l, 0, j)),
                  pl.BlockSpec((1, 1, tn), lambda l, j: (l, 0, j))],
        out_specs=pl.BlockSpec((1, 8, tn), lambda l, j: (l, 0, j)),
        out_shape=jax.ShapeDtypeStruct((depth, 8, n), F32),
        compiler_params=_cparams(("arbitrary", "arbitrary"), 32),
        name="ada_mod",
    )(cond8, ada_w, ada_b.reshape(depth, 1, n))


class _Tok:
    def __init__(self, bc, lc, bl, ll):
        self.bc, self.lc, self.bl, self.ll = bc, lc, bl, ll
        self.t_ctx = bc * lc
        self.t_lat = bl * ll
        self.t = self.t_ctx + self.t_lat

    def cond_map(self, tm):
        n_ctx, per_b = self.t_ctx // tm, self.ll // tm
        return lambda m: jnp.where(m < n_ctx, 0, 1 + (m - n_ctx) // per_b)

    def rope_map(self, tm):
        n_ctx, per_b = self.t_ctx // tm, self.ll // tm
        return lambda m: jnp.where(m < n_ctx, 0, 1 + (m - n_ctx) % per_b)

    def part_specs(self, tm, n_parts, width):
        if n_parts == 1:
            return [pl.BlockSpec((tm, width), lambda m: (m, 0))]
        n_ctx = self.t_ctx // tm
        return [pl.BlockSpec((tm, width), lambda m: (jnp.minimum(m, n_ctx - 1), 0)),
                pl.BlockSpec((tm, width), lambda m: (jnp.maximum(m - n_ctx, 0), 0))]


def _read_parts(refs, n_ctx_tiles, rows=slice(None)):
    if len(refs) == 1:
        return refs[0][rows, :]
    return jnp.where(pl.program_id(0) < n_ctx_tiles, refs[0][rows, :], refs[1][rows, :])


SUB_ROWS = 256


def _rope_tables(ll, tm, n_heads, head_w, rope_off, rope_dim, scale):
    half = rope_dim // 2
    quarter = half // 2
    t = np.arange(ll)
    row, col = t // GRID_W, t % GRID_W
    inv = ROPE_THETA ** (-np.arange(0, half, 2, dtype=np.float64) / half)
    ang_r = row[:, None] * inv[None, :]
    ang_c = col[:, None] * inv[None, :]
    ang = np.concatenate([ang_r, ang_r, ang_c, ang_c], axis=-1)
    sign = np.concatenate([-np.ones(quarter), np.ones(quarter)] * 2)
    pad = ((0, 0), (rope_off, head_w - rope_off - rope_dim))
    cos_l = np.tile(np.pad(np.cos(ang), pad, constant_values=1.0), (1, n_heads))
    sin_l = np.tile(np.pad(np.sin(ang) * sign, pad), (1, n_heads))
    w = n_heads * head_w
    cos = np.concatenate([np.ones((tm, w)), cos_l], axis=0) * scale
    sin = np.concatenate([np.zeros((tm, w)), sin_l], axis=0) * scale
    return jnp.asarray(cos, F32), jnp.asarray(sin, F32)


EV_W = 2048


def _even_in_kernel(*refs, n_x, n_ctx_tiles):
    x_refs = refs[:n_x]
    (mod_ref, w_ref, gq_ref, wuq_ref, gkv_ref, cq_ref, sq_ref, ck_ref, sk_ref,
     uhy_ref, q_ref, ckr_ref) = refs[n_x:]
    n_hy = 3 * HY_DIM
    o = n_hy + MLA_Q_RANK
    for r in range(uhy_ref.shape[0] // SUB_ROWS):
        rows = slice(r * SUB_ROWS, (r + 1) * SUB_ROWS)
        x = _read_parts(x_refs, n_ctx_tiles, rows)
        h = (x * (1.0 + mod_ref[0, 1:2, :]) + mod_ref[0, 0:1, :]).astype(BF16)
        u = jnp.dot(h, w_ref[...], preferred_element_type=F32)
        uhy_ref[rows, :] = u[:, :n_hy]
        cqn = _rms(u[:, n_hy:n_hy + MLA_Q_RANK], gq_ref[...])
        q = jnp.dot(cqn.astype(BF16), wuq_ref[...], preferred_element_type=F32)
        cq, sq = cq_ref[rows, :], sq_ref[rows, :]
        for hd in range(MLA_HEADS):
            cols = slice(hd * LANES, (hd + 1) * LANES)
            q_ref[rows, cols] = _rope(q[:, cols], cq, sq, MLA_ROPE // 4).astype(BF16)
        ckr_ref[rows, :MLA_KV_RANK] = _rms(u[:, o:o + MLA_KV_RANK], gkv_ref[...])
        ckr_ref[rows, MLA_KV_RANK:] = _rope(u[:, o + MLA_KV_RANK:], ck_ref[rows, :], sk_ref[rows, :], MLA_ROPE // 4)


def _even_in(tok, x_parts, mods, layer, w_in_p, gq, wuq_p, gkv, tabs, tm):
    cq, sq, ck, sk = tabs
    d = x_parts[0].shape[1]
    cm, rm = tok.cond_map(tm), tok.rope_map(tm)
    qw = MLA_HEADS * LANES
    row = lambda m: (m, 0)
    full = lambda m: (0, 0)
    tab = pl.BlockSpec((tm, LANES), lambda m: (rm(m), 0))
    return pl.pallas_call(
        functools.partial(_even_in_kernel, n_x=len(x_parts), n_ctx_tiles=tok.t_ctx // tm),
        grid=(tok.t // tm,),
        in_specs=tok.part_specs(tm, len(x_parts), d) + [
            pl.BlockSpec((None, 1, 6, d), lambda m: (layer, cm(m), 0, 0)),
            pl.BlockSpec((d, EV_W), full),
            pl.BlockSpec((1, MLA_Q_RANK), full),
            pl.BlockSpec((MLA_Q_RANK, qw), full),
            pl.BlockSpec((1, MLA_KV_RANK), full),
            tab, tab, tab, tab],
        out_specs=[pl.BlockSpec((tm, 3 * HY_DIM), row),
                   pl.BlockSpec((tm, qw), row),
                   pl.BlockSpec((tm, 2 * LANES), row)],
        out_shape=[jax.ShapeDtypeStruct((tok.t, 3 * HY_DIM), F32),
                   jax.ShapeDtypeStruct((tok.t, qw), BF16),
                   jax.ShapeDtypeStruct((tok.t, 2 * LANES), F32)],
        compiler_params=_cparams(("arbitrary",), 48),
        name="even_in",
    )(*x_parts, mods, w_in_p, gq, wuq_p, gkv, cq, sq, ck, sk)


def _odd_in_kernel(x_ref, mod_ref, w_ref, gq_ref, gk_ref, seg_ref, cq_ref, sq_ref, ck_ref, sk_ref,
                   glu_ref, q_ref, kv_ref):
    n_q = GQA_HEADS * GQA_HD
    n_k = GQA_KV_HEADS * GQA_HD
    o = 2 * CV_DIM + n_q
    seg = seg_ref[...]
    for r in range(glu_ref.shape[0] // SUB_ROWS):
        rows = slice(r * SUB_ROWS, (r + 1) * SUB_ROWS)
        h = (x_ref[rows, :] * (1.0 + mod_ref[0, 1:2, :]) + mod_ref[0, 0:1, :]).astype(BF16)
        u = jnp.dot(h, w_ref[...], preferred_element_type=F32)
        glu_ref[rows, :] = (u[:, :CV_DIM] * jax.nn.sigmoid(u[:, CV_DIM:2 * CV_DIM])).astype(BF16)
        gq, cq, sq = gq_ref[...], cq_ref[rows, :], sq_ref[rows, :]
        for j in range(n_q // LANES):
            q = u[:, 2 * CV_DIM + j * LANES:2 * CV_DIM + (j + 1) * LANES]
            qn = q * lax.rsqrt(_seg_mean(q * q, seg) + RMS_EPS) * gq
            q_ref[rows, j * LANES:(j + 1) * LANES] = _rope(qn, cq, sq, GQA_HD // 4).astype(BF16)
        k = u[:, o:o + n_k]
        kn = k * lax.rsqrt(_seg_mean(k * k, seg) + RMS_EPS) * gk_ref[...]
        kv_ref[rows, :n_k] = _rope(kn, ck_ref[rows, :], sk_ref[rows, :], GQA_HD // 4)
        kv_ref[rows, n_k:] = u[:, o + n_k:]


def _odd_in(tok, x, mods, layer, w_in, gq, gk, tabs, tm):
    cq, sq, ck, sk = tabs
    d, n = w_in.shape
    n_q = GQA_HEADS * GQA_HD
    n_k = GQA_KV_HEADS * GQA_HD
    cm, rm = tok.cond_map(tm), tok.rope_map(tm)
    i = np.arange(LANES)
    seg = jnp.asarray(np.where((i[:, None] // GQA_HD) == (i[None, :] // GQA_HD), 1.0 / GQA_HD, 0.0), BF16)
    row = lambda m: (m, 0)
    full = lambda m: (0, 0)
    tab = pl.BlockSpec((tm, LANES), lambda m: (rm(m), 0))
    return pl.pallas_call(
        _odd_in_kernel,
        grid=(tok.t // tm,),
        in_specs=[pl.BlockSpec((tm, d), row),
                  pl.BlockSpec((None, 1, 6, d), lambda m: (layer, cm(m), 0, 0)),
                  pl.BlockSpec((d, n), full),
                  pl.BlockSpec((1, LANES), full),
                  pl.BlockSpec((1, LANES), full),
                  pl.BlockSpec((LANES, LANES), full),
                  tab, tab, tab, tab],
        out_specs=[pl.BlockSpec((tm, CV_DIM), row),
                   pl.BlockSpec((tm, n_q), row),
                   pl.BlockSpec((tm, 2 * n_k), row)],
        out_shape=[jax.ShapeDtypeStruct((tok.t, CV_DIM), BF16),
                   jax.ShapeDtypeStruct((tok.t, n_q), BF16),
                   jax.ShapeDtypeStruct((tok.t, 2 * n_k), F32)],
        compiler_params=_cparams(("arbitrary",), 48),
        name="odd_in",
    )(x, mods, w_in, gq, gk, seg, cq, sq, ck, sk)


LOG2E = math.log2(math.e)
SUM_LANE = (LANES // 2, 0)
LONG_KEYS = 1024


def _softmax_pv(q, kt, v, sum_lane):
    s = jnp.dot(q, kt, preferred_element_type=F32)
    p = jnp.exp2(s - jnp.max(s, axis=-1, keepdims=True))
    pv = jnp.dot(p.astype(BF16), v, preferred_element_type=F32)
    if kt.shape[1] >= LONG_KEYS:
        return pv * (1.0 / pv[:, sum_lane:sum_lane + 1])
    return pv * (1.0 / jnp.sum(p, axis=-1, keepdims=True))


def _pair_out(o0, o1):
    lane = lax.broadcasted_iota(jnp.int32, o0.shape, 1)
    return jnp.where(lane < LANES // 2, o0, o1)


def _ones_lane(shape, lane_idx):
    return (lax.broadcasted_iota(jnp.int32, shape, 1) == lane_idx).astype(F32)


def _mla_attn_kernel(*refs, seg_lens, has_cache, pps):
    if has_cache:
        q_ref, cckv_ref, ckr_ref, own_ref, wk_ref, wv_ref, o_ref, ck_scr, k_scr, v_scr = refs
    else:
        q_ref, own_ref, wk_ref, wv_ref, o_ref, ck_scr, k_scr, v_scr = refs
    npair = MLA_HEADS // 2

    @pl.when((pl.program_id(1) == 0) & (pl.program_id(2) == 0))
    def _():
        off = 0
        if has_cache:
            n = seg_lens[0]
            ck_scr[0:n, :MLA_KV_RANK] = cckv_ref[0, 0].astype(BF16)
            ck_scr[0:n, MLA_KV_RANK:] = jnp.zeros((n, LANES), BF16)
            ck_scr[0:n, MLA_KV_RANK:MLA_KV_RANK + MLA_ROPE] = ckr_ref[0, 0].astype(BF16)
            off = n
        ck_scr[off:off + seg_lens[-1], :] = own_ref[...].astype(BF16)
        ck = ck_scr[...]
        for hp in range(npair):
            k_scr[hp] = lax.dot_general(wk_ref[hp], ck, (((1,), (1,)), ((), ())),
                                        preferred_element_type=F32).astype(BF16)
        for h in range(MLA_HEADS):
            v = jnp.dot(ck[:, :MLA_KV_RANK], wv_ref[h], preferred_element_type=F32)
            v_scr[h] = (v + _ones_lane(v.shape, SUM_LANE[h % 2])).astype(BF16)

    for lp in range(pps):
        gp = lp if pps == npair else pl.program_id(2) * pps + lp
        kp = k_scr[gp]
        q0 = q_ref[:, 2 * lp * LANES:(2 * lp + 1) * LANES]
        q1 = q_ref[:, (2 * lp + 1) * LANES:(2 * lp + 2) * LANES]
        o = _pair_out(_softmax_pv(q0, kp[:LANES, :], v_scr[2 * gp], SUM_LANE[0]),
                      _softmax_pv(q1, kp[LANES:, :], v_scr[2 * gp + 1], SUM_LANE[1]))
        o_ref[:, lp * LANES:(lp + 1) * LANES] = o.astype(o_ref.dtype)


def _mla_attn(q, ckr, wk, wv, tok_off, nb, lq, tq, pps, cache=None):
    nq = lq // tq
    qb0 = tok_off // tq
    ob0 = tok_off // lq
    npair = MLA_HEADS // 2
    seg_lens = (lq,) if cache is None else (cache[0].shape[2], lq)
    lk = sum(seg_lens)
    in_specs = [pl.BlockSpec((tq, 2 * LANES * pps), lambda b, i, p: (qb0 + b * nq + i, p))]
    args = [q]
    if cache is not None:
        cckv, ckr_c, layer = cache
        past = cckv.shape[2]
        in_specs += [pl.BlockSpec((1, 1, past, MLA_KV_RANK), lambda b, i, p: (b, layer, 0, 0)),
                     pl.BlockSpec((1, 1, past, MLA_ROPE), lambda b, i, p: (b, layer, 0, 0))]
        args += [cckv, ckr_c]
    in_specs += [pl.BlockSpec((lq, 2 * LANES), lambda b, i, p: (ob0 + b, 0)),
                 pl.BlockSpec((npair, 2 * LANES, 2 * LANES), lambda b, i, p: (0, 0, 0)),
                 pl.BlockSpec((MLA_HEADS, MLA_KV_RANK, LANES), lambda b, i, p: (0, 0, 0))]
    args += [ckr, wk, wv]
    return pl.pallas_call(
        functools.partial(_mla_attn_kernel, seg_lens=seg_lens, has_cache=cache is not None, pps=pps),
        grid=(nb, nq, npair // pps),
        in_specs=in_specs,
        out_specs=pl.BlockSpec((tq, LANES * pps), lambda b, i, p: (b * nq + i, p)),
        out_shape=jax.ShapeDtypeStruct((nb * lq, MLA_HEADS * MLA_V), BF16),
        scratch_shapes=[pltpu.VMEM((lk, 2 * LANES), BF16),
                        pltpu.VMEM((npair, 2 * LANES, lk), BF16),
                        pltpu.VMEM((MLA_HEADS, lk, LANES), BF16)],
        compiler_params=_cparams(("arbitrary", "arbitrary", "arbitrary"), 48),
        name="mla_attn_lat" if cache is not None else "mla_attn_ctx",
    )(*args)


def _gqa_attn_kernel(*refs, seg_lens, has_cache, pps):
    if has_cache:
        q_ref, ck_ref, cv_ref, own_ref, sel_ref, selt_ref, o_ref, kv_scr, k_scr, v_scr = refs
    else:
        q_ref, own_ref, sel_ref, selt_ref, o_ref, kv_scr, k_scr, v_scr = refs
    n_k = GQA_KV_HEADS * GQA_HD
    npair = GQA_HEADS // 2
    group_pairs = GQA_HEADS // GQA_KV_HEADS // 2

    @pl.when((pl.program_id(1) == 0) & (pl.program_id(2) == 0))
    def _():
        off = 0
        if has_cache:
            n = seg_lens[0]
            kv_scr[0:n, :n_k] = ck_ref[0, 0].astype(BF16)
            kv_scr[0:n, n_k:] = cv_ref[0, 0].astype(BF16)
            off = n
        kv_scr[off:off + seg_lens[-1], :] = own_ref[...].astype(BF16)
        kk = kv_scr[:, :n_k]
        vv = kv_scr[:, n_k:]
        for s in range(2 * GQA_KV_HEADS):
            k_scr[s] = lax.dot_general(selt_ref[s], kk, (((1,), (1,)), ((), ())),
                                       preferred_element_type=F32).astype(BF16)
            v = jnp.dot(vv, sel_ref[s], preferred_element_type=F32)
            v_scr[s] = (v + _ones_lane(v.shape, SUM_LANE[s % 2])).astype(BF16)

    for lp in range(pps):
        gp = lp if pps == npair else pl.program_id(2) * pps + lp
        kvh = gp // group_pairs
        q = q_ref[:, lp * LANES:(lp + 1) * LANES]
        o = _pair_out(_softmax_pv(q, k_scr[2 * kvh], v_scr[2 * kvh], SUM_LANE[0]),
                      _softmax_pv(q, k_scr[2 * kvh + 1], v_scr[2 * kvh + 1], SUM_LANE[1]))
        o_ref[:, lp * LANES:(lp + 1) * LANES] = o.astype(o_ref.dtype)


def _gqa_attn(q, kv, tok_off, nb, lq, tq, pps, cache=None):
    nq = lq // tq
    qb0 = tok_off // tq
    ob0 = tok_off // lq
    npair = GQA_HEADS // 2
    n_k = GQA_KV_HEADS * GQA_HD
    seg_lens = (lq,) if cache is None else (cache[0].shape[2], lq)
    lk = sum(seg_lens)
    src = np.arange(n_k)[:, None]
    dst = np.arange(n_k)[None, :]
    sel_np = np.stack([np.where((src // GQA_HD == kvh) & (dst // GQA_HD == i) & (src % GQA_HD == dst % GQA_HD), 1.0, 0.0)
                       for kvh in range(GQA_KV_HEADS) for i in range(2)])
    sel, sel_t = jnp.asarray(sel_np, BF16), jnp.asarray(sel_np.transpose(0, 2, 1), BF16)
    in_specs = [pl.BlockSpec((tq, LANES * pps), lambda b, i, p: (qb0 + b * nq + i, p))]
    args = [q]
    if cache is not None:
        ck, cv, layer = cache
        past = ck.shape[2]
        in_specs += [pl.BlockSpec((1, 1, past, n_k), lambda b, i, p: (b, layer, 0, 0)),
                     pl.BlockSpec((1, 1, past, n_k), lambda b, i, p: (b, layer, 0, 0))]
        args += [ck, cv]
    selspec = pl.BlockSpec((2 * GQA_KV_HEADS, n_k, n_k), lambda b, i, p: (0, 0, 0))
    in_specs += [pl.BlockSpec((lq, 2 * n_k), lambda b, i, p: (ob0 + b, 0)), selspec, selspec]
    args += [kv, sel, sel_t]
    return pl.pallas_call(
        functools.partial(_gqa_attn_kernel, seg_lens=seg_lens, has_cache=cache is not None, pps=pps),
        grid=(nb, nq, npair // pps),
        in_specs=in_specs,
        out_specs=pl.BlockSpec((tq, LANES * pps), lambda b, i, p: (b * nq + i, p)),
        out_shape=jax.ShapeDtypeStruct((nb * lq, GQA_HEADS * GQA_HD), BF16),
        scratch_shapes=[pltpu.VMEM((lk, 2 * n_k), BF16),
                        pltpu.VMEM((2 * GQA_KV_HEADS, n_k, lk), BF16),
                        pltpu.VMEM((2 * GQA_KV_HEADS, lk, n_k), BF16)],
        compiler_params=_cparams(("arbitrary", "arbitrary", "arbitrary"), 48),
        name="gqa_attn_lat" if cache is not None else "gqa_attn_ctx",
    )(*args)


def _dft_mats(lb):
    n = 2 * lb
    r = np.arange(n)
    nyq = r == lb
    f = np.where(nyq, lb, r % lb)
    is_im = (r >= lb) & ~nyq
    s = np.arange(lb)

    def mat(pos):
        ang = ((f[:, None] * pos[None, :]) % n) * (2.0 * math.pi / n)
        return np.where(is_im[:, None], -np.sin(ang), np.cos(ang))

    fwd = mat(s)
    fwd_rev = np.where(s[None, :] == 0, 0.0, mat(lb - s))
    wgt = np.where((r == 0) | nyq, 1.0 / n, 2.0 / n)
    inv = (fwd * wgt[:, None]).T
    return tuple(jnp.asarray(m.astype(np.float32)).astype(BF16) for m in (fwd, inv, fwd_rev))


def _spec_kernel(f_ref, fr_ref, ka_ref, kb_ref, p_ref, q_ref, p2_ref, *, lb):
    tf = jnp.dot(f_ref[...], ka_ref[...], preferred_element_type=F32)
    tb = jnp.dot(fr_ref[...], kb_ref[...], preferred_element_type=F32)
    re = tf[:lb] + tb[:lb]
    im = tf[lb:] - tb[lb:]
    nyq = tf[lb:] + tb[lb:]
    row0 = lax.broadcasted_iota(jnp.int32, re.shape, 0) == 0
    p_ref[0] = re
    q_ref[0] = jnp.where(row0, 0.0, im)
    p2_ref[0] = jnp.where(row0, nyq, re)


def _filter_spectrum(mats, k2, dmax):
    fwd_mat, _, fwd_rev = mats
    n, lb = fwd_mat.shape
    cw = k2.shape[1]
    nbk = k2.shape[0] // n
    nd = 2 * dmax + 1
    tc = 512
    out = jax.ShapeDtypeStruct((nd, lb, cw), F32)
    blk = pl.BlockSpec((1, lb, tc), lambda c, d: (d, 0, c))
    mat = pl.BlockSpec((n, lb), lambda c, d: (0, 0))
    return pl.pallas_call(
        functools.partial(_spec_kernel, lb=lb),
        grid=(cw // tc, nd),
        in_specs=[mat, mat,
                  pl.BlockSpec((lb, tc), lambda c, d: (nbk + d - dmax, c)),
                  pl.BlockSpec((lb, tc), lambda c, d: (nbk + d - dmax - 1, c))],
        out_specs=[blk, blk, blk],
        out_shape=[out, out, out],
        compiler_params=_cparams(("arbitrary", "arbitrary"), 32),
        name="filter_spectrum",
    )(fwd_mat, fwd_rev, k2, k2)


def _block_conv(zbf_ref, z_scr, f_ref, fi_ref, p_ref, q_ref, p2_ref, lb, nblk, nbk, dmax, emit):
    for r in range(nblk):
        z_scr[r] = jnp.dot(f_ref[...], zbf_ref[r * lb:(r + 1) * lb, :], preferred_element_type=F32)
    for r in range(nblk):
        s, i = divmod(r, nbk)
        ya = yb = None
        for j in range(nbk):
            d = i - j
            if abs(d) > dmax:
                continue
            re = z_scr[s * nbk + j, :lb, :]
            im = z_scr[s * nbk + j, lb:, :]
            p, q, p2 = p_ref[d + dmax], q_ref[d + dmax], p2_ref[d + dmax]
            ta = re * p - im * q
            tb = re * q + im * p2
            ya = ta if ya is None else ya + ta
            yb = tb if yb is None else yb + tb
        y = jnp.concatenate([ya, yb], axis=0).astype(BF16)
        emit(r, jnp.dot(fi_ref[...], y, preferred_element_type=F32))


def _short_conv(u, w_ref, b_ref, l):
    rows = u.shape[0]
    t = jnp.bitwise_and(lax.broadcasted_iota(jnp.int32, u.shape, 0), l - 1)
    prev = jnp.where(t == 0, 0.0, pltpu.roll(u, 1, 0))
    nxt = jnp.where(t == l - 1, 0.0, pltpu.roll(u, rows - 1, 0))
    return w_ref[0:1, :] * prev + w_ref[1:2, :] * u + w_ref[2:3, :] * nxt + b_ref[...]


def _hyena_kernel(*refs, l, lb, nseq, dmax, conv_z):
    if conv_z:
        (uz_ref, ug_ref, cwz_ref, cbz_ref, cwg_ref, cbg_ref, skip_ref,
         f_ref, fi_ref, p_ref, q_ref, p2_ref, o_ref, zf_scr, g_scr, zbf_scr, z_scr) = refs
    else:
        (uz_ref, ug_ref, cwg_ref, cbg_ref, skip_ref,
         f_ref, fi_ref, p_ref, q_ref, p2_ref, o_ref, zf_scr, g_scr, zbf_scr, z_scr) = refs
    z = _short_conv(uz_ref[...], cwz_ref, cbz_ref, l) if conv_z else uz_ref[...]
    zf_scr[...] = z
    zbf_scr[...] = z.astype(BF16)
    g_scr[...] = _short_conv(ug_ref[...], cwg_ref, cbg_ref, l)
    skip = skip_ref[...]

    def emit(r, y):
        rows = slice(r * lb, (r + 1) * lb)
        o_ref[rows, :] = (g_scr[rows, :] * (y + skip * zf_scr[rows, :])).astype(o_ref.dtype)

    nbk = l // lb
    _block_conv(zbf_scr, z_scr, f_ref, fi_ref, p_ref, q_ref, p2_ref, lb, nseq * nbk, nbk, dmax, emit)


def _hyena_stage(z_src, z_col, u_hy, g_col, conv_w, conv_b, skip, mats, spec, spec_col,
                 tok_off, nb, l, nseq, tc, conv_z, out_dtype):
    fwd_mat, inv_mat, _ = mats
    p_arr, q_arr, p2_arr = spec
    n, lb = fwd_mat.shape
    nd = p_arr.shape[0]
    dmax = nd // 2
    rows = nseq * l
    rb0 = tok_off // rows
    zrb0 = rb0 if conv_z else 0
    cpb = HY_DIM // tc
    cvec = lambda g: pl.BlockSpec((1, tc), lambda c, b: (0, g * cpb + c))
    in_specs = [pl.BlockSpec((rows, tc), lambda c, b: (zrb0 + b, z_col * cpb + c)),
                pl.BlockSpec((rows, tc), lambda c, b: (rb0 + b, g_col * cpb + c))]
    args = [z_src, u_hy]
    if conv_z:
        in_specs += [pl.BlockSpec((3, tc), lambda c, b: (0, z_col * cpb + c)), cvec(z_col)]
        args += [conv_w, conv_b]
    in_specs += [pl.BlockSpec((3, tc), lambda c, b: (0, g_col * cpb + c)), cvec(g_col),
                 pl.BlockSpec((1, tc), lambda c, b: (0, c)),
                 pl.BlockSpec((n, lb), lambda c, b: (0, 0)),
                 pl.BlockSpec((lb, n), lambda c, b: (0, 0))]
    args += [conv_w, conv_b, skip, fwd_mat, inv_mat]
    sspec = pl.BlockSpec((nd, lb, tc), lambda c, b: (0, 0, spec_col * cpb + c), pipeline_mode=pl.Buffered(1))
    in_specs += [sspec, sspec, sspec]
    args += [p_arr, q_arr, p2_arr]
    nblk = rows // lb
    return pl.pallas_call(
        functools.partial(_hyena_kernel, l=l, lb=lb, nseq=nseq, dmax=dmax, conv_z=conv_z),
        grid=(HY_DIM // tc, nb // nseq),
        in_specs=in_specs,
        out_specs=pl.BlockSpec((rows, tc), lambda c, b: (b, c)),
        out_shape=jax.ShapeDtypeStruct((nb * l, HY_DIM), out_dtype),
        scratch_shapes=[pltpu.VMEM((rows, tc), F32), pltpu.VMEM((rows, tc), F32),
                        pltpu.VMEM((rows, tc), BF16), pltpu.VMEM((nblk, n, tc), F32)],
        compiler_params=_cparams(("arbitrary", "arbitrary"), 56),
        name="hyena_stage",
    )(*args)


def _conformer_kernel(glu_ref, b_ref, lng_ref, lnb_ref, f_ref, fi_ref, p_ref, q_ref, p2_ref, o_ref, z_scr,
                      *, l, lb, nseq, dmax):
    bias, lng, lnb = b_ref[...], lng_ref[...], lnb_ref[...]

    def emit(r, y):
        yn = _layer_norm(y + bias, lng, lnb)
        o_ref[r * lb:(r + 1) * lb, :] = (yn * jax.nn.sigmoid(yn)).astype(o_ref.dtype)

    nbk = l // lb
    _block_conv(glu_ref, z_scr, f_ref, fi_ref, p_ref, q_ref, p2_ref, lb, nseq * nbk, nbk, dmax, emit)


def _conformer(glu, bias, ln_g, ln_b, mats, spec, tok_off, nb, l, nseq):
    fwd_mat, inv_mat, _ = mats
    n, lb = fwd_mat.shape
    nd = spec[0].shape[0]
    rows = nseq * l
    rb0 = tok_off // rows
    c = CV_DIM
    vec = pl.BlockSpec((1, c), lambda b: (0, 0))
    sspec = pl.BlockSpec((nd, lb, c), lambda b: (0, 0, 0))
    return pl.pallas_call(
        functools.partial(_conformer_kernel, l=l, lb=lb, nseq=nseq, dmax=nd // 2),
        grid=(nb // nseq,),
        in_specs=[pl.BlockSpec((rows, c), lambda b: (rb0 + b, 0)),
                  vec, vec, vec,
                  pl.BlockSpec((n, lb), lambda b: (0, 0)),
                  pl.BlockSpec((lb, n), lambda b: (0, 0)),
                  sspec, sspec, sspec],
        out_specs=pl.BlockSpec((rows, c), lambda b: (b, 0)),
        out_shape=jax.ShapeDtypeStruct((nb * l, c), BF16),
        scratch_shapes=[pltpu.VMEM((rows // lb, n, c), F32)],
        compiler_params=_cparams(("arbitrary",), 56),
        name="conformer",
    )(glu, bias, ln_g, ln_b, fwd_mat, inv_mat, *spec)


def _dot_bf16x3(a, b):
    ah = a.astype(BF16)
    al = (a - ah.astype(F32)).astype(BF16)
    bh = b.astype(BF16)
    bl = (b - bh.astype(F32)).astype(BF16)
    return (jnp.dot(ah, bh, preferred_element_type=F32) + jnp.dot(al, bh, preferred_element_type=F32)
            + jnp.dot(ah, bl, preferred_element_type=F32))


def _filter_kernel(z_ref, w1_ref, b1_ref, w2_ref, b2_ref, w3_ref, freq_ref, decay_ref, o_ref):
    hp = lax.Precision.HIGHEST
    z = z_ref[...]
    freq = freq_ref[...]
    hid = jnp.sin(freq * (jnp.dot(z, w1_ref[...], precision=hp, preferred_element_type=F32) + b1_ref[...]))
    hid = jnp.sin(freq * (jnp.dot(hid, w2_ref[...], precision=hp, preferred_element_type=F32) + b2_ref[...]))
    h = _dot_bf16x3(hid, w3_ref[...])
    h = h * jnp.exp(-z[:, 0:1] * jnp.abs(decay_ref[...]))
    row = lax.broadcasted_iota(jnp.int32, h.shape, 0) + pl.program_id(0) * h.shape[0]
    o_ref[...] = jnp.where(row == 0, 0.0, h).astype(o_ref.dtype)


def _hyena_filter(l, w1, b1, w2, b2, w3, freq, decay):
    t = np.abs(np.arange(2 * l) - l) / l
    ang = 2.0 * math.pi * t[:, None] * np.arange(1, HY_BANDS + 1)[None, :]
    z = jnp.asarray(np.concatenate([t[:, None], np.cos(ang), np.sin(ang), np.zeros((2 * l, LANES - HY_EMB))], axis=-1), F32)
    w1p = jnp.concatenate([w1, jnp.zeros((LANES - HY_EMB, HY_FILT_HID), F32)], axis=0)
    nout = w3.shape[1] // 2
    tl = 256
    n_bwd = l // tl
    full = lambda i: (0, 0)
    half = lambda i: (0, jnp.where(i < n_bwd, 1, 0))
    return pl.pallas_call(
        _filter_kernel,
        grid=(2 * l // tl,),
        in_specs=[pl.BlockSpec((tl, LANES), lambda i: (i, 0)),
                  pl.BlockSpec((LANES, HY_FILT_HID), full),
                  pl.BlockSpec((1, HY_FILT_HID), full),
                  pl.BlockSpec((HY_FILT_HID, HY_FILT_HID), full),
                  pl.BlockSpec((1, HY_FILT_HID), full),
                  pl.BlockSpec((HY_FILT_HID, nout), half),
                  pl.BlockSpec((1, HY_FILT_HID), full),
                  pl.BlockSpec((1, nout), half)],
        out_specs=pl.BlockSpec((tl, nout), lambda i: (i, 0)),
        out_shape=jax.ShapeDtypeStruct((2 * l, nout), BF16),
        compiler_params=_cparams(("arbitrary",), 32),
        name="hyena_filter",
    )(z, w1p, b1[None], w2, b2[None], w3, freq[None], decay[None])


def _out_proj_kernel(*refs, n_x, n_ctx_tiles):
    ya_refs, yb_refs, x_refs = refs[0:2], refs[2:4], refs[4:4 + n_x]
    mod_ref, wa_ref, wb_ref, g_ref, b_ref, o_ref = refs[4 + n_x:]
    for r in range(o_ref.shape[0] // SUB_ROWS):
        rows = slice(r * SUB_ROWS, (r + 1) * SUB_ROWS)
        y = (jnp.dot(_read_parts(ya_refs, n_ctx_tiles, rows), wa_ref[...], preferred_element_type=F32)
             + jnp.dot(_read_parts(yb_refs, n_ctx_tiles, rows), wb_ref[...], preferred_element_type=F32))
        z = ALPHA * _read_parts(x_refs, n_ctx_tiles, rows) + mod_ref[0, 2:3, :] * y
        o_ref[rows, :] = _layer_norm(z, g_ref[...], b_ref[...])


def _out_proj(tok, ya_parts, yb_parts, x_parts, mods, layer, w_out, ln_g, ln_b, tm):
    d = x_parts[0].shape[1]
    ka, kb = ya_parts[0].shape[1], yb_parts[0].shape[1]
    cm = tok.cond_map(tm)
    return pl.pallas_call(
        functools.partial(_out_proj_kernel, n_x=len(x_parts), n_ctx_tiles=tok.t_ctx // tm),
        grid=(tok.t // tm,),
        in_specs=(tok.part_specs(tm, 2, ka) + tok.part_specs(tm, 2, kb) + tok.part_specs(tm, len(x_parts), d) + [
            pl.BlockSpec((None, 1, 6, d), lambda m: (layer, cm(m), 0, 0)),
            pl.BlockSpec((ka, d), lambda m: (0, 0)),
            pl.BlockSpec((kb, d), lambda m: (ka // kb, 0)),
            pl.BlockSpec((None, 1, d), lambda m: (2 * layer, 0, 0)),
            pl.BlockSpec((None, 1, d), lambda m: (2 * layer, 0, 0))]),
        out_specs=pl.BlockSpec((tm, d), lambda m: (m, 0)),
        out_shape=jax.ShapeDtypeStruct((tok.t, d), F32),
        compiler_params=_cparams(("arbitrary",), 40),
        name="out_proj_ln",
    )(*ya_parts, *yb_parts, *x_parts, mods, w_out, w_out, ln_g, ln_b)


MLP_CHUNK = 512


def _mlp_kernel(x_ref, mod_ref, w1_ref, b1_ref, w2_ref, b2_ref, g_ref, b_ref, o_ref, h_scr, acc_scr, *, nf):
    f = pl.program_id(1)

    @pl.when(f == 0)
    def _():
        h_scr[...] = (x_ref[...] * (1.0 + mod_ref[0, 4:5, :]) + mod_ref[0, 3:4, :]).astype(BF16)
        acc_scr[...] = jnp.zeros_like(acc_scr)

    part = None
    for c0 in range(0, w1_ref.shape[1], MLP_CHUNK):
        cols = slice(c0, c0 + MLP_CHUNK)
        a = jnp.maximum(jnp.dot(h_scr[...], w1_ref[:, cols].astype(BF16), preferred_element_type=F32)
                        + b1_ref[:, cols], 0.0)
        y = jnp.dot((a * a).astype(BF16), w2_ref[cols, :].astype(BF16), preferred_element_type=F32)
        part = y if part is None else part + y
    acc_scr[...] += part

    @pl.when(f == nf - 1)
    def _():
        z = ALPHA * x_ref[...] + mod_ref[0, 5:6, :] * (acc_scr[...] + b2_ref[...])
        o_ref[...] = _layer_norm(z, g_ref[...], b_ref[...])


def _mlp(tok, x, mods, layer, w1, b1, w2, b2, ln_g, ln_b, tm, tf, tok_off, n_rows):
    d = x.shape[1]
    dff = w1.shape[2]
    nf = dff // tf
    m0 = tok_off // tm
    cm = tok.cond_map(tm)
    return pl.pallas_call(
        functools.partial(_mlp_kernel, nf=nf),
        grid=(n_rows // tm, nf),
        in_specs=[pl.BlockSpec((tm, d), lambda m, f: (m0 + m, 0)),
                  pl.BlockSpec((None, 1, 6, d), lambda m, f: (layer, cm(m0 + m), 0, 0)),
                  pl.BlockSpec((None, d, tf), lambda m, f: (layer, 0, f)),
                  pl.BlockSpec((None, 1, tf), lambda m, f: (layer, 0, f)),
                  pl.BlockSpec((None, tf, d), lambda m, f: (layer, f, 0)),
                  pl.BlockSpec((None, 1, d), lambda m, f: (layer, 0, 0)),
                  pl.BlockSpec((None, 1, d), lambda m, f: (2 * layer + 1, 0, 0)),
                  pl.BlockSpec((None, 1, d), lambda m, f: (2 * layer + 1, 0, 0))],
        out_specs=pl.BlockSpec((tm, d), lambda m, f: (m, 0)),
        out_shape=jax.ShapeDtypeStruct((n_rows, d), F32),
        scratch_shapes=[pltpu.VMEM((tm, d), BF16), pltpu.VMEM((tm, d), F32)],
        compiler_params=_cparams(("arbitrary", "arbitrary"), 56),
        name="mlp_ln",
    )(x, mods, w1, b1, w2, b2, ln_g, ln_b)


def _mla_weights(w_uq, w_ukv):
    hd = MLA_NOPE + MLA_ROPE
    wq = w_uq.reshape(MLA_Q_RANK, MLA_HEADS, hd)
    wq = jnp.pad(wq, ((0, 0), (0, 0), (0, LANES - hd))).reshape(MLA_Q_RANK, MLA_HEADS * LANES)
    wkv = w_ukv.reshape(MLA_KV_RANK, MLA_HEADS, MLA_NOPE + MLA_V).transpose(1, 0, 2)
    w_nope, w_v = wkv[..., :MLA_NOPE], wkv[..., MLA_NOPE:]
    top = jnp.pad(w_nope, ((0, 0), (0, 0), (0, LANES - MLA_NOPE)))
    place = jnp.pad(jnp.eye(MLA_ROPE, dtype=F32), ((0, LANES - MLA_ROPE), (MLA_NOPE, LANES - MLA_NOPE - MLA_ROPE)))
    wk = jnp.concatenate([top, jnp.broadcast_to(place, (MLA_HEADS, LANES, LANES))], axis=1)
    wk = wk.transpose(0, 2, 1).reshape(MLA_HEADS // 2, 2 * LANES, 2 * LANES)
    w_v = w_v.reshape(MLA_HEADS // 2, 2, MLA_KV_RANK, MLA_V)
    wv = jnp.stack([jnp.pad(w_v[:, 0], ((0, 0), (0, 0), (0, LANES - MLA_V))),
                    jnp.pad(w_v[:, 1], ((0, 0), (0, 0), (LANES - MLA_V, 0)))], axis=1)
    wv = wv.reshape(MLA_HEADS, MLA_KV_RANK, LANES)
    return wq.astype(BF16), wk.astype(BF16), wv.astype(BF16)


def _conformer_taps(dw_w, l):
    k, c = dw_w.shape
    half = k // 2
    return jnp.pad(dw_w[::-1], ((l - half, l - (k - half)), (0, 0))).astype(BF16)


def kernel(x_prompt, x_sample, cache_mla_ckv, cache_mla_krope, cache_gqa_k, cache_gqa_v, c, c_ctx, ev_w_in, hy_conv_w, hy_conv_b, hy_filt_w1, hy_filt_b1, hy_filt_w2, hy_filt_b2, hy_filt_w3, hy_sin_freq, hy_decay, hy_skip, mla_q_norm_g, mla_w_uq, mla_kv_norm_g, mla_w_ukv, ev_w_out, od_w_in, cv_dw_w, cv_dw_b, cv_ln_g, cv_ln_b, gqa_q_norm_g, gqa_k_norm_g, od_w_out, ada_w, ada_b, ln_g, ln_b, mlp_w1, mlp_b1, mlp_w2, mlp_b2):
    bc, lc, d = x_prompt.shape
    bl, ll, _ = x_sample.shape
    tok = _Tok(bc, lc, bl, ll)
    assert lc & (lc - 1) == 0 and ll & (ll - 1) == 0 and tok.t_ctx % ll == 0 and bl < 8
    n_od = od_w_in.shape[0]
    tm_in = min(512, ll)
    tm_mlp = min(1024, ll)
    tf_mlp = 1024
    tq = 256
    tq_lat = min(256, ll)
    seq_ctx = max(1, 1024 // lc)
    tc_lat = 256 if ll > 1024 else HY_DIM
    cv_half = cv_dw_w.shape[1] // 2

    cond8 = jnp.concatenate([c_ctx[None], c, jnp.zeros((7 - bl, d), F32)], axis=0)
    mods = _ada(cond8, ada_w, ada_b).reshape(DEPTH, 8, 6, d)
    ln_g2 = ln_g.reshape(2 * DEPTH, 1, d)
    ln_b2 = ln_b.reshape(2 * DEPTH, 1, d)

    mla_scale = (MLA_NOPE + MLA_ROPE) ** -0.5 * LOG2E
    ev_tabs = (_rope_tables(ll, tm_in, 1, LANES, MLA_NOPE, MLA_ROPE, mla_scale)
               + _rope_tables(ll, tm_in, 1, LANES, 0, MLA_ROPE, 1.0))
    od_tabs = (_rope_tables(ll, tm_in, LANES // GQA_HD, GQA_HD, 0, GQA_HD, GQA_HD ** -0.5 * LOG2E)
               + _rope_tables(ll, tm_in, LANES // GQA_HD, GQA_HD, 0, GQA_HD, 1.0))
    passes = []
    for off, nb, l, nseq, tc in ((0, bc, lc, seq_ctx, HY_DIM), (tok.t_ctx, bl, ll, 1, tc_lat)):
        lb = min(l, CONV_BLOCK)
        passes.append((off, nb, l, nseq, tc, lb, _dft_mats(lb)))

    x_parts = (x_prompt.reshape(tok.t_ctx, d), x_sample.reshape(tok.t_lat, d))
    ckv_list, krope_list, k_list, v_list = [], [], [], []

    for layer in range(DEPTH):
        i = layer // 2
        if layer % 2 == 0:
            w_in_p = jnp.pad(ev_w_in[i], ((0, 0), (0, EV_W - ev_w_in.shape[2]))).astype(BF16)
            wq, wk, wv = _mla_weights(mla_w_uq[i], mla_w_ukv[i])
            u_hy, q, ckr = _even_in(tok, x_parts, mods, layer, w_in_p, mla_q_norm_g[i][None], wq, mla_kv_norm_g[i][None],
                                    ev_tabs, tm_in)
            ckv_list.append(ckr[:tok.t_ctx, :MLA_KV_RANK].reshape(bc, lc, MLA_KV_RANK))
            krope_list.append(ckr[:tok.t_ctx, MLA_KV_RANK:MLA_KV_RANK + MLA_ROPE].reshape(bc, lc, MLA_ROPE))
            yb_parts = (_mla_attn(q, ckr, wk, wv, 0, bc, lc, min(tq, lc), MLA_HEADS // 2),
                        _mla_attn(q, ckr, wk, wv, tok.t_ctx, bl, ll, tq_lat, 1,
                                  cache=(cache_mla_ckv, cache_mla_krope, i)))

            skip = hy_skip[i]
            conv_b = hy_conv_b[i][None]
            ya_parts = []
            for (off, nb, l, nseq, tc, lb, mats) in passes:
                k2 = _hyena_filter(l, hy_filt_w1[i], hy_filt_b1[i], hy_filt_w2[i], hy_filt_b2[i],
                                   hy_filt_w3[i], hy_sin_freq[i], hy_decay[i])
                spec = _filter_spectrum(mats, k2, l // lb - 1)
                z1 = _hyena_stage(u_hy, 2, u_hy, 0, hy_conv_w[i], conv_b, skip[0:1], mats, spec, 0,
                                  off, nb, l, nseq, tc, True, F32)
                ya_parts.append(_hyena_stage(z1, 0, u_hy, 1, hy_conv_w[i], conv_b, skip[1:2], mats, spec, 1,
                                             off, nb, l, nseq, tc, False, BF16))
            w_out = ev_w_out[i].astype(BF16)
        else:
            (x,) = x_parts
            gq = jnp.tile(gqa_q_norm_g[i], LANES // GQA_HD)[None]
            gk = jnp.tile(gqa_k_norm_g[i], GQA_KV_HEADS)[None]
            glu, q, kv = _odd_in(tok, x, mods, layer, od_w_in[i].astype(BF16), gq, gk, od_tabs, tm_in)
            n_k = GQA_KV_HEADS * GQA_HD
            k_list.append(kv[:tok.t_ctx, :n_k].reshape(bc, lc, GQA_KV_HEADS, GQA_HD))
            v_list.append(kv[:tok.t_ctx, n_k:].reshape(bc, lc, GQA_KV_HEADS, GQA_HD))
            past = cache_gqa_k.shape[2]
            cache = (cache_gqa_k.reshape(bl, n_od, past, n_k), cache_gqa_v.reshape(bl, n_od, past, n_k), i)
            yb_parts = (_gqa_attn(q, kv, 0, bc, lc, min(tq, lc), GQA_HEADS // 2),
                        _gqa_attn(q, kv, tok.t_ctx, bl, ll, tq_lat, 1, cache=cache))

            ya_parts = []
            for (off, nb, l, nseq, tc, lb, mats) in passes:
                dmax = min(l // lb - 1, -(-cv_half // lb))
                spec = _filter_spectrum(mats, _conformer_taps(cv_dw_w[i], l), dmax)
                ya_parts.append(_conformer(glu, cv_dw_b[i][None], cv_ln_g[i][None], cv_ln_b[i][None], mats, spec,
                                           off, nb, l, nseq))
            w_out = od_w_out[i].astype(BF16)

        x = _out_proj(tok, ya_parts, yb_parts, x_parts, mods, layer, w_out, ln_g2, ln_b2, tm_in)
        mlp_args = (mods, layer, mlp_w1, mlp_b1[:, None, :], mlp_w2, mlp_b2[:, None, :], ln_g2, ln_b2, tm_mlp, tf_mlp)
        if layer < DEPTH - 1:
            x_parts = (_mlp(tok, x, *mlp_args, 0, tok.t),)
        else:
            y_prompt = _mlp(tok, x, *mlp_args, 0, tok.t_ctx).reshape(bc, lc, d)
            y_sample = _mlp(tok, x, *mlp_args, tok.t_ctx, tok.t_lat).reshape(bl, ll, d)

    return (y_prompt, y_sample, jnp.stack(ckv_list, axis=1), jnp.stack(krope_list, axis=1),
            jnp.stack(k_list, axis=1), jnp.stack(v_list, axis=1))
```

```python
import functools
import math

import jax
import jax.numpy as jnp
import numpy as np
from jax import lax
from jax.experimental import pallas as pl
from jax.experimental.pallas import tpu as pltpu

F32 = jnp.float32
BF16 = jnp.bfloat16

DEPTH = 4
GRID_W = 64
ALPHA = (2.0 * DEPTH) ** 0.25
LN_EPS = 1e-5
RMS_EPS = 1e-6
ROPE_THETA = 10000.0

HY_DIM = 512
HY_ORDER = 2
HY_BANDS = 16
HY_EMB = 2 * HY_BANDS + 1
HY_FILT_HID = 64

MLA_HEADS = 8
MLA_NOPE = 64
MLA_ROPE = 32
MLA_V = 64
MLA_Q_RANK = 256
MLA_KV_RANK = 128

CV_DIM = 512

GQA_HEADS = 8
GQA_KV_HEADS = 2
GQA_HD = 64

LANES = 128
CONV_BLOCK = 512
MIB = 2 ** 20


def _cparams(sem, vmem_mib):
    return pltpu.CompilerParams(dimension_semantics=sem, vmem_limit_bytes=vmem_mib * MIB)


def _layer_norm(z, g, b):
    mu = jnp.mean(z, axis=-1, keepdims=True)
    zc = z - mu
    var = jnp.mean(zc * zc, axis=-1, keepdims=True)
    return zc * lax.rsqrt(var + LN_EPS) * g + b


def _rms(x, g):
    return x * lax.rsqrt(jnp.mean(x * x, axis=-1, keepdims=True) + RMS_EPS) * g


def _seg_mean(sq, s_mat):
    hi = sq.astype(BF16)
    lo = (sq - hi.astype(F32)).astype(BF16)
    return (jnp.dot(hi, s_mat, preferred_element_type=F32)
            + jnp.dot(lo, s_mat, preferred_element_type=F32))


def _rope(x, cos, sin_signed, half):
    w = x.shape[1]
    lane = lax.broadcasted_iota(jnp.int32, x.shape, 1)
    first = jnp.bitwise_and(lane, 2 * half - 1) < half
    rot = jnp.where(first, pltpu.roll(x, w - half, 1), pltpu.roll(x, half, 1))
    return x * cos + rot * sin_signed


def _ada_kernel(c_ref, w_ref, b_ref, o_ref):
    c = c_ref[...]
    s = (c * jax.nn.sigmoid(c)).astype(BF16)
    o_ref[0] = jnp.dot(s, w_ref[0].astype(BF16), preferred_element_type=F32) + b_ref[0]


def _ada(cond8, ada_w, ada_b):
    depth, d, n = ada_w.shape
    tn = 1536
    return pl.pallas_call(
        _ada_kernel,
        grid=(depth, n // tn),
        in_specs=[pl.BlockSpec((8, d), lambda l, j: (0, 0)),
                  pl.BlockSpec((1, d, tn), lambda l, j: (l, 0, j)),
                  pl.BlockSpec((1, 1, tn), lambda l, j: (l, 0, j))],
        out_specs=pl.BlockSpec((1, 8, tn), lambda l, j: (l, 0, j)),
        out_shape=jax.ShapeDtypeStruct((depth, 8, n), F32),
        compiler_params=_cparams(("arbitrary", "arbitrary"), 32),
        name="ada_mod",
    )(cond8, ada_w, ada_b.reshape(depth, 1, n))


class _Tok:
    def __init__(self, bc, lc, bl, ll):
        self.bc, self.lc, self.bl, self.ll = bc, lc, bl, ll
        self.t_ctx = bc * lc
        self.t_lat = bl * ll
        self.t = self.t_ctx + self.t_lat

    def cond_map(self, tm):
        n_ctx, per_b = self.t_ctx // tm, self.ll // tm
        return lambda m: jnp.where(m < n_ctx, 0, 1 + (m - n_ctx) // per_b)

    def rope_map(self, tm):
        n_ctx, per_b = self.t_ctx // tm, self.ll // tm
        return lambda m: jnp.where(m < n_ctx, 0, 1 + (m - n_ctx) % per_b)

    def part_specs(self, tm, n_parts, width):
        if n_parts == 1:
            return [pl.BlockSpec((tm, width), lambda m: (m, 0))]
        n_ctx = self.t_ctx // tm
        return [pl.BlockSpec((tm, width), lambda m: (jnp.minimum(m, n_ctx - 1), 0)),
                pl.BlockSpec((tm, width), lambda m: (jnp.maximum(m - n_ctx, 0), 0))]


def _read_parts(refs, n_ctx_tiles, rows=slice(None)):
    if len(refs) == 1:
        return refs[0][rows, :]
    return jnp.where(pl.program_id(0) < n_ctx_tiles, refs[0][rows, :], refs[1][rows, :])


SUB_ROWS = 256


def _cast_weights_once(w_ref, w_scr, row0=0):
    @pl.when(pl.program_id(0) == 0)
    def _():
        k, n = w_ref.shape
        w_scr[row0:row0 + k, :n] = w_ref[...].astype(BF16)
        if n < w_scr.shape[1]:
            w_scr[row0:row0 + k, n:] = jnp.zeros((k, w_scr.shape[1] - n), BF16)


def _rope_tables(ll, tm, n_heads, head_w, rope_off, rope_dim, scale):
    half = rope_dim // 2
    quarter = half // 2
    t = np.arange(ll)
    row, col = t // GRID_W, t % GRID_W
    inv = ROPE_THETA ** (-np.arange(0, half, 2, dtype=np.float64) / half)
    ang_r = row[:, None] * inv[None, :]
    ang_c = col[:, None] * inv[None, :]
    ang = np.concatenate([ang_r, ang_r, ang_c, ang_c], axis=-1)
    sign = np.concatenate([-np.ones(quarter), np.ones(quarter)] * 2)
    pad = ((0, 0), (rope_off, head_w - rope_off - rope_dim))
    cos_l = np.tile(np.pad(np.cos(ang), pad, constant_values=1.0), (1, n_heads))
    sin_l = np.tile(np.pad(np.sin(ang) * sign, pad), (1, n_heads))
    w = n_heads * head_w
    cos = np.concatenate([np.ones((tm, w)), cos_l], axis=0) * scale
    sin = np.concatenate([np.zeros((tm, w)), sin_l], axis=0) * scale
    return jnp.asarray(cos, F32), jnp.asarray(sin, F32)


EV_W = 2048


def _even_in_kernel(*refs, n_x, n_ctx_tiles):
    x_refs = refs[:n_x]
    (mod_ref, w_ref, gq_ref, wuq_ref, gkv_ref, cq_ref, sq_ref, ck_ref, sk_ref,
     uhy_ref, q_ref, ckr_ref, w_scr) = refs[n_x:]
    n_hy = 3 * HY_DIM
    o = n_hy + MLA_Q_RANK
    _cast_weights_once(w_ref, w_scr)
    for r in range(uhy_ref.shape[0] // SUB_ROWS):
        rows = slice(r * SUB_ROWS, (r + 1) * SUB_ROWS)
        x = _read_parts(x_refs, n_ctx_tiles, rows)
        h = (x * (1.0 + mod_ref[0, 1:2, :]) + mod_ref[0, 0:1, :]).astype(BF16)
        u = jnp.dot(h, w_scr[...], preferred_element_type=F32)
        uhy_ref[rows, :] = u[:, :n_hy]
        cqn = _rms(u[:, n_hy:n_hy + MLA_Q_RANK], gq_ref[...])
        q = jnp.dot(cqn.astype(BF16), wuq_ref[...], preferred_element_type=F32)
        cq, sq = cq_ref[rows, :], sq_ref[rows, :]
        for hd in range(MLA_HEADS):
            cols = slice(hd * LANES, (hd + 1) * LANES)
            q_ref[rows, cols] = _rope(q[:, cols], cq, sq, MLA_ROPE // 4).astype(BF16)
        ckr_ref[rows, :MLA_KV_RANK] = _rms(u[:, o:o + MLA_KV_RANK], gkv_ref[...])
        ckr_ref[rows, MLA_KV_RANK:] = _rope(u[:, o + MLA_KV_RANK:], ck_ref[rows, :], sk_ref[rows, :], MLA_ROPE // 4)


def _even_in(tok, x_parts, mods, layer, w_in, gq, wuq_p, gkv, tabs, tm):
    cq, sq, ck, sk = tabs
    d = x_parts[0].shape[1]
    n_in = w_in.shape[2]
    cm, rm = tok.cond_map(tm), tok.rope_map(tm)
    qw = MLA_HEADS * LANES
    row = lambda m: (m, 0)
    full = lambda m: (0, 0)
    tab = pl.BlockSpec((tm, LANES), lambda m: (rm(m), 0))
    return pl.pallas_call(
        functools.partial(_even_in_kernel, n_x=len(x_parts), n_ctx_tiles=tok.t_ctx // tm),
        grid=(tok.t // tm,),
        in_specs=tok.part_specs(tm, len(x_parts), d) + [
            pl.BlockSpec((None, 1, 6, d), lambda m: (layer, cm(m), 0, 0)),
            pl.BlockSpec((None, d, n_in), lambda m: (layer // 2, 0, 0), pipeline_mode=pl.Buffered(1)),
            pl.BlockSpec((1, MLA_Q_RANK), full),
            pl.BlockSpec((MLA_Q_RANK, qw), full),
            pl.BlockSpec((1, MLA_KV_RANK), full),
            tab, tab, tab, tab],
        out_specs=[pl.BlockSpec((tm, 3 * HY_DIM), row),
                   pl.BlockSpec((tm, qw), row),
                   pl.BlockSpec((tm, 2 * LANES), row)],
        out_shape=[jax.ShapeDtypeStruct((tok.t, 3 * HY_DIM), F32),
                   jax.ShapeDtypeStruct((tok.t, qw), BF16),
                   jax.ShapeDtypeStruct((tok.t, 2 * LANES), F32)],
        scratch_shapes=[pltpu.VMEM((d, EV_W), BF16)],
        compiler_params=_cparams(("arbitrary",), 48),
        name="even_in",
    )(*x_parts, mods, w_in, gq, wuq_p, gkv, cq, sq, ck, sk)


def _odd_in_kernel(x_ref, mod_ref, w_ref, gq_ref, gk_ref, seg_ref, cq_ref, sq_ref, ck_ref, sk_ref,
                   glu_ref, q_ref, kv_ref, w_scr):
    _cast_weights_once(w_ref, w_scr)
    n_q = GQA_HEADS * GQA_HD
    n_k = GQA_KV_HEADS * GQA_HD
    o = 2 * CV_DIM + n_q
    seg = seg_ref[...]
    for r in range(glu_ref.shape[0] // SUB_ROWS):
        rows = slice(r * SUB_ROWS, (r + 1) * SUB_ROWS)
        h = (x_ref[rows, :] * (1.0 + mod_ref[0, 1:2, :]) + mod_ref[0, 0:1, :]).astype(BF16)
        u = jnp.dot(h, w_scr[...], preferred_element_type=F32)
        glu_ref[rows, :] = (u[:, :CV_DIM] * jax.nn.sigmoid(u[:, CV_DIM:2 * CV_DIM])).astype(BF16)
        gq, cq, sq = gq_ref[...], cq_ref[rows, :], sq_ref[rows, :]
        for j in range(n_q // LANES):
            q = u[:, 2 * CV_DIM + j * LANES:2 * CV_DIM + (j + 1) * LANES]
            qn = q * lax.rsqrt(_seg_mean(q * q, seg) + RMS_EPS) * gq
            q_ref[rows, j * LANES:(j + 1) * LANES] = _rope(qn, cq, sq, GQA_HD // 4).astype(BF16)
        k = u[:, o:o + n_k]
        kn = k * lax.rsqrt(_seg_mean(k * k, seg) + RMS_EPS) * gk_ref[...]
        kv_ref[rows, :n_k] = _rope(kn, ck_ref[rows, :], sk_ref[rows, :], GQA_HD // 4)
        kv_ref[rows, n_k:] = u[:, o + n_k:]


def _odd_in(tok, x, mods, layer, w_in, gq, gk, tabs, tm):
    cq, sq, ck, sk = tabs
    _, d, n = w_in.shape
    n_q = GQA_HEADS * GQA_HD
    n_k = GQA_KV_HEADS * GQA_HD
    cm, rm = tok.cond_map(tm), tok.rope_map(tm)
    i = np.arange(LANES)
    seg = jnp.asarray(np.where((i[:, None] // GQA_HD) == (i[None, :] // GQA_HD), 1.0 / GQA_HD, 0.0), BF16)
    row = lambda m: (m, 0)
    full = lambda m: (0, 0)
    tab = pl.BlockSpec((tm, LANES), lambda m: (rm(m), 0))
    return pl.pallas_call(
        _odd_in_kernel,
        grid=(tok.t // tm,),
        in_specs=[pl.BlockSpec((tm, d), row),
                  pl.BlockSpec((None, 1, 6, d), lambda m: (layer, cm(m), 0, 0)),
                  pl.BlockSpec((None, d, n), lambda m: (layer // 2, 0, 0), pipeline_mode=pl.Buffered(1)),
                  pl.BlockSpec((1, LANES), full),
                  pl.BlockSpec((1, LANES), full),
                  pl.BlockSpec((LANES, LANES), full),
                  tab, tab, tab, tab],
        out_specs=[pl.BlockSpec((tm, CV_DIM), row),
                   pl.BlockSpec((tm, n_q), row),
                   pl.BlockSpec((tm, 2 * n_k), row)],
        out_shape=[jax.ShapeDtypeStruct((tok.t, CV_DIM), BF16),
                   jax.ShapeDtypeStruct((tok.t, n_q), BF16),
                   jax.ShapeDtypeStruct((tok.t, 2 * n_k), F32)],
        scratch_shapes=[pltpu.VMEM((d, n), BF16)],
        compiler_params=_cparams(("arbitrary",), 48),
        name="odd_in",
    )(x, mods, w_in, gq, gk, seg, cq, sq, ck, sk)


LOG2E = math.log2(math.e)
SUM_LANE = (LANES // 2, 0)
LONG_KEYS = 1024


def _softmax_pv(q, kt, v, sum_lane):
    s = jnp.dot(q, kt, preferred_element_type=F32)
    p = jnp.exp2(s - jnp.max(s, axis=-1, keepdims=True))
    pv = jnp.dot(p.astype(BF16), v, preferred_element_type=F32)
    if kt.shape[1] >= LONG_KEYS:
        return pv * (1.0 / pv[:, sum_lane:sum_lane + 1])
    return pv * (1.0 / jnp.sum(p, axis=-1, keepdims=True))


def _pair_out(o0, o1):
    lane = lax.broadcasted_iota(jnp.int32, o0.shape, 1)
    return jnp.where(lane < LANES // 2, o0, o1)


def _ones_lane(shape, lane_idx):
    return (lax.broadcasted_iota(jnp.int32, shape, 1) == lane_idx).astype(F32)


def _mla_attn_kernel(*refs, seg_lens, has_cache, pps):
    if has_cache:
        q_ref, cckv_ref, ckr_ref, own_ref, wk_ref, wv_ref, o_ref, ck_scr, k_scr, v_scr = refs
    else:
        q_ref, own_ref, wk_ref, wv_ref, o_ref, ck_scr, k_scr, v_scr = refs
    npair = MLA_HEADS // 2

    @pl.when((pl.program_id(1) == 0) & (pl.program_id(2) == 0))
    def _():
        off = 0
        if has_cache:
            n = seg_lens[0]
            ck_scr[0:n, :MLA_KV_RANK] = cckv_ref[0, 0].astype(BF16)
            ck_scr[0:n, MLA_KV_RANK:] = jnp.zeros((n, LANES), BF16)
            ck_scr[0:n, MLA_KV_RANK:MLA_KV_RANK + MLA_ROPE] = ckr_ref[0, 0].astype(BF16)
            off = n
        ck_scr[off:off + seg_lens[-1], :] = own_ref[...].astype(BF16)
        ck = ck_scr[...]
        for hp in range(npair):
            k_scr[hp] = lax.dot_general(wk_ref[hp], ck, (((1,), (1,)), ((), ())),
                                        preferred_element_type=F32).astype(BF16)
        for h in range(MLA_HEADS):
            v = jnp.dot(ck[:, :MLA_KV_RANK], wv_ref[h], preferred_element_type=F32)
            v_scr[h] = (v + _ones_lane(v.shape, SUM_LANE[h % 2])).astype(BF16)

    for lp in range(pps):
        gp = lp if pps == npair else pl.program_id(2) * pps + lp
        kp = k_scr[gp]
        q0 = q_ref[:, 2 * lp * LANES:(2 * lp + 1) * LANES]
        q1 = q_ref[:, (2 * lp + 1) * LANES:(2 * lp + 2) * LANES]
        o = _pair_out(_softmax_pv(q0, kp[:LANES, :], v_scr[2 * gp], SUM_LANE[0]),
                      _softmax_pv(q1, kp[LANES:, :], v_scr[2 * gp + 1], SUM_LANE[1]))
        o_ref[:, lp * LANES:(lp + 1) * LANES] = o.astype(o_ref.dtype)


def _mla_attn(q, ckr, wk, wv, tok_off, nb, lq, tq, pps, cache=None):
    nq = lq // tq
    qb0 = tok_off // tq
    ob0 = tok_off // lq
    npair = MLA_HEADS // 2
    seg_lens = (lq,) if cache is None else (cache[0].shape[2], lq)
    lk = sum(seg_lens)
    in_specs = [pl.BlockSpec((tq, 2 * LANES * pps), lambda b, i, p: (qb0 + b * nq + i, p))]
    args = [q]
    if cache is not None:
        cckv, ckr_c, layer = cache
        past = cckv.shape[2]
        in_specs += [pl.BlockSpec((1, 1, past, MLA_KV_RANK), lambda b, i, p: (b, layer, 0, 0)),
                     pl.BlockSpec((1, 1, past, MLA_ROPE), lambda b, i, p: (b, layer, 0, 0))]
        args += [cckv, ckr_c]
    in_specs += [pl.BlockSpec((lq, 2 * LANES), lambda b, i, p: (ob0 + b, 0)),
                 pl.BlockSpec((npair, 2 * LANES, 2 * LANES), lambda b, i, p: (0, 0, 0)),
                 pl.BlockSpec((MLA_HEADS, MLA_KV_RANK, LANES), lambda b, i, p: (0, 0, 0))]
    args += [ckr, wk, wv]
    return pl.pallas_call(
        functools.partial(_mla_attn_kernel, seg_lens=seg_lens, has_cache=cache is not None, pps=pps),
        grid=(nb, nq, npair // pps),
        in_specs=in_specs,
        out_specs=pl.BlockSpec((tq, LANES * pps), lambda b, i, p: (b * nq + i, p)),
        out_shape=jax.ShapeDtypeStruct((nb * lq, MLA_HEADS * MLA_V), BF16),
        scratch_shapes=[pltpu.VMEM((lk, 2 * LANES), BF16),
                        pltpu.VMEM((npair, 2 * LANES, lk), BF16),
                        pltpu.VMEM((MLA_HEADS, lk, LANES), BF16)],
        compiler_params=_cparams(("arbitrary", "arbitrary", "arbitrary"), 48),
        name="mla_attn_lat" if cache is not None else "mla_attn_ctx",
    )(*args)


def _gqa_attn_kernel(*refs, seg_lens, has_cache, pps):
    if has_cache:
        q_ref, ck_ref, cv_ref, own_ref, sel_ref, selt_ref, o_ref, kv_scr, k_scr, v_scr = refs
    else:
        q_ref, own_ref, sel_ref, selt_ref, o_ref, kv_scr, k_scr, v_scr = refs
    n_k = GQA_KV_HEADS * GQA_HD
    npair = GQA_HEADS // 2
    group_pairs = GQA_HEADS // GQA_KV_HEADS // 2

    @pl.when((pl.program_id(1) == 0) & (pl.program_id(2) == 0))
    def _():
        off = 0
        if has_cache:
            n = seg_lens[0]
            kv_scr[0:n, :n_k] = ck_ref[0, 0].astype(BF16)
            kv_scr[0:n, n_k:] = cv_ref[0, 0].astype(BF16)
            off = n
        kv_scr[off:off + seg_lens[-1], :] = own_ref[...].astype(BF16)
        kk = kv_scr[:, :n_k]
        vv = kv_scr[:, n_k:]
        for s in range(2 * GQA_KV_HEADS):
            k_scr[s] = lax.dot_general(selt_ref[s], kk, (((1,), (1,)), ((), ())),
                                       preferred_element_type=F32).astype(BF16)
            v = jnp.dot(vv, sel_ref[s], preferred_element_type=F32)
            v_scr[s] = (v + _ones_lane(v.shape, SUM_LANE[s % 2])).astype(BF16)

    for lp in range(pps):
        gp = lp if pps == npair else pl.program_id(2) * pps + lp
        kvh = gp // group_pairs
        q = q_ref[:, lp * LANES:(lp + 1) * LANES]
        o = _pair_out(_softmax_pv(q, k_scr[2 * kvh], v_scr[2 * kvh], SUM_LANE[0]),
                      _softmax_pv(q, k_scr[2 * kvh + 1], v_scr[2 * kvh + 1], SUM_LANE[1]))
        o_ref[:, lp * LANES:(lp + 1) * LANES] = o.astype(o_ref.dtype)


def _gqa_attn(q, kv, tok_off, nb, lq, tq, pps, cache=None):
    nq = lq // tq
    qb0 = tok_off // tq
    ob0 = tok_off // lq
    npair = GQA_HEADS // 2
    n_k = GQA_KV_HEADS * GQA_HD
    seg_lens = (lq,) if cache is None else (cache[0].shape[2], lq)
    lk = sum(seg_lens)
    src = np.arange(n_k)[:, None]
    dst = np.arange(n_k)[None, :]
    sel_np = np.stack([np.where((src // GQA_HD == kvh) & (dst // GQA_HD == i) & (src % GQA_HD == dst % GQA_HD), 1.0, 0.0)
                       for kvh in range(GQA_KV_HEADS) for i in range(2)])
    sel, sel_t = jnp.asarray(sel_np, BF16), jnp.asarray(sel_np.transpose(0, 2, 1), BF16)
    in_specs = [pl.BlockSpec((tq, LANES * pps), lambda b, i, p: (qb0 + b * nq + i, p))]
    args = [q]
    if cache is not None:
        ck, cv, layer = cache
        past = ck.shape[2]
        in_specs += [pl.BlockSpec((1, 1, past, n_k), lambda b, i, p: (b, layer, 0, 0)),
                     pl.BlockSpec((1, 1, past, n_k), lambda b, i, p: (b, layer, 0, 0))]
        args += [ck, cv]
    selspec = pl.BlockSpec((2 * GQA_KV_HEADS, n_k, n_k), lambda b, i, p: (0, 0, 0))
    in_specs += [pl.BlockSpec((lq, 2 * n_k), lambda b, i, p: (ob0 + b, 0)), selspec, selspec]
    args += [kv, sel, sel_t]
    return pl.pallas_call(
        functools.partial(_gqa_attn_kernel, seg_lens=seg_lens, has_cache=cache is not None, pps=pps),
        grid=(nb, nq, npair // pps),
        in_specs=in_specs,
        out_specs=pl.BlockSpec((tq, LANES * pps), lambda b, i, p: (b * nq + i, p)),
        out_shape=jax.ShapeDtypeStruct((nb * lq, GQA_HEADS * GQA_HD), BF16),
        scratch_shapes=[pltpu.VMEM((lk, 2 * n_k), BF16),
                        pltpu.VMEM((2 * GQA_KV_HEADS, n_k, lk), BF16),
                        pltpu.VMEM((2 * GQA_KV_HEADS, lk, n_k), BF16)],
        compiler_params=_cparams(("arbitrary", "arbitrary", "arbitrary"), 48),
        name="gqa_attn_lat" if cache is not None else "gqa_attn_ctx",
    )(*args)


def _dft_mats(lb):
    n = 2 * lb
    r = np.arange(n)
    nyq = r == lb
    f = np.where(nyq, lb, r % lb)
    is_im = (r >= lb) & ~nyq
    s = np.arange(lb)

    def mat(pos):
        ang = ((f[:, None] * pos[None, :]) % n) * (2.0 * math.pi / n)
        return np.where(is_im[:, None], -np.sin(ang), np.cos(ang))

    fwd = mat(s)
    fwd_rev = np.where(s[None, :] == 0, 0.0, mat(lb - s))
    wgt = np.where((r == 0) | nyq, 1.0 / n, 2.0 / n)
    inv = (fwd * wgt[:, None]).T
    return tuple(jnp.asarray(m.astype(np.float32)).astype(BF16) for m in (fwd, inv, fwd_rev))


def _spec_kernel(f_ref, fr_ref, ka_ref, kb_ref, p_ref, q_ref, p2_ref, *, lb):
    tf = jnp.dot(f_ref[...], ka_ref[...], preferred_element_type=F32)
    tb = jnp.dot(fr_ref[...], kb_ref[...], preferred_element_type=F32)
    re = tf[:lb] + tb[:lb]
    im = tf[lb:] - tb[lb:]
    nyq = tf[lb:] + tb[lb:]
    row0 = lax.broadcasted_iota(jnp.int32, re.shape, 0) == 0
    p_ref[0] = re
    q_ref[0] = jnp.where(row0, 0.0, im)
    p2_ref[0] = jnp.where(row0, nyq, re)


def _filter_spectrum(mats, k2, dmax):
    fwd_mat, _, fwd_rev = mats
    n, lb = fwd_mat.shape
    cw = k2.shape[1]
    nbk = k2.shape[0] // n
    nd = 2 * dmax + 1
    tc = 512
    out = jax.ShapeDtypeStruct((nd, lb, cw), F32)
    blk = pl.BlockSpec((1, lb, tc), lambda c, d: (d, 0, c))
    mat = pl.BlockSpec((n, lb), lambda c, d: (0, 0))
    return pl.pallas_call(
        functools.partial(_spec_kernel, lb=lb),
        grid=(cw // tc, nd),
        in_specs=[mat, mat,
                  pl.BlockSpec((lb, tc), lambda c, d: (nbk + d - dmax, c)),
                  pl.BlockSpec((lb, tc), lambda c, d: (nbk + d - dmax - 1, c))],
        out_specs=[blk, blk, blk],
        out_shape=[out, out, out],
        compiler_params=_cparams(("arbitrary", "arbitrary"), 32),
        name="filter_spectrum",
    )(fwd_mat, fwd_rev, k2, k2)


def _block_conv(zbf_ref, z_scr, f_ref, fi_ref, p_ref, q_ref, p2_ref, lb, nblk, nbk, dmax, emit):
    for r in range(nblk):
        z_scr[r] = jnp.dot(f_ref[...], zbf_ref[r * lb:(r + 1) * lb, :], preferred_element_type=F32)
    for r in range(nblk):
        s, i = divmod(r, nbk)
        ya = yb = None
        for j in range(nbk):
            d = i - j
            if abs(d) > dmax:
                continue
            re = z_scr[s * nbk + j, :lb, :]
            im = z_scr[s * nbk + j, lb:, :]
            p, q, p2 = p_ref[d + dmax], q_ref[d + dmax], p2_ref[d + dmax]
            ta = re * p - im * q
            tb = re * q + im * p2
            ya = ta if ya is None else ya + ta
            yb = tb if yb is None else yb + tb
        y = jnp.concatenate([ya, yb], axis=0).astype(BF16)
        emit(r, jnp.dot(fi_ref[...], y, preferred_element_type=F32))


def _short_conv(u, w_ref, b_ref, l):
    rows = u.shape[0]
    t = jnp.bitwise_and(lax.broadcasted_iota(jnp.int32, u.shape, 0), l - 1)
    prev = jnp.where(t == 0, 0.0, pltpu.roll(u, 1, 0))
    nxt = jnp.where(t == l - 1, 0.0, pltpu.roll(u, rows - 1, 0))
    return w_ref[0:1, :] * prev + w_ref[1:2, :] * u + w_ref[2:3, :] * nxt + b_ref[...]


def _hyena_kernel(*refs, l, lb, nseq, dmax, conv_z):
    if conv_z:
        (uz_ref, ug_ref, cwz_ref, cbz_ref, cwg_ref, cbg_ref, skip_ref,
         f_ref, fi_ref, p_ref, q_ref, p2_ref, o_ref, zf_scr, g_scr, zbf_scr, z_scr) = refs
    else:
        (uz_ref, ug_ref, cwg_ref, cbg_ref, skip_ref,
         f_ref, fi_ref, p_ref, q_ref, p2_ref, o_ref, zf_scr, g_scr, zbf_scr, z_scr) = refs
    z = _short_conv(uz_ref[...], cwz_ref, cbz_ref, l) if conv_z else uz_ref[...]
    zf_scr[...] = z
    zbf_scr[...] = z.astype(BF16)
    g_scr[...] = _short_conv(ug_ref[...], cwg_ref, cbg_ref, l)
    skip = skip_ref[...]

    def emit(r, y):
        rows = slice(r * lb, (r + 1) * lb)
        o_ref[rows, :] = (g_scr[rows, :] * (y + skip * zf_scr[rows, :])).astype(o_ref.dtype)

    nbk = l // lb
    _block_conv(zbf_scr, z_scr, f_ref, fi_ref, p_ref, q_ref, p2_ref, lb, nseq * nbk, nbk, dmax, emit)


def _hyena_stage(z_src, z_col, u_hy, g_col, conv_w, conv_b, skip, mats, spec, spec_col,
                 tok_off, nb, l, nseq, tc, conv_z, out_dtype):
    fwd_mat, inv_mat, _ = mats
    p_arr, q_arr, p2_arr = spec
    n, lb = fwd_mat.shape
    nd = p_arr.shape[0]
    dmax = nd // 2
    rows = nseq * l
    rb0 = tok_off // rows
    zrb0 = rb0 if conv_z else 0
    cpb = HY_DIM // tc
    cvec = lambda g: pl.BlockSpec((1, tc), lambda c, b: (0, g * cpb + c))
    in_specs = [pl.BlockSpec((rows, tc), lambda c, b: (zrb0 + b, z_col * cpb + c)),
                pl.BlockSpec((rows, tc), lambda c, b: (rb0 + b, g_col * cpb + c))]
    args = [z_src, u_hy]
    if conv_z:
        in_specs += [pl.BlockSpec((3, tc), lambda c, b: (0, z_col * cpb + c)), cvec(z_col)]
        args += [conv_w, conv_b]
    in_specs += [pl.BlockSpec((3, tc), lambda c, b: (0, g_col * cpb + c)), cvec(g_col),
                 pl.BlockSpec((1, tc), lambda c, b: (0, c)),
                 pl.BlockSpec((n, lb), lambda c, b: (0, 0)),
                 pl.BlockSpec((lb, n), lambda c, b: (0, 0))]
    args += [conv_w, conv_b, skip, fwd_mat, inv_mat]
    sspec = pl.BlockSpec((nd, lb, tc), lambda c, b: (0, 0, spec_col * cpb + c), pipeline_mode=pl.Buffered(1))
    in_specs += [sspec, sspec, sspec]
    args += [p_arr, q_arr, p2_arr]
    nblk = rows // lb
    return pl.pallas_call(
        functools.partial(_hyena_kernel, l=l, lb=lb, nseq=nseq, dmax=dmax, conv_z=conv_z),
        grid=(HY_DIM // tc, nb // nseq),
        in_specs=in_specs,
        out_specs=pl.BlockSpec((rows, tc), lambda c, b: (b, c)),
        out_shape=jax.ShapeDtypeStruct((nb * l, HY_DIM), out_dtype),
        scratch_shapes=[pltpu.VMEM((rows, tc), F32), pltpu.VMEM((rows, tc), F32),
                        pltpu.VMEM((rows, tc), BF16), pltpu.VMEM((nblk, n, tc), F32)],
        compiler_params=_cparams(("arbitrary", "arbitrary"), 56),
        name="hyena_stage",
    )(*args)


def _conformer_kernel(glu_ref, b_ref, lng_ref, lnb_ref, f_ref, fi_ref, p_ref, q_ref, p2_ref, o_ref, z_scr,
                      *, l, lb, nseq, dmax):
    bias, lng, lnb = b_ref[...], lng_ref[...], lnb_ref[...]

    def emit(r, y):
        yn = _layer_norm(y + bias, lng, lnb)
        o_ref[r * lb:(r + 1) * lb, :] = (yn * jax.nn.sigmoid(yn)).astype(o_ref.dtype)

    nbk = l // lb
    _block_conv(glu_ref, z_scr, f_ref, fi_ref, p_ref, q_ref, p2_ref, lb, nseq * nbk, nbk, dmax, emit)


def _conformer(glu, bias, ln_g, ln_b, mats, spec, tok_off, nb, l, nseq):
    fwd_mat, inv_mat, _ = mats
    n, lb = fwd_mat.shape
    nd = spec[0].shape[0]
    rows = nseq * l
    rb0 = tok_off // rows
    c = CV_DIM
    vec = pl.BlockSpec((1, c), lambda b: (0, 0))
    sspec = pl.BlockSpec((nd, lb, c), lambda b: (0, 0, 0))
    return pl.pallas_call(
        functools.partial(_conformer_kernel, l=l, lb=lb, nseq=nseq, dmax=nd // 2),
        grid=(nb // nseq,),
        in_specs=[pl.BlockSpec((rows, c), lambda b: (rb0 + b, 0)),
                  vec, vec, vec,
                  pl.BlockSpec((n, lb), lambda b: (0, 0)),
                  pl.BlockSpec((lb, n), lambda b: (0, 0)),
                  sspec, sspec, sspec],
        out_specs=pl.BlockSpec((rows, c), lambda b: (b, 0)),
        out_shape=jax.ShapeDtypeStruct((nb * l, c), BF16),
        scratch_shapes=[pltpu.VMEM((rows // lb, n, c), F32)],
        compiler_params=_cparams(("arbitrary",), 56),
        name="conformer",
    )(glu, bias, ln_g, ln_b, fwd_mat, inv_mat, *spec)


def _dot_bf16x3(a, b):
    ah = a.astype(BF16)
    al = (a - ah.astype(F32)).astype(BF16)
    bh = b.astype(BF16)
    bl = (b - bh.astype(F32)).astype(BF16)
    return (jnp.dot(ah, bh, preferred_element_type=F32) + jnp.dot(al, bh, preferred_element_type=F32)
            + jnp.dot(ah, bl, preferred_element_type=F32))


def _filter_kernel(z_ref, w1_ref, b1_ref, w2_ref, b2_ref, w3_ref, freq_ref, decay_ref, o_ref):
    hp = lax.Precision.HIGHEST
    z = z_ref[...]
    freq = freq_ref[...]
    hid = jnp.sin(freq * (jnp.dot(z, w1_ref[...], precision=hp, preferred_element_type=F32) + b1_ref[...]))
    hid = jnp.sin(freq * (jnp.dot(hid, w2_ref[...], precision=hp, preferred_element_type=F32) + b2_ref[...]))
    h = _dot_bf16x3(hid, w3_ref[...])
    h = h * jnp.exp(-z[:, 0:1] * jnp.abs(decay_ref[...]))
    row = lax.broadcasted_iota(jnp.int32, h.shape, 0) + pl.program_id(0) * h.shape[0]
    o_ref[...] = jnp.where(row == 0, 0.0, h).astype(o_ref.dtype)


def _hyena_filter(l, w1, b1, w2, b2, w3, freq, decay):
    t = np.abs(np.arange(2 * l) - l) / l
    ang = 2.0 * math.pi * t[:, None] * np.arange(1, HY_BANDS + 1)[None, :]
    z = jnp.asarray(np.concatenate([t[:, None], np.cos(ang), np.sin(ang), np.zeros((2 * l, LANES - HY_EMB))], axis=-1), F32)
    w1p = jnp.concatenate([w1, jnp.zeros((LANES - HY_EMB, HY_FILT_HID), F32)], axis=0)
    nout = w3.shape[1] // 2
    tl = 256
    n_bwd = l // tl
    full = lambda i: (0, 0)
    half = lambda i: (0, jnp.where(i < n_bwd, 1, 0))
    return pl.pallas_call(
        _filter_kernel,
        grid=(2 * l // tl,),
        in_specs=[pl.BlockSpec((tl, LANES), lambda i: (i, 0)),
                  pl.BlockSpec((LANES, HY_FILT_HID), full),
                  pl.BlockSpec((1, HY_FILT_HID), full),
                  pl.BlockSpec((HY_FILT_HID, HY_FILT_HID), full),
                  pl.BlockSpec((1, HY_FILT_HID), full),
                  pl.BlockSpec((HY_FILT_HID, nout), half),
                  pl.BlockSpec((1, HY_FILT_HID), full),
                  pl.BlockSpec((1, nout), half)],
        out_specs=pl.BlockSpec((tl, nout), lambda i: (i, 0)),
        out_shape=jax.ShapeDtypeStruct((2 * l, nout), BF16),
        compiler_params=_cparams(("arbitrary",), 32),
        name="hyena_filter",
    )(z, w1p, b1[None], w2, b2[None], w3, freq[None], decay[None])


def _out_proj_kernel(*refs, n_x, n_ctx_tiles):
    ya_refs, yb_refs, x_refs = refs[0:2], refs[2:4], refs[4:4 + n_x]
    mod_ref, wa_ref, wb_ref, g_ref, b_ref, o_ref, w_scr = refs[4 + n_x:]
    ka = wa_ref.shape[0]
    _cast_weights_once(wa_ref, w_scr)
    _cast_weights_once(wb_ref, w_scr, row0=ka)
    for r in range(o_ref.shape[0] // SUB_ROWS):
        rows = slice(r * SUB_ROWS, (r + 1) * SUB_ROWS)
        y = (jnp.dot(_read_parts(ya_refs, n_ctx_tiles, rows), w_scr[:ka, :], preferred_element_type=F32)
             + jnp.dot(_read_parts(yb_refs, n_ctx_tiles, rows), w_scr[ka:, :], preferred_element_type=F32))
        z = ALPHA * _read_parts(x_refs, n_ctx_tiles, rows) + mod_ref[0, 2:3, :] * y
        o_ref[rows, :] = _layer_norm(z, g_ref[...], b_ref[...])


def _out_proj(tok, ya_parts, yb_parts, x_parts, mods, layer, w_out, ln_g, ln_b, tm):
    d = x_parts[0].shape[1]
    ka, kb = ya_parts[0].shape[1], yb_parts[0].shape[1]
    cm = tok.cond_map(tm)
    return pl.pallas_call(
        functools.partial(_out_proj_kernel, n_x=len(x_parts), n_ctx_tiles=tok.t_ctx // tm),
        grid=(tok.t // tm,),
        in_specs=(tok.part_specs(tm, 2, ka) + tok.part_specs(tm, 2, kb) + tok.part_specs(tm, len(x_parts), d) + [
            pl.BlockSpec((None, 1, 6, d), lambda m: (layer, cm(m), 0, 0)),
            pl.BlockSpec((None, ka, d), lambda m: (layer // 2, 0, 0), pipeline_mode=pl.Buffered(1)),
            pl.BlockSpec((None, kb, d), lambda m: (layer // 2, ka // kb, 0), pipeline_mode=pl.Buffered(1)),
            pl.BlockSpec((None, 1, d), lambda m: (2 * layer, 0, 0)),
            pl.BlockSpec((None, 1, d), lambda m: (2 * layer, 0, 0))]),
        out_specs=pl.BlockSpec((tm, d), lambda m: (m, 0)),
        out_shape=jax.ShapeDtypeStruct((tok.t, d), F32),
        scratch_shapes=[pltpu.VMEM((ka + kb, d), BF16)],
        compiler_params=_cparams(("arbitrary",), 40),
        name="out_proj_ln",
    )(*ya_parts, *yb_parts, *x_parts, mods, w_out, w_out, ln_g, ln_b)


MLP_CHUNK = 512


def _mlp_kernel(x_ref, mod_ref, w1_ref, b1_ref, w2_ref, b2_ref, g_ref, b_ref, o_ref, h_scr, acc_scr, *, nf):
    f = pl.program_id(1)

    @pl.when(f == 0)
    def _():
        h_scr[...] = (x_ref[...] * (1.0 + mod_ref[0, 4:5, :]) + mod_ref[0, 3:4, :]).astype(BF16)
        acc_scr[...] = jnp.zeros_like(acc_scr)

    part = None
    for c0 in range(0, w1_ref.shape[1], MLP_CHUNK):
        cols = slice(c0, c0 + MLP_CHUNK)
        a = jnp.maximum(jnp.dot(h_scr[...], w1_ref[:, cols].astype(BF16), preferred_element_type=F32)
                        + b1_ref[:, cols], 0.0)
        y = jnp.dot((a * a).astype(BF16), w2_ref[cols, :].astype(BF16), preferred_element_type=F32)
        part = y if part is None else part + y
    acc_scr[...] += part

    @pl.when(f == nf - 1)
    def _():
        z = ALPHA * x_ref[...] + mod_ref[0, 5:6, :] * (acc_scr[...] + b2_ref[...])
        o_ref[...] = _layer_norm(z, g_ref[...], b_ref[...])


def _mlp(tok, x, mods, layer, w1, b1, w2, b2, ln_g, ln_b, tm, tf, tok_off, n_rows):
    d = x.shape[1]
    dff = w1.shape[2]
    nf = dff // tf
    m0 = tok_off // tm
    cm = tok.cond_map(tm)
    return pl.pallas_call(
        functools.partial(_mlp_kernel, nf=nf),
        grid=(n_rows // tm, nf),
        in_specs=[pl.BlockSpec((tm, d), lambda m, f: (m0 + m, 0)),
                  pl.BlockSpec((None, 1, 6, d), lambda m, f: (layer, cm(m0 + m), 0, 0)),
                  pl.BlockSpec((None, d, tf), lambda m, f: (layer, 0, f)),
                  pl.BlockSpec((None, 1, tf), lambda m, f: (layer, 0, f)),
                  pl.BlockSpec((None, tf, d), lambda m, f: (layer, f, 0)),
                  pl.BlockSpec((None, 1, d), lambda m, f: (layer, 0, 0)),
                  pl.BlockSpec((None, 1, d), lambda m, f: (2 * layer + 1, 0, 0)),
                  pl.BlockSpec((None, 1, d), lambda m, f: (2 * layer + 1, 0, 0))],
        out_specs=pl.BlockSpec((tm, d), lambda m, f: (m, 0)),
        out_shape=jax.ShapeDtypeStruct((n_rows, d), F32),
        scratch_shapes=[pltpu.VMEM((tm, d), BF16), pltpu.VMEM((tm, d), F32)],
        compiler_params=_cparams(("arbitrary", "arbitrary"), 56),
        name="mlp_ln",
    )(x, mods, w1, b1, w2, b2, ln_g, ln_b)


def _mla_weights(w_uq, w_ukv):
    hd = MLA_NOPE + MLA_ROPE
    wq = w_uq.reshape(MLA_Q_RANK, MLA_HEADS, hd)
    wq = jnp.pad(wq, ((0, 0), (0, 0), (0, LANES - hd))).reshape(MLA_Q_RANK, MLA_HEADS * LANES)
    wkv = w_ukv.reshape(MLA_KV_RANK, MLA_HEADS, MLA_NOPE + MLA_V).transpose(1, 0, 2)
    w_nope, w_v = wkv[..., :MLA_NOPE], wkv[..., MLA_NOPE:]
    top = jnp.pad(w_nope, ((0, 0), (0, 0), (0, LANES - MLA_NOPE)))
    place = jnp.pad(jnp.eye(MLA_ROPE, dtype=F32), ((0, LANES - MLA_ROPE), (MLA_NOPE, LANES - MLA_NOPE - MLA_ROPE)))
    wk = jnp.concatenate([top, jnp.broadcast_to(place, (MLA_HEADS, LANES, LANES))], axis=1)
    wk = wk.transpose(0, 2, 1).reshape(MLA_HEADS // 2, 2 * LANES, 2 * LANES)
    w_v = w_v.reshape(MLA_HEADS // 2, 2, MLA_KV_RANK, MLA_V)
    wv = jnp.stack([jnp.pad(w_v[:, 0], ((0, 0), (0, 0), (0, LANES - MLA_V))),
                    jnp.pad(w_v[:, 1], ((0, 0), (0, 0), (LANES - MLA_V, 0)))], axis=1)
    wv = wv.reshape(MLA_HEADS, MLA_KV_RANK, LANES)
    return wq.astype(BF16), wk.astype(BF16), wv.astype(BF16)


def _conformer_taps(dw_w, l):
    k, c = dw_w.shape
    half = k // 2
    return jnp.pad(dw_w[::-1], ((l - half, l - (k - half)), (0, 0))).astype(BF16)


def kernel(x_prompt, x_sample, cache_mla_ckv, cache_mla_krope, cache_gqa_k, cache_gqa_v, c, c_ctx, ev_w_in, hy_conv_w, hy_conv_b, hy_filt_w1, hy_filt_b1, hy_filt_w2, hy_filt_b2, hy_filt_w3, hy_sin_freq, hy_decay, hy_skip, mla_q_norm_g, mla_w_uq, mla_kv_norm_g, mla_w_ukv, ev_w_out, od_w_in, cv_dw_w, cv_dw_b, cv_ln_g, cv_ln_b, gqa_q_norm_g, gqa_k_norm_g, od_w_out, ada_w, ada_b, ln_g, ln_b, mlp_w1, mlp_b1, mlp_w2, mlp_b2):
    bc, lc, d = x_prompt.shape
    bl, ll, _ = x_sample.shape
    tok = _Tok(bc, lc, bl, ll)
    assert lc & (lc - 1) == 0 and ll & (ll - 1) == 0 and tok.t_ctx % ll == 0 and bl < 8
    n_od = od_w_in.shape[0]
    tm_in = min(512, ll)
    tm_mlp = min(1024, ll)
    tf_mlp = 1024
    tq = 256
    tq_lat = min(256, ll)
    seq_ctx = max(1, 1024 // lc)
    tc_lat = 256 if ll > 1024 else HY_DIM
    cv_half = cv_dw_w.shape[1] // 2

    cond8 = jnp.concatenate([c_ctx[None], c, jnp.zeros((7 - bl, d), F32)], axis=0)
    mods = _ada(cond8, ada_w, ada_b).reshape(DEPTH, 8, 6, d)
    ln_g2 = ln_g.reshape(2 * DEPTH, 1, d)
    ln_b2 = ln_b.reshape(2 * DEPTH, 1, d)

    mla_scale = (MLA_NOPE + MLA_ROPE) ** -0.5 * LOG2E
    ev_tabs = (_rope_tables(ll, tm_in, 1, LANES, MLA_NOPE, MLA_ROPE, mla_scale)
               + _rope_tables(ll, tm_in, 1, LANES, 0, MLA_ROPE, 1.0))
    od_tabs = (_rope_tables(ll, tm_in, LANES // GQA_HD, GQA_HD, 0, GQA_HD, GQA_HD ** -0.5 * LOG2E)
               + _rope_tables(ll, tm_in, LANES // GQA_HD, GQA_HD, 0, GQA_HD, 1.0))
    passes = []
    for off, nb, l, nseq, tc in ((0, bc, lc, seq_ctx, HY_DIM), (tok.t_ctx, bl, ll, 1, tc_lat)):
        lb = min(l, CONV_BLOCK)
        passes.append((off, nb, l, nseq, tc, lb, _dft_mats(lb)))

    x_parts = (x_prompt.reshape(tok.t_ctx, d), x_sample.reshape(tok.t_lat, d))
    ckv_list, krope_list, k_list, v_list = [], [], [], []

    for layer in range(DEPTH):
        i = layer // 2
        if layer % 2 == 0:
            wq, wk, wv = _mla_weights(mla_w_uq[i], mla_w_ukv[i])
            u_hy, q, ckr = _even_in(tok, x_parts, mods, layer, ev_w_in, mla_q_norm_g[i][None], wq, mla_kv_norm_g[i][None],
                                    ev_tabs, tm_in)
            ckv_list.append(ckr[:tok.t_ctx, :MLA_KV_RANK].reshape(bc, lc, MLA_KV_RANK))
            krope_list.append(ckr[:tok.t_ctx, MLA_KV_RANK:MLA_KV_RANK + MLA_ROPE].reshape(bc, lc, MLA_ROPE))
            yb_parts = (_mla_attn(q, ckr, wk, wv, 0, bc, lc, min(tq, lc), MLA_HEADS // 2),
                        _mla_attn(q, ckr, wk, wv, tok.t_ctx, bl, ll, tq_lat, 1,
                                  cache=(cache_mla_ckv, cache_mla_krope, i)))

            skip = hy_skip[i]
            conv_b = hy_conv_b[i][None]
            ya_parts = []
            for (off, nb, l, nseq, tc, lb, mats) in passes:
                k2 = _hyena_filter(l, hy_filt_w1[i], hy_filt_b1[i], hy_filt_w2[i], hy_filt_b2[i],
                                   hy_filt_w3[i], hy_sin_freq[i], hy_decay[i])
                spec = _filter_spectrum(mats, k2, l // lb - 1)
                z1 = _hyena_stage(u_hy, 2, u_hy, 0, hy_conv_w[i], conv_b, skip[0:1], mats, spec, 0,
                                  off, nb, l, nseq, tc, True, F32)
                ya_parts.append(_hyena_stage(z1, 0, u_hy, 1, hy_conv_w[i], conv_b, skip[1:2], mats, spec, 1,
                                             off, nb, l, nseq, tc, False, BF16))
            w_out = ev_w_out
        else:
            (x,) = x_parts
            gq = jnp.tile(gqa_q_norm_g[i], LANES // GQA_HD)[None]
            gk = jnp.tile(gqa_k_norm_g[i], GQA_KV_HEADS)[None]
            glu, q, kv = _odd_in(tok, x, mods, layer, od_w_in, gq, gk, od_tabs, tm_in)
            n_k = GQA_KV_HEADS * GQA_HD
            k_list.append(kv[:tok.t_ctx, :n_k].reshape(bc, lc, GQA_KV_HEADS, GQA_HD))
            v_list.append(kv[:tok.t_ctx, n_k:].reshape(bc, lc, GQA_KV_HEADS, GQA_HD))
            past = cache_gqa_k.shape[2]
            cache = (cache_gqa_k.reshape(bl, n_od, past, n_k), cache_gqa_v.reshape(bl, n_od, past, n_k), i)
            yb_parts = (_gqa_attn(q, kv, 0, bc, lc, min(tq, lc), GQA_HEADS // 2),
                        _gqa_attn(q, kv, tok.t_ctx, bl, ll, tq_lat, 1, cache=cache))

            ya_parts = []
            for (off, nb, l, nseq, tc, lb, mats) in passes:
                dmax = min(l // lb - 1, -(-cv_half // lb))
                spec = _filter_spectrum(mats, _conformer_taps(cv_dw_w[i], l), dmax)
                ya_parts.append(_conformer(glu, cv_dw_b[i][None], cv_ln_g[i][None], cv_ln_b[i][None], mats, spec,
                                           off, nb, l, nseq))
            w_out = od_w_out

        x = _out_proj(tok, ya_parts, yb_parts, x_parts, mods, layer, w_out, ln_g2, ln_b2, tm_in)
        mlp_args = (mods, layer, mlp_w1, mlp_b1[:, None, :], mlp_w2, mlp_b2[:, None, :], ln_g2, ln_b2, tm_mlp, tf_mlp)
        if layer < DEPTH - 1:
            x_parts = (_mlp(tok, x, *mlp_args, 0, tok.t),)
        else:
            y_prompt = _mlp(tok, x, *mlp_args, 0, tok.t_ctx).reshape(bc, lc, d)
            y_sample = _mlp(tok, x, *mlp_args, tok.t_ctx, tok.t_lat).reshape(bl, ll, d)

    return (y_prompt, y_sample, jnp.stack(ckv_list, axis=1), jnp.stack(krope_list, axis=1),
            jnp.stack(k_list, axis=1), jnp.stack(v_list, axis=1))
```
